```python
import jax, jax.numpy as jnp
from jax import lax
import numpy as np

D_MODEL = 1024
BATCH = 8
SEQ = 2048
DEPTH = 4
DEC_BATCH = 128
DEC_SEQ = 1
PAST_LEN = 16384
PAGE_SIZE = 128

D_LRU = D_MODEL // 2
LRU_HEADS = 8
LRU_HD = D_LRU // LRU_HEADS
LRU_C = 8.0
CONV_W = 4
RET_HEADS = 4
RET_DK = D_MODEL // 16
RET_DV = 2 * RET_DK
D_RET = RET_HEADS * RET_DV
ROPE_BASE = 10000.0
SSD_HD = 64
D_SSD = D_MODEL // 2
SSD_HEADS = D_SSD // SSD_HD
SSD_GROUPS = 2
SSD_DSTATE = 128
SSD_CONV_DIM = D_SSD + 2 * SSD_GROUPS * SSD_DSTATE
D_MIX = D_LRU + D_RET + D_SSD
IN_SIZES = (D_LRU, D_LRU, RET_HEADS * RET_DK, RET_HEADS * RET_DK, D_RET, D_RET, D_SSD, SSD_CONV_DIM, SSD_HEADS)
D_IN = 2 * D_LRU + 2 * RET_HEADS * RET_DK + 2 * D_RET + D_SSD + SSD_CONV_DIM + SSD_HEADS
CHUNK = 128
D_FF = 128 * ((8 * D_MODEL // 3 + 127) // 128)
FFN_CONV_W = 3
EPS = 1e-6

kernel_name = 'hybrid_lru_retention_ssd_convffn_step'


def _rmsnorm(x, g):
    xf = x.astype(jnp.float32)
    y = xf * lax.rsqrt(jnp.mean(xf * xf, axis=-1, keepdims=True) + EPS)
    return (y * g.astype(jnp.float32)).astype(x.dtype)


def _split_cols(t, sizes):
    out, start = [], 0
    for s in sizes:
        out.append(t[..., start:start + s])
        start += s
    return out


def _chunk_len(L):
    return CHUNK if L % CHUNK == 0 else L


def _causal_dwconv(x, prev, w, b):
    W = w.shape[0]
    L = x.shape[1]
    xp = jnp.concatenate([prev.astype(x.dtype), x], axis=1)
    y = b
    for k in range(W):
        y = y + w[k] * xp[:, k:k + L]
    return y, xp[:, L:]


def _lin_combine(c1, c2):
    a1, b1 = c1
    a2, b2 = c2
    return a1 * a2, a2 * b1 + b2


def _rg_lru(xc, h0, wr, br, wi, bi, lam):
    f32 = jnp.float32
    bsz, L, _ = xc.shape
    xf = xc.astype(f32)
    xh = xf.reshape(bsz, L, LRU_HEADS, LRU_HD)
    r = jax.nn.sigmoid(jnp.einsum('blhi,hij->blhj', xh, wr.astype(f32)).reshape(bsz, L, D_LRU) + br.astype(f32))
    ig = jax.nn.sigmoid(jnp.einsum('blhi,hij->blhj', xh, wi.astype(f32)).reshape(bsz, L, D_LRU) + bi.astype(f32))
    log_a = -LRU_C * r * jax.nn.softplus(-lam.astype(f32))
    a = jnp.exp(log_a)
    b = jnp.sqrt(-jnp.expm1(2.0 * log_a)) * (ig * xf)
    b = b.at[:, 0].add(a[:, 0] * h0.astype(f32))
    _, h = lax.associative_scan(_lin_combine, (a, b), axis=1)
    return h, h[:, -1]


def _rotary(x, pos):
    half = x.shape[-1] // 2
    freqs = ROPE_BASE ** (-jnp.arange(half, dtype=jnp.float32) / half)
    ang = pos.astype(jnp.float32)[:, None] * freqs[None, :]
    cos = jnp.cos(ang)[None, :, None, :]
    sin = jnp.sin(ang)[None, :, None, :]
    xf = x.astype(jnp.float32)
    x1, x2 = xf[..., :half], xf[..., half:]
    return jnp.concatenate([x1 * cos - x2 * sin, x1 * sin + x2 * cos], axis=-1)


def _retention(q, k, v, s0):
    f32 = jnp.float32
    bsz, L, H, _ = q.shape
    C = _chunk_len(L)
    nc = L // C
    log_g = jnp.log1p(-jnp.exp2(-5.0 - jnp.arange(H, dtype=f32)))
    idx = jnp.arange(C, dtype=f32)
    diff = idx[:, None] - idx[None, :]
    decay = jnp.exp(jnp.where(diff[None] >= 0, diff[None] * log_g[:, None, None], -jnp.inf))
    q_dec = jnp.exp((idx[None, :] + 1.0) * log_g[:, None])[..., None]
    k_dec = jnp.exp((C - 1.0 - idx[None, :]) * log_g[:, None])[..., None]
    c_dec = jnp.exp(C * log_g)[:, None, None]

    def to_chunks(t):
        t = t.astype(f32).reshape(bsz, nc, C, H, t.shape[-1])
        return jnp.moveaxis(t, 1, 0).transpose(0, 1, 3, 2, 4)

    def step(s, inp):
        qb, kb, vb = inp
        scores = jnp.einsum('bhik,bhjk->bhij', qb, kb) * decay
        o = jnp.einsum('bhij,bhjv->bhiv', scores, vb) + jnp.einsum('bhik,bhkv->bhiv', qb * q_dec, s)
        s = s * c_dec + jnp.einsum('bhjk,bhjv->bhkv', kb * k_dec, vb)
        return s, o

    s, o = lax.scan(step, s0.astype(f32), (to_chunks(q), to_chunks(k), to_chunks(v)))
    o = o.transpose(1, 0, 3, 2, 4).reshape(bsz, L, H, v.shape[-1])
    return o, s


def _ssd(x, dt, a, bm, cm, s0):
    f32 = jnp.float32
    bsz, L = x.shape[:2]
    C = _chunk_len(L)
    nc = L // C
    R = SSD_HEADS // SSD_GROUPS
    mv = lambda t: jnp.moveaxis(t, 1, 0)
    xf = mv(x.astype(f32).reshape(bsz, nc, C, SSD_GROUPS, R, SSD_HD))
    dtf = dt.astype(f32).reshape(bsz, nc, C, SSD_GROUPS, R)
    daf = mv(dtf * a.astype(f32).reshape(SSD_GROUPS, R))
    dtf = mv(dtf)
    bf = mv(bm.astype(f32).reshape(bsz, nc, C, SSD_GROUPS, SSD_DSTATE))
    cf = mv(cm.astype(f32).reshape(bsz, nc, C, SSD_GROUPS, SSD_DSTATE))
    causal = jnp.tril(jnp.ones((C, C), dtype=bool))[None, :, :, None, None]

    def step(s, inp):
        xb, dtb, dab, bb, cb = inp
        cum = jnp.cumsum(dab, axis=1)
        seg = cum[:, :, None] - cum[:, None, :]
        lmask = jnp.exp(jnp.where(causal, seg, -jnp.inf))
        cbw = jnp.einsum('bign,bjgn->bijg', cb, bb)
        wgt = cbw[..., None] * lmask * dtb[:, None]
        y = jnp.einsum('bijgr,bjgrp->bigrp', wgt, xb)
        y = y + jnp.einsum('bign,bgrpn->bigrp', cb, s) * jnp.exp(cum)[..., None]
        wlast = jnp.exp(cum[:, -1:] - cum) * dtb
        s = s * jnp.exp(cum[:, -1])[..., None, None] + jnp.einsum('bjgr,bjgrp,bjgn->bgrpn', wlast, xb, bb)
        return s, y

    s0f = s0.astype(f32).reshape(bsz, SSD_GROUPS, R, SSD_HD, SSD_DSTATE)
    s, y = lax.scan(step, s0f, (xf, dtf, daf, bf, cf))
    y = jnp.moveaxis(y, 0, 1).reshape(bsz, L, SSD_HEADS, SSD_HD)
    return y, s.reshape(bsz, SSD_HEADS, SSD_HD, SSD_DSTATE)


def _layer_stack(x, pos, st_lru_conv, st_lru_h, st_ret, st_ssd_conv, st_ssd, st_ffn_conv, w):
    f32 = jnp.float32
    bsz, L, _ = x.shape
    refs = (st_lru_conv, st_lru_h, st_ret, st_ssd_conv, st_ssd, st_ffn_conv)
    new = [[] for _ in refs]
    for l in range(DEPTH):
        h = _rmsnorm(x, w['norm_mix_g'][l])
        proj = jnp.einsum('bld,de->ble', h, w['w_in'][l])
        lru_x, lru_gate, r_q, r_k, r_v, r_g, s_z, s_xbc, s_dt = _split_cols(proj, IN_SIZES)
        xc, lru_buf = _causal_dwconv(lru_x, st_lru_conv[l], w['lru_conv_w'][l], w['lru_conv_b'][l])
        h_lru, lru_last = _rg_lru(xc, st_lru_h[l], w['lru_wr'][l], w['lru_br'][l], w['lru_wi'][l], w['lru_bi'][l], w['lru_lambda'][l])
        y_a = h_lru.astype(x.dtype) * jax.nn.gelu(lru_gate)
        q = _rotary(r_q.reshape(bsz, L, RET_HEADS, RET_DK), pos)
        k = _rotary(r_k.reshape(bsz, L, RET_HEADS, RET_DK), pos) * (RET_DK ** -0.5)
        v = r_v.reshape(bsz, L, RET_HEADS, RET_DV)
        o, ret_last = _retention(q, k, v, st_ret[l])
        o = _rmsnorm(o, w['ret_norm_g'][l]).astype(x.dtype).reshape(bsz, L, D_RET)
        y_b = jax.nn.silu(r_g) * o
        xbc, ssd_buf = _causal_dwconv(s_xbc, st_ssd_conv[l], w['ssd_conv_w'][l], w['ssd_conv_b'][l])
        xbc = jax.nn.silu(xbc)
        xs, bm, cm = _split_cols(xbc, (D_SSD, SSD_GROUPS * SSD_DSTATE, SSD_GROUPS * SSD_DSTATE))
        dt = jax.nn.softplus(s_dt.astype(f32) + w['ssd_dt_bias'][l].astype(f32))
        a = -jnp.exp(w['ssd_a_log'][l].astype(f32))
        ys, ssd_last = _ssd(xs, dt, a, bm.reshape(bsz, L, SSD_GROUPS, SSD_DSTATE), cm.reshape(bsz, L, SSD_GROUPS, SSD_DSTATE), st_ssd[l])
        ys = ys + w['ssd_d'][l].astype(f32)[:, None] * xs.astype(f32).reshape(bsz, L, SSD_HEADS, SSD_HD)
        gated = ys * jax.nn.silu(s_z.astype(f32)).reshape(bsz, L, SSD_HEADS, SSD_HD)
        gated = gated.reshape(bsz, L, SSD_GROUPS, D_SSD // SSD_GROUPS)
        y_c = _rmsnorm(gated, w['ssd_norm_g'][l].reshape(SSD_GROUPS, D_SSD // SSD_GROUPS)).astype(x.dtype).reshape(bsz, L, D_SSD)
        mix = jnp.concatenate([y_a, y_b, y_c], axis=-1)
        x = x + jnp.einsum('blm,md->bld', mix, w['w_out'][l])
        h = _rmsnorm(x, w['norm_ffn_g'][l])
        gu = jnp.einsum('bld,df->blf', h, w['ffn_w_up'][l])
        gate, up = gu[..., :D_FF], gu[..., D_FF:]
        gate, ffn_buf = _causal_dwconv(gate, st_ffn_conv[l], w['ffn_conv_w'][l], w['ffn_conv_b'][l])
        x = x + jnp.einsum('blf,fd->bld', jax.nn.gelu(gate) * up, w['ffn_w_down'][l])
        for lst, val, ref in zip(new, (lru_buf, lru_last, ret_last, ssd_buf, ssd_last, ffn_buf), refs):
            lst.append(val.astype(ref.dtype))
    y = _rmsnorm(x, w['norm_final_g'])
    return y, [jnp.stack(s, axis=0) for s in new]


def setup_inputs(seed: int = 0) -> dict:
    key = jax.random.key(seed)
    ks = iter(jax.random.split(key, 48))
    f32 = jnp.float32

    def nrm(shape, scale):
        return jax.random.normal(next(ks), shape, f32) * scale

    x_prompt = nrm((BATCH, SEQ, D_MODEL), 1.0)
    x_sample = nrm((DEC_BATCH, DEC_SEQ, D_MODEL), 1.0)
    state_lru_conv = nrm((DEPTH, DEC_BATCH, CONV_W - 1, D_LRU), 1.0)
    state_lru_h = nrm((DEPTH, DEC_BATCH, D_LRU), 0.5)
    state_ret = nrm((DEPTH, DEC_BATCH, RET_HEADS, RET_DK, RET_DV), 0.5)
    state_ssd_conv = nrm((DEPTH, DEC_BATCH, CONV_W - 1, SSD_CONV_DIM), 1.0)
    state_ssd = nrm((DEPTH, DEC_BATCH, SSD_HEADS, SSD_HD, SSD_DSTATE), 0.1)
    state_ffn_conv = nrm((DEPTH, DEC_BATCH, FFN_CONV_W - 1, D_FF), 1.0)
    norm_mix_g = 1.0 + nrm((DEPTH, D_MODEL), 0.02)
    w_in = nrm((DEPTH, D_MODEL, D_IN), D_MODEL ** -0.5)
    lru_conv_w = nrm((DEPTH, CONV_W, D_LRU), CONV_W ** -0.5)
    lru_conv_b = nrm((DEPTH, D_LRU), 0.02)
    lru_wr = nrm((DEPTH, LRU_HEADS, LRU_HD, LRU_HD), LRU_HD ** -0.5)
    lru_br = nrm((DEPTH, D_LRU), 0.02)
    lru_wi = nrm((DEPTH, LRU_HEADS, LRU_HD, LRU_HD), LRU_HD ** -0.5)
    lru_bi = nrm((DEPTH, D_LRU), 0.02)
    a_lru = jax.random.uniform(next(ks), (DEPTH, D_LRU), f32, 0.9, 0.999)
    a_root = a_lru ** (1.0 / LRU_C)
    lru_lambda = jnp.log(a_root) - jnp.log1p(-a_root)
    ret_norm_g = 1.0 + nrm((DEPTH, RET_HEADS, RET_DV), 0.02)
    ssd_conv_w = nrm((DEPTH, CONV_W, SSD_CONV_DIM), CONV_W ** -0.5)
    ssd_conv_b = nrm((DEPTH, SSD_CONV_DIM), 0.02)
    dt0 = jnp.exp(jax.random.uniform(next(ks), (DEPTH, SSD_HEADS), f32, float(np.log(1e-3)), float(np.log(1e-1))))
    ssd_dt_bias = dt0 + jnp.log(-jnp.expm1(-dt0))
    ssd_a_log = jnp.log(jax.random.uniform(next(ks), (DEPTH, SSD_HEADS), f32, 1.0, 16.0))
    ssd_d = 1.0 + nrm((DEPTH, SSD_HEADS), 0.02)
    ssd_norm_g = 1.0 + nrm((DEPTH, D_SSD), 0.02)
    w_out = nrm((DEPTH, D_MIX, D_MODEL), D_MIX ** -0.5)
    norm_ffn_g = 1.0 + nrm((DEPTH, D_MODEL), 0.02)
    ffn_w_up = nrm((DEPTH, D_MODEL, 2 * D_FF), D_MODEL ** -0.5)
    ffn_conv_w = nrm((DEPTH, FFN_CONV_W, D_FF), FFN_CONV_W ** -0.5)
    ffn_conv_b = nrm((DEPTH, D_FF), 0.02)
    ffn_w_down = nrm((DEPTH, D_FF, D_MODEL), D_FF ** -0.5)
    norm_final_g = 1.0 + nrm((D_MODEL,), 0.02)
    return {'x_prompt': x_prompt, 'x_sample': x_sample,
            'state_lru_conv': state_lru_conv, 'state_lru_h': state_lru_h, 'state_ret': state_ret,
            'state_ssd_conv': state_ssd_conv, 'state_ssd': state_ssd, 'state_ffn_conv': state_ffn_conv,
            'norm_mix_g': norm_mix_g, 'w_in': w_in,
            'lru_conv_w': lru_conv_w, 'lru_conv_b': lru_conv_b, 'lru_wr': lru_wr, 'lru_br': lru_br,
            'lru_wi': lru_wi, 'lru_bi': lru_bi, 'lru_lambda': lru_lambda,
            'ret_norm_g': ret_norm_g,
            'ssd_conv_w': ssd_conv_w, 'ssd_conv_b': ssd_conv_b, 'ssd_dt_bias': ssd_dt_bias,
            'ssd_a_log': ssd_a_log, 'ssd_d': ssd_d, 'ssd_norm_g': ssd_norm_g,
            'w_out': w_out, 'norm_ffn_g': norm_ffn_g, 'ffn_w_up': ffn_w_up,
            'ffn_conv_w': ffn_conv_w, 'ffn_conv_b': ffn_conv_b, 'ffn_w_down': ffn_w_down,
            'norm_final_g': norm_final_g}


def reference(x_prompt, x_sample, state_lru_conv, state_lru_h, state_ret, state_ssd_conv, state_ssd, state_ffn_conv,
              norm_mix_g, w_in, lru_conv_w, lru_conv_b, lru_wr, lru_br, lru_wi, lru_bi, lru_lambda,
              ret_norm_g, ssd_conv_w, ssd_conv_b, ssd_dt_bias, ssd_a_log, ssd_d, ssd_norm_g,
              w_out, norm_ffn_g, ffn_w_up, ffn_conv_w, ffn_conv_b, ffn_w_down, norm_final_g):
    w = {'norm_mix_g': norm_mix_g, 'w_in': w_in, 'lru_conv_w': lru_conv_w, 'lru_conv_b': lru_conv_b,
         'lru_wr': lru_wr, 'lru_br': lru_br, 'lru_wi': lru_wi, 'lru_bi': lru_bi, 'lru_lambda': lru_lambda,
         'ret_norm_g': ret_norm_g, 'ssd_conv_w': ssd_conv_w, 'ssd_conv_b': ssd_conv_b,
         'ssd_dt_bias': ssd_dt_bias, 'ssd_a_log': ssd_a_log, 'ssd_d': ssd_d, 'ssd_norm_g': ssd_norm_g,
         'w_out': w_out, 'norm_ffn_g': norm_ffn_g, 'ffn_w_up': ffn_w_up, 'ffn_conv_w': ffn_conv_w,
         'ffn_conv_b': ffn_conv_b, 'ffn_w_down': ffn_w_down, 'norm_final_g': norm_final_g}
    bp = x_prompt.shape[0]
    p_lru_conv0 = jnp.zeros((DEPTH, bp) + state_lru_conv.shape[2:], state_lru_conv.dtype)
    p_lru_h0 = jnp.zeros((DEPTH, bp) + state_lru_h.shape[2:], state_lru_h.dtype)
    p_ret0 = jnp.zeros((DEPTH, bp) + state_ret.shape[2:], state_ret.dtype)
    p_ssd_conv0 = jnp.zeros((DEPTH, bp) + state_ssd_conv.shape[2:], state_ssd_conv.dtype)
    p_ssd0 = jnp.zeros((DEPTH, bp) + state_ssd.shape[2:], state_ssd.dtype)
    p_ffn_conv0 = jnp.zeros((DEPTH, bp) + state_ffn_conv.shape[2:], state_ffn_conv.dtype)
    pos_p = jnp.arange(x_prompt.shape[1], dtype=jnp.int32)
    pos_s = PAST_LEN + jnp.arange(x_sample.shape[1], dtype=jnp.int32)
    y_prompt, ns_p = _layer_stack(x_prompt, pos_p, p_lru_conv0, p_lru_h0, p_ret0, p_ssd_conv0, p_ssd0, p_ffn_conv0, w)
    y_sample, ns_s = _layer_stack(x_sample, pos_s, state_lru_conv, state_lru_h, state_ret, state_ssd_conv, state_ssd, state_ffn_conv, w)
    lru_conv_p, lru_h_p, ret_p, ssd_conv_p, ssd_p, ffn_conv_p = ns_p
    lru_conv_s, lru_h_s, ret_s, ssd_conv_s, ssd_s, ffn_conv_s = ns_s
    return (y_prompt, y_sample, lru_conv_p, lru_conv_s, lru_h_p, lru_h_s, ret_p, ret_s,
            ssd_conv_p, ssd_conv_s, ssd_p, ssd_s, ffn_conv_p, ffn_conv_s)
```

```python
import functools

import jax
import jax.numpy as jnp
import numpy as np
from jax import lax
from jax.experimental import pallas as pl
from jax.experimental.pallas import tpu as pltpu

F32 = jnp.float32
BF16 = jnp.bfloat16

D_MODEL = 1024
DEPTH = 4
D_LRU = 512
LRU_HEADS = 8
LRU_HD = D_LRU // LRU_HEADS
LRU_C = 8.0
CONV_W = 4
RET_HEADS = 4
RET_DK = 64
RET_DV = 128
D_RET = RET_HEADS * RET_DV
D_QK = RET_HEADS * RET_DK
ROPE_BASE = 10000.0
SSD_HD = 64
D_SSD = 512
SSD_HEADS = 8
SSD_GROUPS = 2
SSD_DSTATE = 128
SSD_BC = SSD_GROUPS * SSD_DSTATE
SSD_CONV_DIM = D_SSD + 2 * SSD_BC
D_MIX = D_LRU + D_RET + D_SSD
D_FF = 2816
FFN_CONV_W = 3
EPS = 1e-6
CHUNK = 128

COL_LRU_X = 0
COL_LRU_G = 512
COL_Q = 1024
COL_K = 1280
COL_V = 1536
COL_RG = 2048
COL_Z = 2560
COL_XBC = 3072
COL_DT = 4096
D_IN_PAD = 4224

LANES = 128
SUBLANES = 8
VMEM_LIMIT_BYTES = 56 * 1024 * 1024


def _cparams(*sem):
    return pltpu.CompilerParams(dimension_semantics=sem, vmem_limit_bytes=VMEM_LIMIT_BYTES)


def _dot(a, b):
    return jnp.dot(a, b, preferred_element_type=F32)


def _dot_nt(a, b):
    return lax.dot_general(a, b, (((1,), (1,)), ((), ())), preferred_element_type=F32)


def _dot_tn(a, b):
    return lax.dot_general(a, b, (((0,), (0,)), ((), ())), preferred_element_type=F32)


def _dot_exact(a, b):
    return jnp.dot(a, b, preferred_element_type=F32, precision=lax.Precision.HIGHEST)


def _dot_nt_exact(a, b):
    return lax.dot_general(a, b, (((1,), (1,)), ((), ())), preferred_element_type=F32,
                           precision=lax.Precision.HIGHEST)


def _rms_rows(x, g):
    return x * lax.rsqrt(jnp.mean(x * x, axis=-1, keepdims=True) + EPS) * g


def _sigmoid(x):
    return 1.0 / (1.0 + jnp.exp(-x))


def _silu(x):
    return x * _sigmoid(x)


def _gelu(x):
    return jax.nn.gelu(x, approximate=True)


def _softplus(x):
    return jnp.maximum(x, 0.0) + jnp.log1p(jnp.exp(-jnp.abs(x)))


IN_COL_CHUNK = 384


def _inproj_kernel(x_ref, g_ref, w_ref, o_ref):
    hn = _rms_rows(x_ref[...], g_ref[...]).astype(BF16)
    for c0 in range(0, D_IN_PAD, IN_COL_CHUNK):
        o_ref[:, c0:c0 + IN_COL_CHUNK] = _dot(hn, w_ref[:, c0:c0 + IN_COL_CHUNK])


def _inproj(x2d, g, w_pad, tm):
    t = x2d.shape[0]
    return pl.pallas_call(
        _inproj_kernel,
        grid=(t // tm,),
        in_specs=[
            pl.BlockSpec((tm, D_MODEL), lambda i: (i, 0)),
            pl.BlockSpec((1, D_MODEL), lambda i: (0, 0)),
            pl.BlockSpec((D_MODEL, D_IN_PAD), lambda i: (0, 0)),
        ],
        out_specs=pl.BlockSpec((tm, D_IN_PAD), lambda i: (i, 0)),
        out_shape=jax.ShapeDtypeStruct((t, D_IN_PAD), F32),
        compiler_params=_cparams("arbitrary"),
        name="inproj",
    )(x2d, g, w_pad)


def _lru_gates(xc, wr_ref, br_ref, wi_ref, bi_ref, lam_ref):
    xcb = xc.astype(BF16)
    r = _sigmoid(_dot(xcb, wr_ref[...]) + br_ref[...])
    ig = _sigmoid(_dot(xcb, wi_ref[...]) + bi_ref[...])
    log_a = (-LRU_C * r) * _softplus(-lam_ref[...])
    a = jnp.exp(log_a)
    b = jnp.sqrt(-jnp.tanh(log_a) * (a * a + 1.0)) * (ig * xc)
    return a, b


def _scan_rows(a, b):
    n = a.shape[0]
    row = lax.broadcasted_iota(jnp.int32, a.shape, 0)
    d = 1
    while d < n:
        if d < SUBLANES:
            keep = row >= d
            a_sh = jnp.where(keep, pltpu.roll(a, d, axis=0), 1.0)
            b_sh = jnp.where(keep, pltpu.roll(b, d, axis=0), 0.0)
        else:
            a_sh = jnp.concatenate([jnp.ones((d, a.shape[1]), F32), a[:n - d]], axis=0)
            b_sh = jnp.concatenate([jnp.zeros((d, a.shape[1]), F32), b[:n - d]], axis=0)
        b = a * b_sh + b
        a = a * a_sh
        d *= 2
    return a, b


def _rotary_split(x, cos, sin):
    xa, xb = x[:, :LANES], x[:, LANES:]
    return jnp.concatenate([xa * cos - xb * sin, xa * sin + xb * cos], axis=1)


def _head_of_qk_lane(shape, axis):
    lane = lax.broadcasted_iota(jnp.int32, shape, axis)
    return (lane % LANES) // (RET_DK // 2)


def _lru_prompt_kernel(x_ref, gate_ref, cw_ref, cb_ref, wr_ref, br_ref, wi_ref, bi_ref, lam_ref,
                       y_ref, conv_ref, h_ref, xbuf, hcar):
    t = pl.program_id(1)
    tl = x_ref.shape[0]

    @pl.when(t == 0)
    def _():
        xbuf[0:SUBLANES, :] = jnp.zeros((SUBLANES, D_LRU), F32)
        hcar[...] = jnp.zeros_like(hcar)

    x = x_ref[...]
    xbuf[SUBLANES:SUBLANES + tl, :] = x
    xc = cb_ref[...] + cw_ref[0:1, :] * xbuf[SUBLANES - 3:SUBLANES - 3 + tl, :]
    xc = xc + cw_ref[1:2, :] * xbuf[SUBLANES - 2:SUBLANES - 2 + tl, :]
    xc = xc + cw_ref[2:3, :] * xbuf[SUBLANES - 1:SUBLANES - 1 + tl, :]
    xc = xc + cw_ref[3:4, :] * x
    tail = xbuf[tl:tl + SUBLANES, :]
    xbuf[0:SUBLANES, :] = tail
    conv_ref[0] = tail[SUBLANES - (CONV_W - 1):, :]

    a, b = _lru_gates(xc, wr_ref, br_ref, wi_ref, bi_ref, lam_ref)
    a_cum, h0 = _scan_rows(a, b)
    h = h0 + a_cum * hcar[...]
    hlast = h[tl - 1:tl, :]
    hcar[...] = hlast
    h_ref[0] = hlast
    y_ref[...] = (h * _gelu(gate_ref[...])).astype(y_ref.dtype)


def _lru_prompt(proj, bsz, seqlen, w, tl):
    nt = seqlen // tl
    row = lambda b, t: b * nt + t
    vec = lambda n: pl.BlockSpec((1, n), lambda b, t: (0, 0))
    return pl.pallas_call(
        _lru_prompt_kernel,
        grid=(bsz, nt),
        in_specs=[
            pl.BlockSpec((tl, D_LRU), lambda b, t: (row(b, t), COL_LRU_X // D_LRU)),
            pl.BlockSpec((tl, D_LRU), lambda b, t: (row(b, t), COL_LRU_G // D_LRU)),
            pl.BlockSpec((CONV_W, D_LRU), lambda b, t: (0, 0)),
            vec(D_LRU),
            pl.BlockSpec((D_LRU, D_LRU), lambda b, t: (0, 0)),
            vec(D_LRU),
            pl.BlockSpec((D_LRU, D_LRU), lambda b, t: (0, 0)),
            vec(D_LRU),
            vec(D_LRU),
        ],
        out_specs=[
            pl.BlockSpec((tl, D_LRU), lambda b, t: (row(b, t), 0)),
            pl.BlockSpec((1, CONV_W - 1, D_LRU), lambda b, t: (b, 0, 0)),
            pl.BlockSpec((1, 1, D_LRU), lambda b, t: (b, 0, 0)),
        ],
        out_shape=[
            jax.ShapeDtypeStruct((bsz * seqlen, D_LRU), BF16),
            jax.ShapeDtypeStruct((bsz, CONV_W - 1, D_LRU), F32),
            jax.ShapeDtypeStruct((bsz, 1, D_LRU), F32),
        ],
        scratch_shapes=[
            pltpu.VMEM((tl + SUBLANES, D_LRU), F32),
            pltpu.VMEM((1, D_LRU), F32),
        ],
        compiler_params=_cparams("arbitrary", "arbitrary"),
        name="lru_prompt",
    )(proj, proj, w["lru_conv_w"], w["lru_conv_b"], w["lru_wr"], w["lru_br"], w["lru_wi"],
      w["lru_bi"], w["lru_lambda"])


def _ret_prompt_kernel(q_ref, k_ref, v_ref, g_ref, cos_ref, sin_ref, decay_ref, qdec_ref, kdec_ref,
                       cdec_ref, ng_ref, y_ref, s_ref, state):
    c = pl.program_id(1)

    @pl.when(c == 0)
    def _():
        state[...] = jnp.zeros_like(state)

    cos = cos_ref[...]
    sin = sin_ref[...]
    q = _rotary_split(q_ref[...], cos, sin)
    k = _rotary_split(k_ref[...], cos, sin) * (RET_DK ** -0.5)
    kb = k.astype(BF16)
    qd = q * qdec_ref[...]
    kd = k * kdec_ref[...]
    head = _head_of_qk_lane(q.shape, 1)
    for h in range(RET_HEADS):
        sel = head == h
        vh = v_ref[:, h * RET_DV:(h + 1) * RET_DV].astype(BF16)
        scores = _dot_nt(jnp.where(sel, q, 0.0).astype(BF16), kb) * decay_ref[h]
        o = _dot(scores.astype(BF16), vh)
        o = o + _dot(jnp.where(sel, qd, 0.0).astype(BF16), state[h].astype(BF16))
        state[h] = state[h] * cdec_ref[h] + _dot_tn(jnp.where(sel, kd, 0.0).astype(BF16), vh)
        o = _rms_rows(o, ng_ref[h:h + 1, :])
        gh = g_ref[:, h * RET_DV:(h + 1) * RET_DV]
        y_ref[:, h * RET_DV:(h + 1) * RET_DV] = (_silu(gh) * o).astype(y_ref.dtype)
        half = RET_DK // 2
        s_ref[0, h, 0:half, :] = state[h, h * half:(h + 1) * half, :]
        s_ref[0, h, half:RET_DK, :] = state[h, LANES + h * half:LANES + (h + 1) * half, :]


def _ret_prompt(proj, bsz, seqlen, w, tabs):
    nc = seqlen // CHUNK
    row = lambda b, c: b * nc + c
    const2 = lambda shape: pl.BlockSpec(shape, lambda b, c: (0, 0))
    const3 = lambda shape: pl.BlockSpec(shape, lambda b, c: (0, 0, 0))
    return pl.pallas_call(
        _ret_prompt_kernel,
        grid=(bsz, nc),
        in_specs=[
            pl.BlockSpec((CHUNK, D_QK), lambda b, c: (row(b, c), COL_Q // D_QK)),
            pl.BlockSpec((CHUNK, D_QK), lambda b, c: (row(b, c), COL_K // D_QK)),
            pl.BlockSpec((CHUNK, D_RET), lambda b, c: (row(b, c), COL_V // D_RET)),
            pl.BlockSpec((CHUNK, D_RET), lambda b, c: (row(b, c), COL_RG // D_RET)),
            pl.BlockSpec((CHUNK, LANES), lambda b, c: (c, 0)),
            pl.BlockSpec((CHUNK, LANES), lambda b, c: (c, 0)),
            const3((RET_HEADS, CHUNK, CHUNK)),
            const2((CHUNK, D_QK)),
            const2((CHUNK, D_QK)),
            const3((RET_HEADS, 1, 1)),
            const2((RET_HEADS, RET_DV)),
        ],
        out_specs=[
            pl.BlockSpec((CHUNK, D_RET), lambda b, c: (row(b, c), 0)),
            pl.BlockSpec((1, RET_HEADS, RET_DK, RET_DV), lambda b, c: (b, 0, 0, 0)),
        ],
        out_shape=[
            jax.ShapeDtypeStruct((bsz * seqlen, D_RET), BF16),
            jax.ShapeDtypeStruct((bsz, RET_HEADS, RET_DK, RET_DV), F32),
        ],
        scratch_shapes=[pltpu.VMEM((RET_HEADS, D_QK, RET_DV), F32)],
        compiler_params=_cparams("arbitrary", "arbitrary"),
        name="ret_prompt",
    )(proj, proj, proj, proj, tabs["cos_p"], tabs["sin_p"], tabs["decay"], tabs["qdec"],
      tabs["kdec"], tabs["cdec"], w["ret_norm_g"])


HEADS_PER_GROUP = SSD_HEADS // SSD_GROUPS
PAIRS = SSD_HEADS // 2


def _ssd_prompt_kernel(z_ref, xbc_ref, dt_ref, cw_ref, cb_ref, dtb_ref, alog_ref, dfull_ref, ng_ref,
                       tril_ref, y_ref, conv_ref, s_ref, xbuf, state):
    c = pl.program_id(1)

    @pl.when(c == 0)
    def _():
        xbuf[0:SUBLANES, :] = jnp.zeros((SUBLANES, SSD_CONV_DIM), F32)
        state[...] = jnp.zeros_like(state)

    x = xbc_ref[...]
    xbuf[SUBLANES:SUBLANES + CHUNK, :] = x
    xc = cb_ref[...] + cw_ref[0:1, :] * xbuf[SUBLANES - 3:SUBLANES - 3 + CHUNK, :]
    xc = xc + cw_ref[1:2, :] * xbuf[SUBLANES - 2:SUBLANES - 2 + CHUNK, :]
    xc = xc + cw_ref[2:3, :] * xbuf[SUBLANES - 1:SUBLANES - 1 + CHUNK, :]
    xc = xc + cw_ref[3:4, :] * x
    tail = xbuf[CHUNK:CHUNK + SUBLANES, :]
    xbuf[0:SUBLANES, :] = tail
    conv_ref[0] = tail[SUBLANES - (CONV_W - 1):, :]
    xc = _silu(xc)
    xs = xc[:, :D_SSD]
    bm = xc[:, D_SSD:D_SSD + SSD_BC].astype(BF16)
    cm = xc[:, D_SSD + SSD_BC:].astype(BF16)

    dt = _softplus(dt_ref[...] + dtb_ref[...])
    da = dt * (-jnp.exp(alog_ref[...]))
    cum = _dot_exact(tril_ref[...], da)
    cum_t = cum.T
    dt_t = dt.T
    ecum = jnp.exp(cum)
    wlast = jnp.exp(cum[CHUNK - 1:CHUNK, :] - cum) * dt
    elast = jnp.exp(cum_t[:, CHUNK - 1:CHUNK])

    causal = (lax.broadcasted_iota(jnp.int32, (CHUNK, CHUNK), 0)
              >= lax.broadcasted_iota(jnp.int32, (CHUNK, CHUNK), 1))
    lane = lax.broadcasted_iota(jnp.int32, (CHUNK, LANES), 1)
    lo = lane < SSD_HD
    rowi = lax.broadcasted_iota(jnp.int32, (2 * SSD_HD, SSD_DSTATE), 0)
    row_lo = rowi < SSD_HD

    cbw = [_dot_nt(cm[:, g * SSD_DSTATE:(g + 1) * SSD_DSTATE], bm[:, g * SSD_DSTATE:(g + 1) * SSD_DSTATE])
           for g in range(SSD_GROUPS)]
    for p in range(PAIRS):
        g = (2 * p) // HEADS_PER_GROUP
        xp = xs[:, p * LANES:(p + 1) * LANES]
        bg = bm[:, g * SSD_DSTATE:(g + 1) * SSD_DSTATE]
        cg = cm[:, g * SSD_DSTATE:(g + 1) * SSD_DSTATE]
        y = None
        for j, sel in ((0, lo), (1, jnp.logical_not(lo))):
            h = 2 * p + j
            seg = cum[:, h:h + 1] - cum_t[h:h + 1, :]
            lmask = jnp.exp(jnp.where(causal, seg, -jnp.inf))
            wgt = cbw[g] * lmask * dt_t[h:h + 1, :]
            yj = _dot(wgt.astype(BF16), jnp.where(sel, xp, 0.0).astype(BF16))
            y = yj if y is None else y + yj
        h0, h1 = 2 * p, 2 * p + 1
        sp = state[p]
        y = y + _dot_nt(cg, sp.astype(BF16)) * jnp.where(lo, ecum[:, h0:h0 + 1], ecum[:, h1:h1 + 1])
        wl = jnp.where(lo, wlast[:, h0:h0 + 1], wlast[:, h1:h1 + 1])
        sp = sp * jnp.where(row_lo, elast[h0:h0 + 1, :], elast[h1:h1 + 1, :]) + _dot_tn(
            (xp * wl).astype(BF16), bg)
        state[p] = sp
        s_ref[0, h0] = sp[:SSD_HD, :]
        s_ref[0, h1] = sp[SSD_HD:, :]
        y_ref[:, p * LANES:(p + 1) * LANES] = (y + dfull_ref[:, p * LANES:(p + 1) * LANES] * xp).astype(
            y_ref.dtype)


def _ssd_gate_norm(y, z, ng):
    gated = y * _silu(z)
    gw = D_SSD // SSD_GROUPS
    outs = []
    for g in range(SSD_GROUPS):
        outs.append(_rms_rows(gated[:, g * gw:(g + 1) * gw], ng[:, g * gw:(g + 1) * gw]))
    return jnp.concatenate(outs, axis=1)


def _ssd_prompt_kernel_full(z_ref, xbc_ref, dt_ref, cw_ref, cb_ref, dtb_ref, alog_ref, dfull_ref, ng_ref,
                            tril_ref, y_ref, conv_ref, s_ref, xbuf, state, ybuf):
    _ssd_prompt_kernel(z_ref, xbc_ref, dt_ref, cw_ref, cb_ref, dtb_ref, alog_ref, dfull_ref, ng_ref,
                       tril_ref, ybuf, conv_ref, s_ref, xbuf, state)
    y_ref[...] = _ssd_gate_norm(ybuf[...], z_ref[...], ng_ref[...]).astype(y_ref.dtype)


def _ssd_prompt(proj, bsz, seqlen, w, tabs):
    nc = seqlen // CHUNK
    row = lambda b, c: b * nc + c
    const2 = lambda shape: pl.BlockSpec(shape, lambda b, c: (0, 0))
    return pl.pallas_call(
        _ssd_prompt_kernel_full,
        grid=(bsz, nc),
        in_specs=[
            pl.BlockSpec((CHUNK, D_SSD), lambda b, c: (row(b, c), COL_Z // D_SSD)),
            pl.BlockSpec((CHUNK, SSD_CONV_DIM), lambda b, c: (row(b, c), COL_XBC // SSD_CONV_DIM)),
            pl.BlockSpec((CHUNK, LANES), lambda b, c: (row(b, c), COL_DT // LANES)),
            const2((CONV_W, SSD_CONV_DIM)),
            const2((1, SSD_CONV_DIM)),
            const2((1, LANES)),
            const2((1, LANES)),
            const2((1, D_SSD)),
            const2((1, D_SSD)),
            const2((CHUNK, CHUNK)),
        ],
        out_specs=[
            pl.BlockSpec((CHUNK, D_SSD), lambda b, c: (row(b, c), 0)),
            pl.BlockSpec((1, CONV_W - 1, SSD_CONV_DIM), lambda b, c: (b, 0, 0)),
            pl.BlockSpec((1, SSD_HEADS, SSD_HD, SSD_DSTATE), lambda b, c: (b, 0, 0, 0)),
        ],
        out_shape=[
            jax.ShapeDtypeStruct((bsz * seqlen, D_SSD), BF16),
            jax.ShapeDtypeStruct((bsz, CONV_W - 1, SSD_CONV_DIM), F32),
            jax.ShapeDtypeStruct((bsz, SSD_HEADS, SSD_HD, SSD_DSTATE), F32),
        ],
        scratch_shapes=[
            pltpu.VMEM((CHUNK + SUBLANES, SSD_CONV_DIM), F32),
            pltpu.VMEM((PAIRS, 2 * SSD_HD, SSD_DSTATE), F32),
            pltpu.VMEM((CHUNK, D_SSD), F32),
        ],
        compiler_params=_cparams("arbitrary", "arbitrary"),
        name="ssd_prompt",
    )(proj, proj, proj, w["ssd_conv_w"], w["ssd_conv_b"], w["ssd_dt_bias"], w["ssd_a_log"],
      w["ssd_d_full"], w["ssd_norm_g"], tabs["tril"])


FF_TILE = D_FF // 2


def _ffn_kernel(x_ref, ya_ref, yb_ref, yc_ref, wo_ref, g_ref, wup_ref, cw_ref, cb_ref, wdn_ref, fg_ref,
                *rest, decode, final_norm):
    if decode:
        st_ref, o_ref, conv_ref = rest
    else:
        o_ref, conv_ref, gbuf, halo = rest
    tm = x_ref.shape[0]
    x1 = x_ref[...]
    x1 = x1 + _dot(ya_ref[...], wo_ref[0:D_LRU, :])
    x1 = x1 + _dot(yb_ref[...], wo_ref[D_LRU:D_LRU + D_RET, :])
    x1 = x1 + _dot(yc_ref[...], wo_ref[D_LRU + D_RET:, :])
    hn = _rms_rows(x1, g_ref[...]).astype(BF16)

    if not decode:
        @pl.when(pl.program_id(1) == 0)
        def _():
            halo[...] = jnp.zeros_like(halo)

    acc = None
    for f0 in range(0, D_FF, FF_TILE):
        gate = _dot(hn, wup_ref[:, f0:f0 + FF_TILE])
        up = _dot(hn, wup_ref[:, D_FF + f0:D_FF + f0 + FF_TILE])
        w0 = cw_ref[0:1, f0:f0 + FF_TILE]
        w1 = cw_ref[1:2, f0:f0 + FF_TILE]
        w2 = cw_ref[2:3, f0:f0 + FF_TILE]
        cb = cb_ref[:, f0:f0 + FF_TILE]
        if decode:
            s0 = st_ref[:, f0:f0 + FF_TILE]
            s1 = st_ref[:, D_FF + f0:D_FF + f0 + FF_TILE]
            conv = cb + w0 * s0 + w1 * s1 + w2 * gate
            conv_ref[:, f0:f0 + FF_TILE] = s1
            conv_ref[:, D_FF + f0:D_FF + f0 + FF_TILE] = gate
        else:
            gbuf[0:SUBLANES, :] = halo[:, f0:f0 + FF_TILE]
            gbuf[SUBLANES:SUBLANES + tm, :] = gate
            conv = cb + w0 * gbuf[SUBLANES - 2:SUBLANES - 2 + tm, :]
            conv = conv + w1 * gbuf[SUBLANES - 1:SUBLANES - 1 + tm, :] + w2 * gate
            tail = gbuf[tm:tm + SUBLANES, :]
            halo[:, f0:f0 + FF_TILE] = tail
            conv_ref[0, :, f0:f0 + FF_TILE] = tail[SUBLANES - (FFN_CONV_W - 1):, :]
        act = (_gelu(conv) * up).astype(BF16)
        d = _dot(act, wdn_ref[f0:f0 + FF_TILE, :])
        acc = d if acc is None else acc + d
    acc = x1 + acc
    if final_norm:
        acc = _rms_rows(acc, fg_ref[...])
    o_ref[...] = acc


def _ffn(x2d, ya, yb, yc, w, fg, *, bsz, seqlen, tm, decode, final_norm, st=None):
    t = x2d.shape[0]
    kern = functools.partial(_ffn_kernel, decode=decode, final_norm=final_norm)
    if decode:
        grid = (t // tm,)
        rowmap = lambda i: (i, 0)
        const = lambda shape: pl.BlockSpec(shape, lambda i: (0, 0))
        sem = ("arbitrary",)
    else:
        nt = seqlen // tm
        grid = (bsz, nt)
        rowmap = lambda b, i: (b * nt + i, 0)
        const = lambda shape: pl.BlockSpec(shape, lambda b, i: (0, 0))
        sem = ("arbitrary", "arbitrary")
    in_specs = [
        pl.BlockSpec((tm, D_MODEL), rowmap),
        pl.BlockSpec((tm, D_LRU), rowmap),
        pl.BlockSpec((tm, D_RET), rowmap),
        pl.BlockSpec((tm, D_SSD), rowmap),
        const((D_MIX, D_MODEL)),
        const((1, D_MODEL)),
        const((D_MODEL, 2 * D_FF)),
        const((FFN_CONV_W, D_FF)),
        const((1, D_FF)),
        const((D_FF, D_MODEL)),
        const((1, D_MODEL)),
    ]
    args = [x2d, ya, yb, yc, w["w_out"], w["norm_ffn_g"], w["ffn_w_up"], w["ffn_conv_w"], w["ffn_conv_b"],
            w["ffn_w_down"], fg]
    if decode:
        in_specs.append(pl.BlockSpec((tm, 2 * D_FF), rowmap))
        args.append(st)
        out_specs = [pl.BlockSpec((tm, D_MODEL), rowmap), pl.BlockSpec((tm, 2 * D_FF), rowmap)]
        out_shape = [jax.ShapeDtypeStruct((t, D_MODEL), F32), jax.ShapeDtypeStruct((t, 2 * D_FF), F32)]
        scratch = []
    else:
        out_specs = [pl.BlockSpec((tm, D_MODEL), rowmap),
                     pl.BlockSpec((1, FFN_CONV_W - 1, D_FF), lambda b, i: (b, 0, 0))]
        out_shape = [jax.ShapeDtypeStruct((t, D_MODEL), F32),
                     jax.ShapeDtypeStruct((bsz, FFN_CONV_W - 1, D_FF), F32)]
        scratch = [pltpu.VMEM((tm + SUBLANES, FF_TILE), F32), pltpu.VMEM((SUBLANES, D_FF), F32)]
    return pl.pallas_call(
        kern, grid=grid, in_specs=in_specs, out_specs=out_specs, out_shape=out_shape,
        scratch_shapes=scratch, compiler_params=_cparams(*sem),
        name="ffn_decode" if decode else "ffn_prompt",
    )(*args)


DEC_TILE = 8


def _transpose_rows_exact(x, eye_ref):
    n = x.shape[1]
    return _dot_nt_exact(eye_ref[0:n, 0:n], x)


def _mix_decode_kernel(lx_ref, lg_ref, q_ref, k_ref, v_ref, rg_ref, z_ref, xbc_ref, dt_ref,
                       lconv_ref, lh_ref, rs_ref, sconv_ref, ss_ref,
                       lcw_ref, lcb_ref, wr_ref, br_ref, wi_ref, bi_ref, lam_ref,
                       cos_ref, sin_ref, cdec_ref, rng_ref,
                       scw_ref, scb_ref, dtb_ref, alog_ref, afull_ref, dfull_ref, sng_ref,
                       eye_ref, hexp_ref,
                       ya_ref, yb_ref, yc_ref, lconv_o, lh_o, rs_o, sconv_o, ss_o,
                       obuf, ybuf):
    x = lx_ref[...]
    st = lconv_ref[...]
    xc = lcb_ref[...] + lcw_ref[0:1, :] * st[:, 0:D_LRU] + lcw_ref[1:2, :] * st[:, D_LRU:2 * D_LRU]
    xc = xc + lcw_ref[2:3, :] * st[:, 2 * D_LRU:] + lcw_ref[3:4, :] * x
    lconv_o[:, 0:2 * D_LRU] = st[:, D_LRU:]
    lconv_o[:, 2 * D_LRU:] = x
    a, b = _lru_gates(xc, wr_ref, br_ref, wi_ref, bi_ref, lam_ref)
    h = b + a * lh_ref[...]
    lh_o[...] = h
    ya_ref[...] = (h * _gelu(lg_ref[...])).astype(ya_ref.dtype)

    cos = cos_ref[...]
    sin = sin_ref[...]
    q = _rotary_split(q_ref[...], cos, sin)
    k = _rotary_split(k_ref[...], cos, sin) * (RET_DK ** -0.5)
    q_t = _transpose_rows_exact(q, eye_ref)
    k_t = _transpose_rows_exact(k, eye_ref)
    half = RET_DK // 2
    v = v_ref[...]
    for r in range(DEC_TILE):
        qc = q_t[:, r:r + 1]
        kc = k_t[:, r:r + 1]
        for hh in range(RET_HEADS):
            vrow = v[r:r + 1, hh * RET_DV:(hh + 1) * RET_DV]
            o = None
            for part in range(2):
                r0 = part * LANES + hh * half
                s_old = rs_ref[0, r, hh, part * half:(part + 1) * half, :]
                s_new = s_old * cdec_ref[hh] + kc[r0:r0 + half, :] * vrow
                rs_o[0, r, hh, part * half:(part + 1) * half, :] = s_new
                contrib = jnp.sum(qc[r0:r0 + half, :] * s_new, axis=0, keepdims=True)
                o = contrib if o is None else o + contrib
            obuf[r:r + 1, hh * RET_DV:(hh + 1) * RET_DV] = o
    o = obuf[...]
    rg = rg_ref[...]
    for hh in range(RET_HEADS):
        oh = _rms_rows(o[:, hh * RET_DV:(hh + 1) * RET_DV], rng_ref[hh:hh + 1, :])
        yb_ref[:, hh * RET_DV:(hh + 1) * RET_DV] = (
            _silu(rg[:, hh * RET_DV:(hh + 1) * RET_DV]) * oh).astype(yb_ref.dtype)

    xin = xbc_ref[...]
    sst = sconv_ref[...]
    n = SSD_CONV_DIM
    xc = scb_ref[...] + scw_ref[0:1, :] * sst[:, 0:n] + scw_ref[1:2, :] * sst[:, n:2 * n]
    xc = xc + scw_ref[2:3, :] * sst[:, 2 * n:] + scw_ref[3:4, :] * xin
    sconv_o[:, 0:2 * n] = sst[:, n:]
    sconv_o[:, 2 * n:] = xin
    xc = _silu(xc)
    xs = xc[:, :D_SSD]
    bm = xc[:, D_SSD:D_SSD + SSD_BC]
    cm = xc[:, D_SSD + SSD_BC:]
    dt = _softplus(dt_ref[...] + dtb_ref[...])
    dt_full = _dot_exact(dt, hexp_ref[...])
    eda = jnp.exp(dt_full * afull_ref[...])
    xdt_t = _transpose_rows_exact(xs * dt_full, eye_ref)
    eda_t = _transpose_rows_exact(eda, eye_ref)
    ybuf[...] = jnp.zeros_like(ybuf)
    for r in range(DEC_TILE):
        for hh in range(SSD_HEADS):
            g = hh // HEADS_PER_GROUP
            brow = bm[r:r + 1, g * SSD_DSTATE:(g + 1) * SSD_DSTATE]
            crow = cm[r:r + 1, g * SSD_DSTATE:(g + 1) * SSD_DSTATE]
            rows = slice(hh * SSD_HD, (hh + 1) * SSD_HD)
            s_new = ss_ref[0, r, hh] * eda_t[rows, r:r + 1] + xdt_t[rows, r:r + 1] * brow
            ss_o[0, r, hh] = s_new
            ycol = jnp.sum(s_new * crow, axis=1, keepdims=True)
            ybuf[rows, r:r + 1] = ycol
    y_t = ybuf[...]
    y = _dot_nt_exact(eye_ref[0:SUBLANES, 0:LANES], y_t)
    y = y + dfull_ref[...] * xs
    yc_ref[...] = _ssd_gate_norm(y, z_ref[...], sng_ref[...]).astype(yc_ref.dtype)


def _mix_decode(proj, states, layer, w, tabs):
    nb = proj.shape[0]
    grid = (nb // DEC_TILE,)
    col = lambda width, c0: pl.BlockSpec((DEC_TILE, width), lambda i: (i, c0 // width))
    const2 = lambda shape: pl.BlockSpec(shape, lambda i: (0, 0))
    const3 = lambda shape: pl.BlockSpec(shape, lambda i: (0, 0, 0))
    lconv, lh, rs, sconv, ss = states
    in_specs = [
        col(D_LRU, COL_LRU_X), col(D_LRU, COL_LRU_G), col(D_QK, COL_Q), col(D_QK, COL_K),
        col(D_RET, COL_V), col(D_RET, COL_RG), col(D_SSD, COL_Z), col(SSD_CONV_DIM, COL_XBC),
        col(LANES, COL_DT),
        pl.BlockSpec((DEC_TILE, (CONV_W - 1) * D_LRU), lambda i: (i, 0)),
        pl.BlockSpec((DEC_TILE, D_LRU), lambda i: (i, 0)),
        pl.BlockSpec((1, DEC_TILE, RET_HEADS, RET_DK, RET_DV), lambda i: (layer, i, 0, 0, 0)),
        pl.BlockSpec((DEC_TILE, (CONV_W - 1) * SSD_CONV_DIM), lambda i: (i, 0)),
        pl.BlockSpec((1, DEC_TILE, SSD_HEADS, SSD_HD, SSD_DSTATE), lambda i: (layer, i, 0, 0, 0)),
        const2((CONV_W, D_LRU)), const2((1, D_LRU)), const2((D_LRU, D_LRU)), const2((1, D_LRU)),
        const2((D_LRU, D_LRU)), const2((1, D_LRU)), const2((1, D_LRU)),
        const2((1, LANES)), const2((1, LANES)), const3((RET_HEADS, 1, 1)), const2((RET_HEADS, RET_DV)),
        const2((CONV_W, SSD_CONV_DIM)), const2((1, SSD_CONV_DIM)), const2((1, LANES)), const2((1, LANES)),
        const2((1, D_SSD)), const2((1, D_SSD)), const2((1, D_SSD)),
        const2((D_SSD, D_SSD)), const2((LANES, D_SSD)),
    ]
    out_specs = [
        pl.BlockSpec((DEC_TILE, D_LRU), lambda i: (i, 0)),
        pl.BlockSpec((DEC_TILE, D_RET), lambda i: (i, 0)),
        pl.BlockSpec((DEC_TILE, D_SSD), lambda i: (i, 0)),
        pl.BlockSpec((DEC_TILE, (CONV_W - 1) * D_LRU), lambda i: (i, 0)),
        pl.BlockSpec((DEC_TILE, D_LRU), lambda i: (i, 0)),
        pl.BlockSpec((1, DEC_TILE, RET_HEADS, RET_DK, RET_DV), lambda i: (0, i, 0, 0, 0)),
        pl.BlockSpec((DEC_TILE, (CONV_W - 1) * SSD_CONV_DIM), lambda i: (i, 0)),
        pl.BlockSpec((1, DEC_TILE, SSD_HEADS, SSD_HD, SSD_DSTATE), lambda i: (0, i, 0, 0, 0)),
    ]
    out_shape = [
        jax.ShapeDtypeStruct((nb, D_LRU), BF16),
        jax.ShapeDtypeStruct((nb, D_RET), BF16),
        jax.ShapeDtypeStruct((nb, D_SSD), BF16),
        jax.ShapeDtypeStruct((nb, (CONV_W - 1) * D_LRU), F32),
        jax.ShapeDtypeStruct((nb, D_LRU), F32),
        jax.ShapeDtypeStruct((1, nb, RET_HEADS, RET_DK, RET_DV), F32),
        jax.ShapeDtypeStruct((nb, (CONV_W - 1) * SSD_CONV_DIM), F32),
        jax.ShapeDtypeStruct((1, nb, SSD_HEADS, SSD_HD, SSD_DSTATE), F32),
    ]
    return pl.pallas_call(
        _mix_decode_kernel, grid=grid, in_specs=in_specs, out_specs=out_specs, out_shape=out_shape,
        scratch_shapes=[pltpu.VMEM((DEC_TILE, D_RET), F32), pltpu.VMEM((D_SSD, LANES), F32)],
        compiler_params=_cparams("arbitrary"),
        name="mix_decode",
    )(proj, proj, proj, proj, proj, proj, proj, proj, proj,
      lconv, lh, rs, sconv, ss,
      w["lru_conv_w"], w["lru_conv_b"], w["lru_wr"], w["lru_br"], w["lru_wi"], w["lru_bi"], w["lru_lambda"],
      tabs["cos_s"], tabs["sin_s"], tabs["cdec1"], w["ret_norm_g"],
      w["ssd_conv_w"], w["ssd_conv_b"], w["ssd_dt_bias"], w["ssd_a_log"], w["ssd_a_full"], w["ssd_d_full"],
      w["ssd_norm_g"], tabs["eye"], tabs["hexp"])


def _qk_perm():
    half = RET_DK // 2
    idx = np.empty((D_QK,), np.int32)
    for part in range(2):
        for h in range(RET_HEADS):
            for j in range(half):
                idx[part * LANES + h * half + j] = h * RET_DK + part * half + j
    return idx


def _block_diag(wh):
    nh, hd, _ = wh.shape
    eye = jnp.eye(nh, dtype=wh.dtype)
    return (eye[:, None, :, None] * wh[:, :, None, :]).reshape(nh * hd, nh * hd)


def _layer_weights(l, p):
    perm = _qk_perm()
    w_in = p["w_in"][l]
    w_q = w_in[:, COL_Q:COL_Q + D_QK][:, perm]
    w_k = w_in[:, COL_K:COL_K + D_QK][:, perm]
    w_pad = jnp.concatenate(
        [w_in[:, :COL_Q], w_q, w_k, w_in[:, COL_V:],
         jnp.zeros((D_MODEL, D_IN_PAD - w_in.shape[1]), w_in.dtype)], axis=1).astype(BF16)
    pad_lanes = lambda v: jnp.concatenate([v, jnp.zeros((LANES - v.shape[0],), v.dtype)])[None, :]
    return {
        "norm_mix_g": p["norm_mix_g"][l][None, :],
        "w_in": w_pad,
        "lru_conv_w": p["lru_conv_w"][l],
        "lru_conv_b": p["lru_conv_b"][l][None, :],
        "lru_wr": _block_diag(p["lru_wr"][l]).astype(BF16),
        "lru_br": p["lru_br"][l][None, :],
        "lru_wi": _block_diag(p["lru_wi"][l]).astype(BF16),
        "lru_bi": p["lru_bi"][l][None, :],
        "lru_lambda": p["lru_lambda"][l][None, :],
        "ret_norm_g": p["ret_norm_g"][l],
        "ssd_conv_w": p["ssd_conv_w"][l],
        "ssd_conv_b": p["ssd_conv_b"][l][None, :],
        "ssd_dt_bias": pad_lanes(p["ssd_dt_bias"][l]),
        "ssd_a_log": pad_lanes(p["ssd_a_log"][l]),
        "ssd_a_full": jnp.repeat(-jnp.exp(p["ssd_a_log"][l]), SSD_HD)[None, :],
        "ssd_d_full": jnp.repeat(p["ssd_d"][l], SSD_HD)[None, :],
        "ssd_norm_g": p["ssd_norm_g"][l][None, :],
        "w_out": p["w_out"][l].astype(BF16),
        "norm_ffn_g": p["norm_ffn_g"][l][None, :],
        "ffn_w_up": p["ffn_w_up"][l].astype(BF16),
        "ffn_conv_w": p["ffn_conv_w"][l],
        "ffn_conv_b": p["ffn_conv_b"][l][None, :],
        "ffn_w_down": p["ffn_w_down"][l].astype(BF16),
    }


def _tables(seqlen, past_len):
    half = RET_DK // 2
    freqs = ROPE_BASE ** (-jnp.arange(half, dtype=F32) / half)

    def cs(pos):
        ang = pos.astype(F32)[:, None] * freqs[None, :]
        return jnp.tile(jnp.cos(ang), (1, RET_HEADS)), jnp.tile(jnp.sin(ang), (1, RET_HEADS))

    cos_p, sin_p = cs(jnp.arange(seqlen, dtype=jnp.int32))
    cos_s, sin_s = cs(past_len + jnp.arange(1, dtype=jnp.int32))
    log_g = jnp.log1p(-jnp.exp2(-5.0 - jnp.arange(RET_HEADS, dtype=F32)))
    idx = jnp.arange(CHUNK, dtype=F32)
    diff = idx[:, None] - idx[None, :]
    decay = jnp.exp(jnp.where(diff[None] >= 0, diff[None] * log_g[:, None, None], -jnp.inf))
    q_dec = jnp.exp((idx[None, :] + 1.0) * log_g[:, None])
    k_dec = jnp.exp((CHUNK - 1.0 - idx[None, :]) * log_g[:, None])
    lanes_of = lambda t: jnp.tile(jnp.repeat(t.T, half, axis=1), (1, 2))
    hexp = (jnp.arange(LANES)[:, None] == (jnp.arange(D_SSD)[None, :] // SSD_HD)).astype(F32)
    return {
        "cos_p": cos_p, "sin_p": sin_p, "cos_s": cos_s, "sin_s": sin_s,
        "decay": decay, "qdec": lanes_of(q_dec), "kdec": lanes_of(k_dec),
        "cdec": jnp.exp(CHUNK * log_g)[:, None, None],
        "cdec1": jnp.exp(log_g)[:, None, None],
        "tril": jnp.tril(jnp.ones((CHUNK, CHUNK), F32)),
        "eye": jnp.eye(D_SSD, dtype=F32),
        "hexp": hexp,
    }


def _prompt_stack(x, weights, tabs, fg, tm=512, tl=256):
    bsz, seqlen, _ = x.shape
    x2d = x.reshape(bsz * seqlen, D_MODEL)
    outs = [[] for _ in range(6)]
    for l, w in enumerate(weights):
        proj = _inproj(x2d, w["norm_mix_g"], w["w_in"], tm=tm)
        ya, lconv, lh = _lru_prompt(proj, bsz, seqlen, w, tl=tl)
        yb, rs = _ret_prompt(proj, bsz, seqlen, w, tabs)
        yc, sconv, ss = _ssd_prompt(proj, bsz, seqlen, w, tabs)
        x2d, fconv = _ffn(x2d, ya, yb, yc, w, fg, bsz=bsz, seqlen=seqlen, tm=tm, decode=False,
                          final_norm=(l == len(weights) - 1))
        for lst, val in zip(outs, (lconv, lh[:, 0, :], rs, sconv, ss, fconv)):
            lst.append(val)
    return x2d.reshape(bsz, seqlen, D_MODEL), [jnp.stack(o, axis=0) for o in outs]


def _sample_stack(x, states, weights, tabs, fg):
    nb = x.shape[0]
    x2d = x.reshape(nb, D_MODEL)
    st_lconv, st_lh, st_rs, st_sconv, st_ss, st_fconv = states
    outs = [[] for _ in range(6)]
    for l, w in enumerate(weights):
        proj = _inproj(x2d, w["norm_mix_g"], w["w_in"], tm=nb)
        lconv_in = st_lconv[l].reshape(nb, (CONV_W - 1) * D_LRU)
        sconv_in = st_sconv[l].reshape(nb, (CONV_W - 1) * SSD_CONV_DIM)
        ya, yb, yc, lconv, lh, rs, sconv, ss = _mix_decode(
            proj, (lconv_in, st_lh[l], st_rs, sconv_in, st_ss), l, w, tabs)
        fst = st_fconv[l].reshape(nb, (FFN_CONV_W - 1) * D_FF)
        x2d, fconv = _ffn(x2d, ya, yb, yc, w, fg, bsz=nb, seqlen=1, tm=nb, decode=True,
                          final_norm=(l == len(weights) - 1), st=fst)
        vals = (lconv.reshape(nb, CONV_W - 1, D_LRU), lh, rs[0],
                sconv.reshape(nb, CONV_W - 1, SSD_CONV_DIM), ss[0],
                fconv.reshape(nb, FFN_CONV_W - 1, D_FF))
        for lst, val in zip(outs, vals):
            lst.append(val)
    return x2d.reshape(nb, 1, D_MODEL), [jnp.stack(o, axis=0) for o in outs]


def kernel(x_prompt, x_sample, state_lru_conv, state_lru_h, state_ret, state_ssd_conv, state_ssd, state_ffn_conv,
           norm_mix_g, w_in, lru_conv_w, lru_conv_b, lru_wr, lru_br, lru_wi, lru_bi, lru_lambda,
           ret_norm_g, ssd_conv_w, ssd_conv_b, ssd_dt_bias, ssd_a_log, ssd_d, ssd_norm_g,
           w_out, norm_ffn_g, ffn_w_up, ffn_conv_w, ffn_conv_b, ffn_w_down, norm_final_g):
    p = {"norm_mix_g": norm_mix_g, "w_in": w_in, "lru_conv_w": lru_conv_w, "lru_conv_b": lru_conv_b,
         "lru_wr": lru_wr, "lru_br": lru_br, "lru_wi": lru_wi, "lru_bi": lru_bi, "lru_lambda": lru_lambda,
         "ret_norm_g": ret_norm_g, "ssd_conv_w": ssd_conv_w, "ssd_conv_b": ssd_conv_b,
         "ssd_dt_bias": ssd_dt_bias, "ssd_a_log": ssd_a_log, "ssd_d": ssd_d, "ssd_norm_g": ssd_norm_g,
         "w_out": w_out, "norm_ffn_g": norm_ffn_g, "ffn_w_up": ffn_w_up, "ffn_conv_w": ffn_conv_w,
         "ffn_conv_b": ffn_conv_b, "ffn_w_down": ffn_w_down}
    depth = w_in.shape[0]
    weights = [_layer_weights(l, p) for l in range(depth)]
    past_len = 16384
    tabs = _tables(x_prompt.shape[1], past_len)
    fg = norm_final_g[None, :]

    y_p, ns_p = _prompt_stack(x_prompt, weights, tabs, fg)
    y_s, ns_s = _sample_stack(
        x_sample, (state_lru_conv, state_lru_h, state_ret, state_ssd_conv, state_ssd, state_ffn_conv),
        weights, tabs, fg)
    lru_conv_p, lru_h_p, ret_p, ssd_conv_p, ssd_p, ffn_conv_p = ns_p
    lru_conv_s, lru_h_s, ret_s, ssd_conv_s, ssd_s, ffn_conv_s = ns_s
    return (y_p, y_s, lru_conv_p, lru_conv_s, lru_h_p, lru_h_s, ret_p, ret_s,
            ssd_conv_p, ssd_conv_s, ssd_p, ssd_s, ffn_conv_p, ffn_conv_s)
```

```python
import functools

import jax
import jax.numpy as jnp
import numpy as np
from jax import lax
from jax.experimental import pallas as pl
from jax.experimental.pallas import tpu as pltpu

F32 = jnp.float32
BF16 = jnp.bfloat16

D_MODEL = 1024
DEPTH = 4
D_LRU = 512
LRU_HEADS = 8
LRU_HD = D_LRU // LRU_HEADS
LRU_C = 8.0
CONV_W = 4
RET_HEADS = 4
RET_DK = 64
RET_DV = 128
D_RET = RET_HEADS * RET_DV
D_QK = RET_HEADS * RET_DK
ROPE_BASE = 10000.0
SSD_HD = 64
D_SSD = 512
SSD_HEADS = 8
SSD_GROUPS = 2
SSD_DSTATE = 128
SSD_BC = SSD_GROUPS * SSD_DSTATE
SSD_CONV_DIM = D_SSD + 2 * SSD_BC
D_MIX = D_LRU + D_RET + D_SSD
D_FF = 2816
FFN_CONV_W = 3
EPS = 1e-6
CHUNK = 128

COL_LRU_X = 0
COL_LRU_G = 512
COL_Q = 1024
COL_K = 1280
COL_V = 1536
COL_RG = 2048
COL_Z = 2560
COL_XBC = 3072
COL_DT = 4096
D_IN_PAD = 4224

LANES = 128
SUBLANES = 8
VMEM_LIMIT_BYTES = 56 * 1024 * 1024


def _cparams(*sem):
    return pltpu.CompilerParams(dimension_semantics=sem, vmem_limit_bytes=VMEM_LIMIT_BYTES)


def _dot(a, b):
    return jnp.dot(a, b, preferred_element_type=F32)


def _dot_nt(a, b):
    return lax.dot_general(a, b, (((1,), (1,)), ((), ())), preferred_element_type=F32)


def _dot_tn(a, b):
    return lax.dot_general(a, b, (((0,), (0,)), ((), ())), preferred_element_type=F32)


def _dot_exact(a, b):
    return jnp.dot(a, b, preferred_element_type=F32, precision=lax.Precision.HIGHEST)


def _dot_nt_exact(a, b):
    return lax.dot_general(a, b, (((1,), (1,)), ((), ())), preferred_element_type=F32,
                           precision=lax.Precision.HIGHEST)


def _rms_rows(x, g):
    return x * lax.rsqrt(jnp.mean(x * x, axis=-1, keepdims=True) + EPS) * g


def _sigmoid(x):
    return 1.0 / (1.0 + jnp.exp(-x))


def _silu(x):
    return x * _sigmoid(x)


def _gelu(x):
    return jax.nn.gelu(x, approximate=True)


def _softplus(x):
    return jnp.maximum(x, 0.0) + jnp.log1p(jnp.exp(-jnp.abs(x)))


IN_COL_CHUNK = 512


def _lspec(arr, layer):
    _, r, c = arr.shape
    return pl.BlockSpec((None, r, c), lambda *_: (layer, 0, 0))


def _inproj_kernel(x_ref, g_ref, wm_ref, wqk_ref, wdt_ref, o_ref):
    hn = _rms_rows(x_ref[...], g_ref[...]).astype(BF16)
    for c0 in range(0, COL_DT, IN_COL_CHUNK):
        if c0 == COL_Q:
            o_ref[:, c0:c0 + IN_COL_CHUNK] = _dot(hn, wqk_ref[...])
        else:
            o_ref[:, c0:c0 + IN_COL_CHUNK] = _dot(hn, wm_ref[:, c0:c0 + IN_COL_CHUNK])
    o_ref[:, COL_DT:] = _dot(hn, wdt_ref[...])


def _inproj(x2d, w, layer, tm):
    t = x2d.shape[0]
    return pl.pallas_call(
        _inproj_kernel,
        grid=(t // tm,),
        in_specs=[
            pl.BlockSpec((tm, D_MODEL), lambda i: (i, 0)),
            _lspec(w["norm_mix_g"], layer),
            _lspec(w["w_in"], layer),
            _lspec(w["w_qk"], layer),
            _lspec(w["w_dt"], layer),
        ],
        out_specs=pl.BlockSpec((tm, D_IN_PAD), lambda i: (i, 0)),
        out_shape=jax.ShapeDtypeStruct((t, D_IN_PAD), F32),
        compiler_params=_cparams("arbitrary"),
        name="inproj",
    )(x2d, w["norm_mix_g"], w["w_in"], w["w_qk"], w["w_dt"])


def _lru_gates(xc, wr_ref, br_ref, wi_ref, bi_ref, lam_ref):
    xcb = xc.astype(BF16)
    r = _sigmoid(_dot(xcb, wr_ref[...]) + br_ref[...])
    ig = _sigmoid(_dot(xcb, wi_ref[...]) + bi_ref[...])
    log_a = (-LRU_C * r) * _softplus(-lam_ref[...])
    a = jnp.exp(log_a)
    b = jnp.sqrt(-jnp.tanh(log_a) * (a * a + 1.0)) * (ig * xc)
    return a, b


def _scan_rows(a, b):
    n = a.shape[0]
    row = lax.broadcasted_iota(jnp.int32, a.shape, 0)
    d = 1
    while d < n:
        if d < SUBLANES:
            keep = row >= d
            a_sh = jnp.where(keep, pltpu.roll(a, d, axis=0), 1.0)
            b_sh = jnp.where(keep, pltpu.roll(b, d, axis=0), 0.0)
        else:
            a_sh = jnp.concatenate([jnp.ones((d, a.shape[1]), F32), a[:n - d]], axis=0)
            b_sh = jnp.concatenate([jnp.zeros((d, a.shape[1]), F32), b[:n - d]], axis=0)
        b = a * b_sh + b
        a = a * a_sh
        d *= 2
    return a, b


def _rotary_split(x, cos, sin):
    xa, xb = x[:, :LANES], x[:, LANES:]
    return jnp.concatenate([xa * cos - xb * sin, xa * sin + xb * cos], axis=1)


def _head_of_qk_lane(shape, axis):
    lane = lax.broadcasted_iota(jnp.int32, shape, axis)
    return (lane % LANES) // (RET_DK // 2)


def _lru_prompt_kernel(x_ref, gate_ref, cw_ref, cb_ref, wr_ref, br_ref, wi_ref, bi_ref, lam_ref,
                       y_ref, conv_ref, h_ref, xbuf, hcar):
    t = pl.program_id(1)
    tl = x_ref.shape[0]

    @pl.when(t == 0)
    def _():
        xbuf[0:SUBLANES, :] = jnp.zeros((SUBLANES, D_LRU), F32)
        hcar[...] = jnp.zeros_like(hcar)

    x = x_ref[...]
    xbuf[SUBLANES:SUBLANES + tl, :] = x
    xc = cb_ref[...] + cw_ref[0:1, :] * xbuf[SUBLANES - 3:SUBLANES - 3 + tl, :]
    xc = xc + cw_ref[1:2, :] * xbuf[SUBLANES - 2:SUBLANES - 2 + tl, :]
    xc = xc + cw_ref[2:3, :] * xbuf[SUBLANES - 1:SUBLANES - 1 + tl, :]
    xc = xc + cw_ref[3:4, :] * x
    tail = xbuf[tl:tl + SUBLANES, :]
    xbuf[0:SUBLANES, :] = tail
    conv_ref[0] = tail[SUBLANES - (CONV_W - 1):, :]

    a, b = _lru_gates(xc, wr_ref, br_ref, wi_ref, bi_ref, lam_ref)
    a_cum, h0 = _scan_rows(a, b)
    h = h0 + a_cum * hcar[...]
    hlast = h[tl - 1:tl, :]
    hcar[...] = hlast
    h_ref[0] = hlast
    y_ref[...] = (h * _gelu(gate_ref[...])).astype(y_ref.dtype)


LRU_PARAMS = ("lru_conv_w", "lru_conv_b", "lru_wr", "lru_br", "lru_wi", "lru_bi", "lru_lambda")


def _lru_prompt(proj, bsz, seqlen, w, layer, tl):
    nt = seqlen // tl
    row = lambda b, t: b * nt + t
    return pl.pallas_call(
        _lru_prompt_kernel,
        grid=(bsz, nt),
        in_specs=[
            pl.BlockSpec((tl, D_LRU), lambda b, t: (row(b, t), COL_LRU_X // D_LRU)),
            pl.BlockSpec((tl, D_LRU), lambda b, t: (row(b, t), COL_LRU_G // D_LRU)),
        ] + [_lspec(w[n], layer) for n in LRU_PARAMS],
        out_specs=[
            pl.BlockSpec((tl, D_LRU), lambda b, t: (row(b, t), 0)),
            pl.BlockSpec((1, CONV_W - 1, D_LRU), lambda b, t: (b, 0, 0)),
            pl.BlockSpec((1, 1, D_LRU), lambda b, t: (b, 0, 0)),
        ],
        out_shape=[
            jax.ShapeDtypeStruct((bsz * seqlen, D_LRU), BF16),
            jax.ShapeDtypeStruct((bsz, CONV_W - 1, D_LRU), F32),
            jax.ShapeDtypeStruct((bsz, 1, D_LRU), F32),
        ],
        scratch_shapes=[
            pltpu.VMEM((tl + SUBLANES, D_LRU), F32),
            pltpu.VMEM((1, D_LRU), F32),
        ],
        compiler_params=_cparams("arbitrary", "arbitrary"),
        name="lru_prompt",
    )(proj, proj, *[w[n] for n in LRU_PARAMS])


def _ret_prompt_kernel(q_ref, k_ref, v_ref, g_ref, cos_ref, sin_ref, decay_ref, qdec_ref, kdec_ref,
                       cdec_ref, ng_ref, y_ref, s_ref, state):
    c = pl.program_id(1)

    @pl.when(c == 0)
    def _():
        state[...] = jnp.zeros_like(state)

    cos = cos_ref[...]
    sin = sin_ref[...]
    q = _rotary_split(q_ref[...], cos, sin)
    k = _rotary_split(k_ref[...], cos, sin) * (RET_DK ** -0.5)
    kb = k.astype(BF16)
    qd = q * qdec_ref[...]
    kd = k * kdec_ref[...]
    head = _head_of_qk_lane(q.shape, 1)
    for h in range(RET_HEADS):
        sel = head == h
        vh = v_ref[:, h * RET_DV:(h + 1) * RET_DV].astype(BF16)
        scores = _dot_nt(jnp.where(sel, q, 0.0).astype(BF16), kb) * decay_ref[h]
        o = _dot(scores.astype(BF16), vh)
        o = o + _dot(jnp.where(sel, qd, 0.0).astype(BF16), state[h].astype(BF16))
        state[h] = state[h] * cdec_ref[h] + _dot_tn(jnp.where(sel, kd, 0.0).astype(BF16), vh)
        o = _rms_rows(o, ng_ref[h:h + 1, :])
        gh = g_ref[:, h * RET_DV:(h + 1) * RET_DV]
        y_ref[:, h * RET_DV:(h + 1) * RET_DV] = (_silu(gh) * o).astype(y_ref.dtype)
        half = RET_DK // 2
        s_ref[0, h, 0:half, :] = state[h, h * half:(h + 1) * half, :]
        s_ref[0, h, half:RET_DK, :] = state[h, LANES + h * half:LANES + (h + 1) * half, :]


def _ret_prompt(proj, bsz, seqlen, w, layer, tabs):
    nc = seqlen // CHUNK
    row = lambda b, c: b * nc + c
    const2 = lambda shape: pl.BlockSpec(shape, lambda b, c: (0, 0))
    const3 = lambda shape: pl.BlockSpec(shape, lambda b, c: (0, 0, 0))
    return pl.pallas_call(
        _ret_prompt_kernel,
        grid=(bsz, nc),
        in_specs=[
            pl.BlockSpec((CHUNK, D_QK), lambda b, c: (row(b, c), COL_Q // D_QK)),
            pl.BlockSpec((CHUNK, D_QK), lambda b, c: (row(b, c), COL_K // D_QK)),
            pl.BlockSpec((CHUNK, D_RET), lambda b, c: (row(b, c), COL_V // D_RET)),
            pl.BlockSpec((CHUNK, D_RET), lambda b, c: (row(b, c), COL_RG // D_RET)),
            pl.BlockSpec((CHUNK, LANES), lambda b, c: (c, 0)),
            pl.BlockSpec((CHUNK, LANES), lambda b, c: (c, 0)),
            const3((RET_HEADS, CHUNK, CHUNK)),
            const2((CHUNK, D_QK)),
            const2((CHUNK, D_QK)),
            const3((RET_HEADS, 1, 1)),
            _lspec(w["ret_norm_g"], layer),
        ],
        out_specs=[
            pl.BlockSpec((CHUNK, D_RET), lambda b, c: (row(b, c), 0)),
            pl.BlockSpec((1, RET_HEADS, RET_DK, RET_DV), lambda b, c: (b, 0, 0, 0)),
        ],
        out_shape=[
            jax.ShapeDtypeStruct((bsz * seqlen, D_RET), BF16),
            jax.ShapeDtypeStruct((bsz, RET_HEADS, RET_DK, RET_DV), F32),
        ],
        scratch_shapes=[pltpu.VMEM((RET_HEADS, D_QK, RET_DV), F32)],
        compiler_params=_cparams("arbitrary", "arbitrary"),
        name="ret_prompt",
    )(proj, proj, proj, proj, tabs["cos_p"], tabs["sin_p"], tabs["decay"], tabs["qdec"],
      tabs["kdec"], tabs["cdec"], w["ret_norm_g"])


HEADS_PER_GROUP = SSD_HEADS // SSD_GROUPS
PAIRS = SSD_HEADS // 2


def _ssd_prompt_kernel(z_ref, xbc_ref, dt_ref, cw_ref, cb_ref, dtb_ref, alog_ref, dfull_ref, ng_ref,
                       tril_ref, y_ref, conv_ref, s_ref, xbuf, state):
    c = pl.program_id(1)

    @pl.when(c == 0)
    def _():
        xbuf[0:SUBLANES, :] = jnp.zeros((SUBLANES, SSD_CONV_DIM), F32)
        state[...] = jnp.zeros_like(state)

    x = xbc_ref[...]
    xbuf[SUBLANES:SUBLANES + CHUNK, :] = x
    xc = cb_ref[...] + cw_ref[0:1, :] * xbuf[SUBLANES - 3:SUBLANES - 3 + CHUNK, :]
    xc = xc + cw_ref[1:2, :] * xbuf[SUBLANES - 2:SUBLANES - 2 + CHUNK, :]
    xc = xc + cw_ref[2:3, :] * xbuf[SUBLANES - 1:SUBLANES - 1 + CHUNK, :]
    xc = xc + cw_ref[3:4, :] * x
    tail = xbuf[CHUNK:CHUNK + SUBLANES, :]
    xbuf[0:SUBLANES, :] = tail
    conv_ref[0] = tail[SUBLANES - (CONV_W - 1):, :]
    xc = _silu(xc)
    xs = xc[:, :D_SSD]
    bm = xc[:, D_SSD:D_SSD + SSD_BC].astype(BF16)
    cm = xc[:, D_SSD + SSD_BC:].astype(BF16)

    dt = _softplus(dt_ref[...] + dtb_ref[...])
    da = dt * (-jnp.exp(alog_ref[...]))
    cum = _dot_exact(tril_ref[...], da)
    cum_t = cum.T
    dt_t = dt.T
    ecum = jnp.exp(cum)
    wlast = jnp.exp(cum[CHUNK - 1:CHUNK, :] - cum) * dt
    elast = jnp.exp(cum_t[:, CHUNK - 1:CHUNK])

    causal = (lax.broadcasted_iota(jnp.int32, (CHUNK, CHUNK), 0)
              >= lax.broadcasted_iota(jnp.int32, (CHUNK, CHUNK), 1))
    lane = lax.broadcasted_iota(jnp.int32, (CHUNK, LANES), 1)
    lo = lane < SSD_HD
    rowi = lax.broadcasted_iota(jnp.int32, (2 * SSD_HD, SSD_DSTATE), 0)
    row_lo = rowi < SSD_HD

    cbw = [_dot_nt(cm[:, g * SSD_DSTATE:(g + 1) * SSD_DSTATE], bm[:, g * SSD_DSTATE:(g + 1) * SSD_DSTATE])
           for g in range(SSD_GROUPS)]
    for p in range(PAIRS):
        g = (2 * p) // HEADS_PER_GROUP
        xp = xs[:, p * LANES:(p + 1) * LANES]
        bg = bm[:, g * SSD_DSTATE:(g + 1) * SSD_DSTATE]
        cg = cm[:, g * SSD_DSTATE:(g + 1) * SSD_DSTATE]
        y = None
        for j, sel in ((0, lo), (1, jnp.logical_not(lo))):
            h = 2 * p + j
            seg = cum[:, h:h + 1] - cum_t[h:h + 1, :]
            lmask = jnp.exp(jnp.where(causal, seg, -jnp.inf))
            wgt = cbw[g] * lmask * dt_t[h:h + 1, :]
            yj = _dot(wgt.astype(BF16), jnp.where(sel, xp, 0.0).astype(BF16))
            y = yj if y is None else y + yj
        h0, h1 = 2 * p, 2 * p + 1
        sp = state[p]
        y = y + _dot_nt(cg, sp.astype(BF16)) * jnp.where(lo, ecum[:, h0:h0 + 1], ecum[:, h1:h1 + 1])
        wl = jnp.where(lo, wlast[:, h0:h0 + 1], wlast[:, h1:h1 + 1])
        sp = sp * jnp.where(row_lo, elast[h0:h0 + 1, :], elast[h1:h1 + 1, :]) + _dot_tn(
            (xp * wl).astype(BF16), bg)
        state[p] = sp
        s_ref[0, h0] = sp[:SSD_HD, :]
        s_ref[0, h1] = sp[SSD_HD:, :]
        y_ref[:, p * LANES:(p + 1) * LANES] = (y + dfull_ref[:, p * LANES:(p + 1) * LANES] * xp).astype(
            y_ref.dtype)


def _ssd_gate_norm(y, z, ng):
    gated = y * _silu(z)
    gw = D_SSD // SSD_GROUPS
    outs = []
    for g in range(SSD_GROUPS):
        outs.append(_rms_rows(gated[:, g * gw:(g + 1) * gw], ng[:, g * gw:(g + 1) * gw]))
    return jnp.concatenate(outs, axis=1)


def _ssd_prompt_kernel_full(z_ref, xbc_ref, dt_ref, cw_ref, cb_ref, dtb_ref, alog_ref, dfull_ref, ng_ref,
                            tril_ref, y_ref, conv_ref, s_ref, xbuf, state, ybuf):
    _ssd_prompt_kernel(z_ref, xbc_ref, dt_ref, cw_ref, cb_ref, dtb_ref, alog_ref, dfull_ref, ng_ref,
                       tril_ref, ybuf, conv_ref, s_ref, xbuf, state)
    y_ref[...] = _ssd_gate_norm(ybuf[...], z_ref[...], ng_ref[...]).astype(y_ref.dtype)


SSD_PROMPT_PARAMS = ("ssd_conv_w", "ssd_conv_b", "ssd_dt_bias", "ssd_a_log", "ssd_d_full", "ssd_norm_g")


def _ssd_prompt(proj, bsz, seqlen, w, layer, tabs):
    nc = seqlen // CHUNK
    row = lambda b, c: b * nc + c
    const2 = lambda shape: pl.BlockSpec(shape, lambda b, c: (0, 0))
    return pl.pallas_call(
        _ssd_prompt_kernel_full,
        grid=(bsz, nc),
        in_specs=[
            pl.BlockSpec((CHUNK, D_SSD), lambda b, c: (row(b, c), COL_Z // D_SSD)),
            pl.BlockSpec((CHUNK, SSD_CONV_DIM), lambda b, c: (row(b, c), COL_XBC // SSD_CONV_DIM)),
            pl.BlockSpec((CHUNK, LANES), lambda b, c: (row(b, c), COL_DT // LANES)),
        ] + [_lspec(w[n], layer) for n in SSD_PROMPT_PARAMS] + [const2((CHUNK, CHUNK))],
        out_specs=[
            pl.BlockSpec((CHUNK, D_SSD), lambda b, c: (row(b, c), 0)),
            pl.BlockSpec((1, CONV_W - 1, SSD_CONV_DIM), lambda b, c: (b, 0, 0)),
            pl.BlockSpec((1, SSD_HEADS, SSD_HD, SSD_DSTATE), lambda b, c: (b, 0, 0, 0)),
        ],
        out_shape=[
            jax.ShapeDtypeStruct((bsz * seqlen, D_SSD), BF16),
            jax.ShapeDtypeStruct((bsz, CONV_W - 1, SSD_CONV_DIM), F32),
            jax.ShapeDtypeStruct((bsz, SSD_HEADS, SSD_HD, SSD_DSTATE), F32),
        ],
        scratch_shapes=[
            pltpu.VMEM((CHUNK + SUBLANES, SSD_CONV_DIM), F32),
            pltpu.VMEM((PAIRS, 2 * SSD_HD, SSD_DSTATE), F32),
            pltpu.VMEM((CHUNK, D_SSD), F32),
        ],
        compiler_params=_cparams("arbitrary", "arbitrary"),
        name="ssd_prompt",
    )(proj, proj, proj, *[w[n] for n in SSD_PROMPT_PARAMS], tabs["tril"])


FF_TILES = ((0, 1536), (1536, 1280))
FF_TILE_MAX = max(wd for _, wd in FF_TILES)
FFN_PARAMS = ("w_out", "norm_ffn_g", "ffn_w_up", "ffn_conv_w", "ffn_conv_b", "ffn_w_down")


def _ffn_kernel(x_ref, ya_ref, yb_ref, yc_ref, wo_ref, g_ref, wup_ref, cw_ref, cb_ref, wdn_ref, fg_ref,
                *rest, decode, final_norm):
    if decode:
        st_ref, o_ref, conv_ref = rest
    else:
        o_ref, conv_ref, gbuf, halo = rest
    tm = x_ref.shape[0]
    x1 = x_ref[...]
    x1 = x1 + _dot(ya_ref[...], wo_ref[0:D_LRU, :])
    x1 = x1 + _dot(yb_ref[...], wo_ref[D_LRU:D_LRU + D_RET, :])
    x1 = x1 + _dot(yc_ref[...], wo_ref[D_LRU + D_RET:, :])
    hn = _rms_rows(x1, g_ref[...]).astype(BF16)

    if not decode:
        @pl.when(pl.program_id(1) == 0)
        def _():
            halo[...] = jnp.zeros_like(halo)

    acc = None
    for f0, fw in FF_TILES:
        gate = _dot(hn, wup_ref[:, f0:f0 + fw])
        up = _dot(hn, wup_ref[:, D_FF + f0:D_FF + f0 + fw])
        w0 = cw_ref[0:1, f0:f0 + fw]
        w1 = cw_ref[1:2, f0:f0 + fw]
        w2 = cw_ref[2:3, f0:f0 + fw]
        cb = cb_ref[:, f0:f0 + fw]
        if decode:
            s0 = st_ref[:, f0:f0 + fw]
            s1 = st_ref[:, D_FF + f0:D_FF + f0 + fw]
            conv = cb + w0 * s0 + w1 * s1 + w2 * gate
            conv_ref[:, f0:f0 + fw] = s1
            conv_ref[:, D_FF + f0:D_FF + f0 + fw] = gate
        else:
            gbuf[0:SUBLANES, 0:fw] = halo[:, f0:f0 + fw]
            gbuf[SUBLANES:SUBLANES + tm, 0:fw] = gate
            conv = cb + w0 * gbuf[SUBLANES - 2:SUBLANES - 2 + tm, 0:fw]
            conv = conv + w1 * gbuf[SUBLANES - 1:SUBLANES - 1 + tm, 0:fw] + w2 * gate
            tail = gbuf[tm:tm + SUBLANES, 0:fw]
            halo[:, f0:f0 + fw] = tail
            conv_ref[0, :, f0:f0 + fw] = tail[SUBLANES - (FFN_CONV_W - 1):, :]
        act = (_gelu(conv) * up).astype(BF16)
        d = _dot(act, wdn_ref[f0:f0 + fw, :])
        acc = d if acc is None else acc + d
    acc = x1 + acc
    if final_norm:
        acc = _rms_rows(acc, fg_ref[...])
    o_ref[...] = acc


def _ffn(x2d, ya, yb, yc, w, layer, fg, *, bsz, seqlen, tm, decode, final_norm, st=None):
    t = x2d.shape[0]
    kern = functools.partial(_ffn_kernel, decode=decode, final_norm=final_norm)
    if decode:
        grid = (t // tm,)
        rowmap = lambda i: (i, 0)
        sem = ("arbitrary",)
    else:
        nt = seqlen // tm
        grid = (bsz, nt)
        rowmap = lambda b, i: (b * nt + i, 0)
        sem = ("arbitrary", "arbitrary")
    in_specs = [
        pl.BlockSpec((tm, D_MODEL), rowmap),
        pl.BlockSpec((tm, D_LRU), rowmap),
        pl.BlockSpec((tm, D_RET), rowmap),
        pl.BlockSpec((tm, D_SSD), rowmap),
    ] + [_lspec(w[n], layer) for n in FFN_PARAMS] + [pl.BlockSpec((1, D_MODEL), lambda *_: (0, 0))]
    args = [x2d, ya, yb, yc] + [w[n] for n in FFN_PARAMS] + [fg]
    if decode:
        in_specs.append(pl.BlockSpec((tm, 2 * D_FF), rowmap))
        args.append(st)
        out_specs = [pl.BlockSpec((tm, D_MODEL), rowmap), pl.BlockSpec((tm, 2 * D_FF), rowmap)]
        out_shape = [jax.ShapeDtypeStruct((t, D_MODEL), F32), jax.ShapeDtypeStruct((t, 2 * D_FF), F32)]
        scratch = []
    else:
        out_specs = [pl.BlockSpec((tm, D_MODEL), rowmap),
                     pl.BlockSpec((1, FFN_CONV_W - 1, D_FF), lambda b, i: (b, 0, 0))]
        out_shape = [jax.ShapeDtypeStruct((t, D_MODEL), F32),
                     jax.ShapeDtypeStruct((bsz, FFN_CONV_W - 1, D_FF), F32)]
        scratch = [pltpu.VMEM((tm + SUBLANES, FF_TILE_MAX), F32), pltpu.VMEM((SUBLANES, D_FF), F32)]
    return pl.pallas_call(
        kern, grid=grid, in_specs=in_specs, out_specs=out_specs, out_shape=out_shape,
        scratch_shapes=scratch, compiler_params=_cparams(*sem),
        name="ffn_decode" if decode else "ffn_prompt",
    )(*args)


DEC_TILE = 8


def _transpose_rows_exact(x, eye_ref):
    n = x.shape[1]
    return _dot_nt_exact(eye_ref[0:n, 0:n], x)


MIX_DECODE_INPUTS = 34


def _mix_decode_kernel(*refs, n_alias):
    _mix_decode_body(*refs[:MIX_DECODE_INPUTS], *refs[MIX_DECODE_INPUTS + n_alias:])


def _mix_decode_body(lx_ref, lg_ref, q_ref, k_ref, v_ref, rg_ref, z_ref, xbc_ref, dt_ref,
                       lconv_ref, lh_ref, rs_ref, sconv_ref, ss_ref,
                       lcw_ref, lcb_ref, wr_ref, br_ref, wi_ref, bi_ref, lam_ref,
                       cos_ref, sin_ref, cdec_ref, rng_ref,
                       scw_ref, scb_ref, dtb_ref, alog_ref, afull_ref, dfull_ref, sng_ref,
                       eye_ref, hexp_ref,
                       ya_ref, yb_ref, yc_ref, lconv_o, lh_o, rs_o, sconv_o, ss_o,
                       obuf, ybuf):
    x = lx_ref[...]
    st = lconv_ref[...]
    xc = lcb_ref[...] + lcw_ref[0:1, :] * st[:, 0:D_LRU] + lcw_ref[1:2, :] * st[:, D_LRU:2 * D_LRU]
    xc = xc + lcw_ref[2:3, :] * st[:, 2 * D_LRU:] + lcw_ref[3:4, :] * x
    lconv_o[:, 0:2 * D_LRU] = st[:, D_LRU:]
    lconv_o[:, 2 * D_LRU:] = x
    a, b = _lru_gates(xc, wr_ref, br_ref, wi_ref, bi_ref, lam_ref)
    h = b + a * lh_ref[...]
    lh_o[...] = h
    ya_ref[...] = (h * _gelu(lg_ref[...])).astype(ya_ref.dtype)

    cos = cos_ref[...]
    sin = sin_ref[...]
    q = _rotary_split(q_ref[...], cos, sin)
    k = _rotary_split(k_ref[...], cos, sin) * (RET_DK ** -0.5)
    q_t = _transpose_rows_exact(q, eye_ref)
    k_t = _transpose_rows_exact(k, eye_ref)
    half = RET_DK // 2
    v = v_ref[...]
    for r in range(DEC_TILE):
        qc = q_t[:, r:r + 1]
        kc = k_t[:, r:r + 1]
        for hh in range(RET_HEADS):
            vrow = v[r:r + 1, hh * RET_DV:(hh + 1) * RET_DV]
            o = None
            for part in range(2):
                r0 = part * LANES + hh * half
                s_old = rs_ref[0, r, hh, part * half:(part + 1) * half, :]
                s_new = s_old * cdec_ref[hh] + kc[r0:r0 + half, :] * vrow
                rs_o[0, r, hh, part * half:(part + 1) * half, :] = s_new
                contrib = jnp.sum(qc[r0:r0 + half, :] * s_new, axis=0, keepdims=True)
                o = contrib if o is None else o + contrib
            obuf[r:r + 1, hh * RET_DV:(hh + 1) * RET_DV] = o
    o = obuf[...]
    rg = rg_ref[...]
    for hh in range(RET_HEADS):
        oh = _rms_rows(o[:, hh * RET_DV:(hh + 1) * RET_DV], rng_ref[hh:hh + 1, :])
        yb_ref[:, hh * RET_DV:(hh + 1) * RET_DV] = (
            _silu(rg[:, hh * RET_DV:(hh + 1) * RET_DV]) * oh).astype(yb_ref.dtype)

    xin = xbc_ref[...]
    sst = sconv_ref[...]
    n = SSD_CONV_DIM
    xc = scb_ref[...] + scw_ref[0:1, :] * sst[:, 0:n] + scw_ref[1:2, :] * sst[:, n:2 * n]
    xc = xc + scw_ref[2:3, :] * sst[:, 2 * n:] + scw_ref[3:4, :] * xin
    sconv_o[:, 0:2 * n] = sst[:, n:]
    sconv_o[:, 2 * n:] = xin
    xc = _silu(xc)
    xs = xc[:, :D_SSD]
    bm = xc[:, D_SSD:D_SSD + SSD_BC]
    cm = xc[:, D_SSD + SSD_BC:]
    dt = _softplus(dt_ref[...] + dtb_ref[...])
    dt_full = _dot_exact(dt, hexp_ref[...])
    eda = jnp.exp(dt_full * (-jnp.exp(afull_ref[...])))
    xdt_t = _transpose_rows_exact(xs * dt_full, eye_ref)
    eda_t = _transpose_rows_exact(eda, eye_ref)
    ybuf[...] = jnp.zeros_like(ybuf)
    for r in range(DEC_TILE):
        for hh in range(SSD_HEADS):
            g = hh // HEADS_PER_GROUP
            brow = bm[r:r + 1, g * SSD_DSTATE:(g + 1) * SSD_DSTATE]
            crow = cm[r:r + 1, g * SSD_DSTATE:(g + 1) * SSD_DSTATE]
            rows = slice(hh * SSD_HD, (hh + 1) * SSD_HD)
            s_new = ss_ref[0, r, hh] * eda_t[rows, r:r + 1] + xdt_t[rows, r:r + 1] * brow
            ss_o[0, r, hh] = s_new
            ycol = jnp.sum(s_new * crow, axis=1, keepdims=True)
            ybuf[rows, r:r + 1] = ycol
    y_t = ybuf[...]
    y = _dot_nt_exact(eye_ref[0:SUBLANES, 0:LANES], y_t)
    y = y + dfull_ref[...] * xs
    yc_ref[...] = _ssd_gate_norm(y, z_ref[...], sng_ref[...]).astype(yc_ref.dtype)


SSD_DECODE_PARAMS = ("ssd_conv_w", "ssd_conv_b", "ssd_dt_bias", "ssd_a_log", "ssd_a_log_full", "ssd_d_full",
                     "ssd_norm_g")


def _mix_decode(proj, states, layer, w, tabs, prev_full):
    nb = proj.shape[0]
    depth = states[2].shape[0]
    grid = (nb // DEC_TILE,)
    col = lambda width, c0: pl.BlockSpec((DEC_TILE, width), lambda i: (i, c0 // width))
    const2 = lambda shape: pl.BlockSpec(shape, lambda i: (0, 0))
    const3 = lambda shape: pl.BlockSpec(shape, lambda i: (0, 0, 0))
    lconv, lh, rs, sconv, ss = states
    ret_blk = pl.BlockSpec((1, DEC_TILE, RET_HEADS, RET_DK, RET_DV), lambda i: (layer, i, 0, 0, 0))
    ssd_blk = pl.BlockSpec((1, DEC_TILE, SSD_HEADS, SSD_HD, SSD_DSTATE), lambda i: (layer, i, 0, 0, 0))
    in_specs = [
        col(D_LRU, COL_LRU_X), col(D_LRU, COL_LRU_G), col(D_QK, COL_Q), col(D_QK, COL_K),
        col(D_RET, COL_V), col(D_RET, COL_RG), col(D_SSD, COL_Z), col(SSD_CONV_DIM, COL_XBC),
        col(LANES, COL_DT),
        pl.BlockSpec((DEC_TILE, (CONV_W - 1) * D_LRU), lambda i: (i, 0)),
        pl.BlockSpec((DEC_TILE, D_LRU), lambda i: (i, 0)),
        ret_blk,
        pl.BlockSpec((DEC_TILE, (CONV_W - 1) * SSD_CONV_DIM), lambda i: (i, 0)),
        ssd_blk,
    ] + [_lspec(w[n], layer) for n in LRU_PARAMS] + [
        const2((1, LANES)), const2((1, LANES)), const3((RET_HEADS, 1, 1)), _lspec(w["ret_norm_g"], layer),
    ] + [_lspec(w[n], layer) for n in SSD_DECODE_PARAMS] + [
        const2((D_SSD, D_SSD)), const2((LANES, D_SSD)),
    ]
    args = [proj] * 9 + [lconv, lh, rs, sconv, ss] + [w[n] for n in LRU_PARAMS] + [
        tabs["cos_s"], tabs["sin_s"], tabs["cdec1"], w["ret_norm_g"]] + [
        w[n] for n in SSD_DECODE_PARAMS] + [tabs["eye"], tabs["hexp"]]
    assert len(args) == MIX_DECODE_INPUTS
    aliases = {}
    if prev_full is not None:
        in_specs += [pl.BlockSpec(memory_space=pl.ANY)] * 2
        args += list(prev_full)
        aliases = {MIX_DECODE_INPUTS: 5, MIX_DECODE_INPUTS + 1: 7}
    out_specs = [
        pl.BlockSpec((DEC_TILE, D_LRU), lambda i: (i, 0)),
        pl.BlockSpec((DEC_TILE, D_RET), lambda i: (i, 0)),
        pl.BlockSpec((DEC_TILE, D_SSD), lambda i: (i, 0)),
        pl.BlockSpec((DEC_TILE, (CONV_W - 1) * D_LRU), lambda i: (i, 0)),
        pl.BlockSpec((DEC_TILE, D_LRU), lambda i: (i, 0)),
        ret_blk,
        pl.BlockSpec((DEC_TILE, (CONV_W - 1) * SSD_CONV_DIM), lambda i: (i, 0)),
        ssd_blk,
    ]
    out_shape = [
        jax.ShapeDtypeStruct((nb, D_LRU), BF16),
        jax.ShapeDtypeStruct((nb, D_RET), BF16),
        jax.ShapeDtypeStruct((nb, D_SSD), BF16),
        jax.ShapeDtypeStruct((nb, (CONV_W - 1) * D_LRU), F32),
        jax.ShapeDtypeStruct((nb, D_LRU), F32),
        jax.ShapeDtypeStruct((depth, nb, RET_HEADS, RET_DK, RET_DV), F32),
        jax.ShapeDtypeStruct((nb, (CONV_W - 1) * SSD_CONV_DIM), F32),
        jax.ShapeDtypeStruct((depth, nb, SSD_HEADS, SSD_HD, SSD_DSTATE), F32),
    ]
    return pl.pallas_call(
        functools.partial(_mix_decode_kernel, n_alias=len(aliases)),
        grid=grid, in_specs=in_specs, out_specs=out_specs, out_shape=out_shape,
        input_output_aliases=aliases,
        scratch_shapes=[pltpu.VMEM((DEC_TILE, D_RET), F32), pltpu.VMEM((D_SSD, LANES), F32)],
        compiler_params=_cparams("arbitrary"),
        name="mix_decode",
    )(*args)


def _qk_perm():
    half = RET_DK // 2
    idx = np.empty((D_QK,), np.int32)
    for part in range(2):
        for h in range(RET_HEADS):
            for j in range(half):
                idx[part * LANES + h * half + j] = h * RET_DK + part * half + j
    return idx


def _block_diag(wh):
    nh, hd, _ = wh.shape
    eye = jnp.eye(nh, dtype=wh.dtype)
    return (eye[:, None, :, None] * wh[:, :, None, :]).reshape(nh * hd, nh * hd)


def _prep_weights(p):
    perm = _qk_perm()
    w_in = p["w_in"]
    depth = w_in.shape[0]
    w_qk = jnp.concatenate([w_in[:, :, COL_Q:COL_Q + D_QK][:, :, perm],
                            w_in[:, :, COL_K:COL_K + D_QK][:, :, perm]], axis=2).astype(BF16)
    n_dt = w_in.shape[2] - COL_DT
    w_dt = jnp.concatenate([w_in[:, :, COL_DT:], jnp.zeros((depth, D_MODEL, LANES - n_dt), w_in.dtype)],
                           axis=2).astype(BF16)
    row = lambda v: v[:, None, :]
    pad_lanes = lambda v: row(jnp.concatenate([v, jnp.zeros((depth, LANES - v.shape[1]), v.dtype)], axis=1))
    return {
        "norm_mix_g": row(p["norm_mix_g"]),
        "w_in": w_in.astype(BF16),
        "w_qk": w_qk,
        "w_dt": w_dt,
        "lru_conv_w": p["lru_conv_w"],
        "lru_conv_b": row(p["lru_conv_b"]),
        "lru_wr": jax.vmap(_block_diag)(p["lru_wr"]).astype(BF16),
        "lru_br": row(p["lru_br"]),
        "lru_wi": jax.vmap(_block_diag)(p["lru_wi"]).astype(BF16),
        "lru_bi": row(p["lru_bi"]),
        "lru_lambda": row(p["lru_lambda"]),
        "ret_norm_g": p["ret_norm_g"],
        "ssd_conv_w": p["ssd_conv_w"],
        "ssd_conv_b": row(p["ssd_conv_b"]),
        "ssd_dt_bias": pad_lanes(p["ssd_dt_bias"]),
        "ssd_a_log": pad_lanes(p["ssd_a_log"]),
        "ssd_a_log_full": row(jnp.repeat(p["ssd_a_log"], SSD_HD, axis=1)),
        "ssd_d_full": row(jnp.repeat(p["ssd_d"], SSD_HD, axis=1)),
        "ssd_norm_g": row(p["ssd_norm_g"]),
        "w_out": p["w_out"].astype(BF16),
        "norm_ffn_g": row(p["norm_ffn_g"]),
        "ffn_w_up": p["ffn_w_up"].astype(BF16),
        "ffn_conv_w": p["ffn_conv_w"],
        "ffn_conv_b": row(p["ffn_conv_b"]),
        "ffn_w_down": p["ffn_w_down"].astype(BF16),
    }


def _tables(seqlen, past_len):
    half = RET_DK // 2
    freqs = ROPE_BASE ** (-jnp.arange(half, dtype=F32) / half)

    def cs(pos):
        ang = pos.astype(F32)[:, None] * freqs[None, :]
        return jnp.tile(jnp.cos(ang), (1, RET_HEADS)), jnp.tile(jnp.sin(ang), (1, RET_HEADS))

    cos_p, sin_p = cs(jnp.arange(seqlen, dtype=jnp.int32))
    cos_s, sin_s = cs(past_len + jnp.arange(1, dtype=jnp.int32))
    log_g = jnp.log1p(-jnp.exp2(-5.0 - jnp.arange(RET_HEADS, dtype=F32)))
    idx = jnp.arange(CHUNK, dtype=F32)
    diff = idx[:, None] - idx[None, :]
    decay = jnp.exp(jnp.where(diff[None] >= 0, diff[None] * log_g[:, None, None], -jnp.inf))
    q_dec = jnp.exp((idx[None, :] + 1.0) * log_g[:, None])
    k_dec = jnp.exp((CHUNK - 1.0 - idx[None, :]) * log_g[:, None])
    lanes_of = lambda t: jnp.tile(jnp.repeat(t.T, half, axis=1), (1, 2))
    hexp = (jnp.arange(LANES)[:, None] == (jnp.arange(D_SSD)[None, :] // SSD_HD)).astype(F32)
    return {
        "cos_p": cos_p, "sin_p": sin_p, "cos_s": cos_s, "sin_s": sin_s,
        "decay": decay, "qdec": lanes_of(q_dec), "kdec": lanes_of(k_dec),
        "cdec": jnp.exp(CHUNK * log_g)[:, None, None],
        "cdec1": jnp.exp(log_g)[:, None, None],
        "tril": jnp.tril(jnp.ones((CHUNK, CHUNK), F32)),
        "eye": jnp.eye(D_SSD, dtype=F32),
        "hexp": hexp,
    }


TM_PROMPT = 512
TL_LRU = 256


def _prompt_stack(x, w, tabs, fg, tm=TM_PROMPT, tl=TL_LRU):
    bsz, seqlen, _ = x.shape
    depth = w["w_in"].shape[0]
    x2d = x.reshape(bsz * seqlen, D_MODEL)
    outs = [[] for _ in range(6)]
    for l in range(depth):
        proj = _inproj(x2d, w, l, tm=tm)
        ya, lconv, lh = _lru_prompt(proj, bsz, seqlen, w, l, tl=tl)
        yb, rs = _ret_prompt(proj, bsz, seqlen, w, l, tabs)
        yc, sconv, ss = _ssd_prompt(proj, bsz, seqlen, w, l, tabs)
        x2d, fconv = _ffn(x2d, ya, yb, yc, w, l, fg, bsz=bsz, seqlen=seqlen, tm=tm, decode=False,
                          final_norm=(l == depth - 1))
        for lst, val in zip(outs, (lconv, lh[:, 0, :], rs, sconv, ss, fconv)):
            lst.append(val)
    return x2d.reshape(bsz, seqlen, D_MODEL), [jnp.stack(o, axis=0) for o in outs]


def _sample_stack(x, states, w, tabs, fg):
    nb = x.shape[0]
    depth = w["w_in"].shape[0]
    x2d = x.reshape(nb, D_MODEL)
    st_lconv, st_lh, st_rs, st_sconv, st_ss, st_fconv = states
    lconv_in = st_lconv.reshape(depth, nb, (CONV_W - 1) * D_LRU)
    sconv_in = st_sconv.reshape(depth, nb, (CONV_W - 1) * SSD_CONV_DIM)
    fconv_in = st_fconv.reshape(depth, nb, (FFN_CONV_W - 1) * D_FF)
    outs = [[] for _ in range(4)]
    full = None
    for l in range(depth):
        proj = _inproj(x2d, w, l, tm=nb)
        ya, yb, yc, lconv, lh, rs_full, sconv, ss_full = _mix_decode(
            proj, (lconv_in[l], st_lh[l], st_rs, sconv_in[l], st_ss), l, w, tabs, full)
        full = (rs_full, ss_full)
        x2d, fconv = _ffn(x2d, ya, yb, yc, w, l, fg, bsz=nb, seqlen=1, tm=nb, decode=True,
                          final_norm=(l == depth - 1), st=fconv_in[l])
        for lst, val in zip(outs, (lconv, lh, sconv, fconv)):
            lst.append(val)
    lconv, lh, sconv, fconv = [jnp.stack(o, axis=0) for o in outs]
    new_states = [lconv.reshape(depth, nb, CONV_W - 1, D_LRU), lh, full[0],
                  sconv.reshape(depth, nb, CONV_W - 1, SSD_CONV_DIM), full[1],
                  fconv.reshape(depth, nb, FFN_CONV_W - 1, D_FF)]
    return x2d.reshape(nb, 1, D_MODEL), new_states


def kernel(x_prompt, x_sample, state_lru_conv, state_lru_h, state_ret, state_ssd_conv, state_ssd, state_ffn_conv,
           norm_mix_g, w_in, lru_conv_w, lru_conv_b, lru_wr, lru_br, lru_wi, lru_bi, lru_lambda,
           ret_norm_g, ssd_conv_w, ssd_conv_b, ssd_dt_bias, ssd_a_log, ssd_d, ssd_norm_g,
           w_out, norm_ffn_g, ffn_w_up, ffn_conv_w, ffn_conv_b, ffn_w_down, norm_final_g):
    p = {"norm_mix_g": norm_mix_g, "w_in": w_in, "lru_conv_w": lru_conv_w, "lru_conv_b": lru_conv_b,
         "lru_wr": lru_wr, "lru_br": lru_br, "lru_wi": lru_wi, "lru_bi": lru_bi, "lru_lambda": lru_lambda,
         "ret_norm_g": ret_norm_g, "ssd_conv_w": ssd_conv_w, "ssd_conv_b": ssd_conv_b,
         "ssd_dt_bias": ssd_dt_bias, "ssd_a_log": ssd_a_log, "ssd_d": ssd_d, "ssd_norm_g": ssd_norm_g,
         "w_out": w_out, "norm_ffn_g": norm_ffn_g, "ffn_w_up": ffn_w_up, "ffn_conv_w": ffn_conv_w,
         "ffn_conv_b": ffn_conv_b, "ffn_w_down": ffn_w_down}
    weights = _prep_weights(p)
    past_len = 16384
    tabs = _tables(x_prompt.shape[1], past_len)
    fg = norm_final_g[None, :]

    y_p, ns_p = _prompt_stack(x_prompt, weights, tabs, fg)
    y_s, ns_s = _sample_stack(
        x_sample, (state_lru_conv, state_lru_h, state_ret, state_ssd_conv, state_ssd, state_ffn_conv),
        weights, tabs, fg)
    lru_conv_p, lru_h_p, ret_p, ssd_conv_p, ssd_p, ffn_conv_p = ns_p
    lru_conv_s, lru_h_s, ret_s, ssd_conv_s, ssd_s, ffn_conv_s = ns_s
    return (y_p, y_s, lru_conv_p, lru_conv_s, lru_h_p, lru_h_s, ret_p, ret_s,
            ssd_conv_p, ssd_conv_s, ssd_p, ssd_s, ffn_conv_p, ffn_conv_s)
```

```python
import functools

import jax
import jax.numpy as jnp
import numpy as np
from jax import lax
from jax.experimental import pallas as pl
from jax.experimental.pallas import tpu as pltpu

F32 = jnp.float32
BF16 = jnp.bfloat16

D_MODEL = 1024
DEPTH = 4
D_LRU = 512
LRU_HEADS = 8
LRU_HD = D_LRU // LRU_HEADS
LRU_C = 8.0
CONV_W = 4
RET_HEADS = 4
RET_DK = 64
RET_DV = 128
D_RET = RET_HEADS * RET_DV
D_QK = RET_HEADS * RET_DK
ROPE_BASE = 10000.0
SSD_HD = 64
D_SSD = 512
SSD_HEADS = 8
SSD_GROUPS = 2
SSD_DSTATE = 128
SSD_BC = SSD_GROUPS * SSD_DSTATE
SSD_CONV_DIM = D_SSD + 2 * SSD_BC
D_MIX = D_LRU + D_RET + D_SSD
D_FF = 2816
FFN_CONV_W = 3
EPS = 1e-6
CHUNK = 128

COL_LRU_X = 0
COL_LRU_G = 512
COL_Q = 1024
COL_K = 1280
COL_V = 1536
COL_RG = 2048
COL_Z = 2560
COL_XBC = 3072
COL_DT = 4096
D_IN_PAD = 4224

LANES = 128
SUBLANES = 8
VMEM_LIMIT_BYTES = 56 * 1024 * 1024


def _cparams(*sem):
    return pltpu.CompilerParams(dimension_semantics=sem, vmem_limit_bytes=VMEM_LIMIT_BYTES)


def _dot(a, b):
    return jnp.dot(a, b, preferred_element_type=F32)


def _dot_nt(a, b):
    return lax.dot_general(a, b, (((1,), (1,)), ((), ())), preferred_element_type=F32)


def _dot_tn(a, b):
    return lax.dot_general(a, b, (((0,), (0,)), ((), ())), preferred_element_type=F32)


def _dot_exact(a, b):
    return jnp.dot(a, b, preferred_element_type=F32, precision=lax.Precision.HIGHEST)


def _dot_nt_exact(a, b):
    return lax.dot_general(a, b, (((1,), (1,)), ((), ())), preferred_element_type=F32,
                           precision=lax.Precision.HIGHEST)


def _rms_rows(x, g):
    return x * lax.rsqrt(jnp.mean(x * x, axis=-1, keepdims=True) + EPS) * g


def _sigmoid(x):
    return 1.0 / (1.0 + jnp.exp(-x))


def _silu(x):
    return x * _sigmoid(x)


def _gelu(x):
    return jax.nn.gelu(x, approximate=True)


def _softplus(x):
    return jnp.maximum(x, 0.0) + jnp.log1p(jnp.exp(-jnp.abs(x)))


IN_COL_CHUNK = 512


def _lspec(arr, layer):
    _, r, c = arr.shape
    return pl.BlockSpec((None, r, c), lambda *_: (layer, 0, 0))


def _inproj_kernel(x_ref, g_ref, wm_ref, wqk_ref, wdt_ref, o_ref):
    hn = _rms_rows(x_ref[...], g_ref[...]).astype(BF16)
    for c0 in range(0, COL_DT, IN_COL_CHUNK):
        if c0 == COL_Q:
            o_ref[:, c0:c0 + IN_COL_CHUNK] = _dot(hn, wqk_ref[...])
        else:
            o_ref[:, c0:c0 + IN_COL_CHUNK] = _dot(hn, wm_ref[:, c0:c0 + IN_COL_CHUNK])
    o_ref[:, COL_DT:] = _dot(hn, wdt_ref[...])


def _inproj(x2d, w, layer, tm):
    t = x2d.shape[0]
    return pl.pallas_call(
        _inproj_kernel,
        grid=(t // tm,),
        in_specs=[
            pl.BlockSpec((tm, D_MODEL), lambda i: (i, 0)),
            _lspec(w["norm_mix_g"], layer),
            _lspec(w["w_in"], layer),
            _lspec(w["w_qk"], layer),
            _lspec(w["w_dt"], layer),
        ],
        out_specs=pl.BlockSpec((tm, D_IN_PAD), lambda i: (i, 0)),
        out_shape=jax.ShapeDtypeStruct((t, D_IN_PAD), F32),
        compiler_params=_cparams("arbitrary"),
        name="inproj",
    )(x2d, w["norm_mix_g"], w["w_in"], w["w_qk"], w["w_dt"])


def _lru_gates(xc, wr_ref, br_ref, wi_ref, bi_ref, lam_ref):
    xcb = xc.astype(BF16)
    r = _sigmoid(_dot(xcb, wr_ref[...]) + br_ref[...])
    ig = _sigmoid(_dot(xcb, wi_ref[...]) + bi_ref[...])
    log_a = (-LRU_C * r) * _softplus(-lam_ref[...])
    a = jnp.exp(log_a)
    b = jnp.sqrt(-jnp.tanh(log_a) * (a * a + 1.0)) * (ig * xc)
    return a, b


def _scan_rows(a, b, h_in):
    n, cols = a.shape
    groups = n // SUBLANES
    a = a.reshape(groups, SUBLANES, cols)
    b = b.reshape(groups, SUBLANES, cols)
    row = lax.broadcasted_iota(jnp.int32, a.shape, 1)
    d = 1
    while d < SUBLANES:
        keep = row >= d
        a_sh = jnp.where(keep, pltpu.roll(a, d, axis=1), 1.0)
        b_sh = jnp.where(keep, pltpu.roll(b, d, axis=1), 0.0)
        b = a * b_sh + b
        a = a * a_sh
        d *= 2
    hs = []
    h_prev = h_in
    for g in range(groups):
        hg = b[g] + a[g] * h_prev
        hs.append(hg)
        h_prev = hg[SUBLANES - 1:SUBLANES, :]
    return jnp.concatenate(hs, axis=0)


def _causal_conv(x, halo, taps, bias):
    n, cols = x.shape
    groups = n // SUBLANES
    xe = jnp.concatenate([halo, x], axis=0).reshape(groups + 1, SUBLANES, cols)
    row = lax.broadcasted_iota(jnp.int32, (groups, SUBLANES, cols), 1)
    y = bias + taps[-1] * x
    for s in range(1, len(taps)):
        r = pltpu.roll(xe, s, axis=1)
        shifted = jnp.where(row >= s, r[1:], r[:-1]).reshape(n, cols)
        y = y + taps[-1 - s] * shifted
    return y


def _rotary_split(x, cos, sin):
    xa, xb = x[:, :LANES], x[:, LANES:]
    return jnp.concatenate([xa * cos - xb * sin, xa * sin + xb * cos], axis=1)


def _head_of_qk_lane(shape, axis):
    lane = lax.broadcasted_iota(jnp.int32, shape, axis)
    return (lane % LANES) // (RET_DK // 2)


def _lru_prompt_kernel(x_ref, gate_ref, cw_ref, cb_ref, wr_ref, br_ref, wi_ref, bi_ref, lam_ref,
                       y_ref, conv_ref, h_ref, xbuf, hcar):
    t = pl.program_id(1)
    tl = x_ref.shape[0]

    @pl.when(t == 0)
    def _():
        xbuf[...] = jnp.zeros_like(xbuf)
        hcar[...] = jnp.zeros_like(hcar)

    x = x_ref[...]
    xc = _causal_conv(x, xbuf[...], [cw_ref[k:k + 1, :] for k in range(CONV_W)], cb_ref[...])
    tail = x[tl - SUBLANES:, :]
    xbuf[...] = tail
    conv_ref[0] = tail[SUBLANES - (CONV_W - 1):, :]

    a, b = _lru_gates(xc, wr_ref, br_ref, wi_ref, bi_ref, lam_ref)
    h = _scan_rows(a, b, hcar[...])
    hlast = h[tl - 1:tl, :]
    hcar[...] = hlast
    h_ref[0] = hlast
    y_ref[...] = (h * _gelu(gate_ref[...])).astype(y_ref.dtype)


LRU_PARAMS = ("lru_conv_w", "lru_conv_b", "lru_wr", "lru_br", "lru_wi", "lru_bi", "lru_lambda")


def _lru_prompt(proj, bsz, seqlen, w, layer, tl):
    nt = seqlen // tl
    row = lambda b, t: b * nt + t
    return pl.pallas_call(
        _lru_prompt_kernel,
        grid=(bsz, nt),
        in_specs=[
            pl.BlockSpec((tl, D_LRU), lambda b, t: (row(b, t), COL_LRU_X // D_LRU)),
            pl.BlockSpec((tl, D_LRU), lambda b, t: (row(b, t), COL_LRU_G // D_LRU)),
        ] + [_lspec(w[n], layer) for n in LRU_PARAMS],
        out_specs=[
            pl.BlockSpec((tl, D_LRU), lambda b, t: (row(b, t), 0)),
            pl.BlockSpec((1, CONV_W - 1, D_LRU), lambda b, t: (b, 0, 0)),
            pl.BlockSpec((1, 1, D_LRU), lambda b, t: (b, 0, 0)),
        ],
        out_shape=[
            jax.ShapeDtypeStruct((bsz * seqlen, D_LRU), BF16),
            jax.ShapeDtypeStruct((bsz, CONV_W - 1, D_LRU), F32),
            jax.ShapeDtypeStruct((bsz, 1, D_LRU), F32),
        ],
        scratch_shapes=[
            pltpu.VMEM((SUBLANES, D_LRU), F32),
            pltpu.VMEM((1, D_LRU), F32),
        ],
        compiler_params=_cparams("arbitrary", "arbitrary"),
        name="lru_prompt",
    )(proj, proj, *[w[n] for n in LRU_PARAMS])


def _ret_prompt_kernel(q_ref, k_ref, v_ref, g_ref, cos_ref, sin_ref, decay_ref, qdec_ref, kdec_ref,
                       cdec_ref, ng_ref, y_ref, s_ref, state):
    c = pl.program_id(1)

    @pl.when(c == 0)
    def _():
        state[...] = jnp.zeros_like(state)

    head = _head_of_qk_lane((CHUNK, D_QK), 1)
    for ci in range(q_ref.shape[0] // CHUNK):
        rows = slice(ci * CHUNK, (ci + 1) * CHUNK)
        cos = cos_ref[rows, :]
        sin = sin_ref[rows, :]
        q = _rotary_split(q_ref[rows, :], cos, sin)
        k = _rotary_split(k_ref[rows, :], cos, sin) * (RET_DK ** -0.5)
        kb = k.astype(BF16)
        qd = q * qdec_ref[...]
        kd = k * kdec_ref[...]
        for h in range(RET_HEADS):
            sel = head == h
            cols = slice(h * RET_DV, (h + 1) * RET_DV)
            vh = v_ref[rows, cols].astype(BF16)
            scores = _dot_nt(jnp.where(sel, q, 0.0).astype(BF16), kb) * decay_ref[h]
            o = _dot(scores.astype(BF16), vh)
            o = o + _dot(jnp.where(sel, qd, 0.0).astype(BF16), state[h].astype(BF16))
            state[h] = state[h] * cdec_ref[h] + _dot_tn(jnp.where(sel, kd, 0.0).astype(BF16), vh)
            o = _rms_rows(o, ng_ref[h:h + 1, :])
            y_ref[rows, cols] = (_silu(g_ref[rows, cols]) * o).astype(y_ref.dtype)
    half = RET_DK // 2
    for h in range(RET_HEADS):
        s_ref[0, h, 0:half, :] = state[h, h * half:(h + 1) * half, :]
        s_ref[0, h, half:RET_DK, :] = state[h, LANES + h * half:LANES + (h + 1) * half, :]


def _ret_prompt(proj, bsz, seqlen, w, layer, tabs, tr):
    nc = seqlen // tr
    row = lambda b, c: b * nc + c
    const2 = lambda shape: pl.BlockSpec(shape, lambda b, c: (0, 0))
    const3 = lambda shape: pl.BlockSpec(shape, lambda b, c: (0, 0, 0))
    return pl.pallas_call(
        _ret_prompt_kernel,
        grid=(bsz, nc),
        in_specs=[
            pl.BlockSpec((tr, D_QK), lambda b, c: (row(b, c), COL_Q // D_QK)),
            pl.BlockSpec((tr, D_QK), lambda b, c: (row(b, c), COL_K // D_QK)),
            pl.BlockSpec((tr, D_RET), lambda b, c: (row(b, c), COL_V // D_RET)),
            pl.BlockSpec((tr, D_RET), lambda b, c: (row(b, c), COL_RG // D_RET)),
            pl.BlockSpec((tr, LANES), lambda b, c: (c, 0)),
            pl.BlockSpec((tr, LANES), lambda b, c: (c, 0)),
            const3((RET_HEADS, CHUNK, CHUNK)),
            const2((CHUNK, D_QK)),
            const2((CHUNK, D_QK)),
            const3((RET_HEADS, 1, 1)),
            _lspec(w["ret_norm_g"], layer),
        ],
        out_specs=[
            pl.BlockSpec((tr, D_RET), lambda b, c: (row(b, c), 0)),
            pl.BlockSpec((1, RET_HEADS, RET_DK, RET_DV), lambda b, c: (b, 0, 0, 0)),
        ],
        out_shape=[
            jax.ShapeDtypeStruct((bsz * seqlen, D_RET), BF16),
            jax.ShapeDtypeStruct((bsz, RET_HEADS, RET_DK, RET_DV), F32),
        ],
        scratch_shapes=[pltpu.VMEM((RET_HEADS, D_QK, RET_DV), F32)],
        compiler_params=_cparams("arbitrary", "arbitrary"),
        name="ret_prompt",
    )(proj, proj, proj, proj, tabs["cos_p"], tabs["sin_p"], tabs["decay"], tabs["qdec"],
      tabs["kdec"], tabs["cdec"], w["ret_norm_g"])


HEADS_PER_GROUP = SSD_HEADS // SSD_GROUPS
PAIRS = SSD_HEADS // 2


def _ssd_prompt_kernel_full(z_ref, xbc_ref, dt_ref, cw_ref, cb_ref, dtb_ref, alog_ref, dfull_ref, ng_ref,
                            tril_ref, y_ref, conv_ref, s_ref, xbuf, state, ybuf):
    c = pl.program_id(1)
    tr = xbc_ref.shape[0]

    @pl.when(c == 0)
    def _():
        xbuf[...] = jnp.zeros_like(xbuf)
        state[...] = jnp.zeros_like(state)

    x = xbc_ref[...]
    xc = _causal_conv(x, xbuf[...], [cw_ref[k:k + 1, :] for k in range(CONV_W)], cb_ref[...])
    tail = x[tr - SUBLANES:, :]
    xbuf[...] = tail
    conv_ref[0] = tail[SUBLANES - (CONV_W - 1):, :]
    xc = _silu(xc)
    xs_all = xc[:, :D_SSD]
    bm_all = xc[:, D_SSD:D_SSD + SSD_BC].astype(BF16)
    cm_all = xc[:, D_SSD + SSD_BC:].astype(BF16)

    dt_all = _softplus(dt_ref[...] + dtb_ref[...])
    da_all = dt_all * (-jnp.exp(alog_ref[...]))

    causal = (lax.broadcasted_iota(jnp.int32, (CHUNK, CHUNK), 0)
              >= lax.broadcasted_iota(jnp.int32, (CHUNK, CHUNK), 1))
    lane = lax.broadcasted_iota(jnp.int32, (CHUNK, LANES), 1)
    lo = lane < SSD_HD
    rowi = lax.broadcasted_iota(jnp.int32, (2 * SSD_HD, SSD_DSTATE), 0)
    row_lo = rowi < SSD_HD

    for ci in range(tr // CHUNK):
        rows = slice(ci * CHUNK, (ci + 1) * CHUNK)
        xs, bm, cm, dt = xs_all[rows], bm_all[rows], cm_all[rows], dt_all[rows]
        cum = _dot_exact(tril_ref[...], da_all[rows])
        cum_t = cum.T
        dt_t = dt.T
        ecum = jnp.exp(cum)
        wlast = jnp.exp(cum[CHUNK - 1:CHUNK, :] - cum) * dt
        elast = jnp.exp(cum_t[:, CHUNK - 1:CHUNK])
        cbw = [_dot_nt(cm[:, g * SSD_DSTATE:(g + 1) * SSD_DSTATE], bm[:, g * SSD_DSTATE:(g + 1) * SSD_DSTATE])
               for g in range(SSD_GROUPS)]
        for p in range(PAIRS):
            g = (2 * p) // HEADS_PER_GROUP
            xp = xs[:, p * LANES:(p + 1) * LANES]
            bg = bm[:, g * SSD_DSTATE:(g + 1) * SSD_DSTATE]
            cg = cm[:, g * SSD_DSTATE:(g + 1) * SSD_DSTATE]
            y = None
            for j, sel in ((0, lo), (1, jnp.logical_not(lo))):
                h = 2 * p + j
                seg = cum[:, h:h + 1] - cum_t[h:h + 1, :]
                lmask = jnp.exp(jnp.where(causal, seg, -jnp.inf))
                wgt = cbw[g] * lmask * dt_t[h:h + 1, :]
                yj = _dot(wgt.astype(BF16), jnp.where(sel, xp, 0.0).astype(BF16))
                y = yj if y is None else y + yj
            h0, h1 = 2 * p, 2 * p + 1
            sp = state[p]
            y = y + _dot_nt(cg, sp.astype(BF16)) * jnp.where(lo, ecum[:, h0:h0 + 1], ecum[:, h1:h1 + 1])
            wl = jnp.where(lo, wlast[:, h0:h0 + 1], wlast[:, h1:h1 + 1])
            state[p] = sp * jnp.where(row_lo, elast[h0:h0 + 1, :], elast[h1:h1 + 1, :]) + _dot_tn(
                (xp * wl).astype(BF16), bg)
            ybuf[rows, p * LANES:(p + 1) * LANES] = y + dfull_ref[:, p * LANES:(p + 1) * LANES] * xp
    for p in range(PAIRS):
        s_ref[0, 2 * p] = state[p, :SSD_HD, :]
        s_ref[0, 2 * p + 1] = state[p, SSD_HD:, :]
    y_ref[...] = _ssd_gate_norm(ybuf[...], z_ref[...], ng_ref[...]).astype(y_ref.dtype)


def _ssd_gate_norm(y, z, ng):
    gated = y * _silu(z)
    gw = D_SSD // SSD_GROUPS
    outs = []
    for g in range(SSD_GROUPS):
        outs.append(_rms_rows(gated[:, g * gw:(g + 1) * gw], ng[:, g * gw:(g + 1) * gw]))
    return jnp.concatenate(outs, axis=1)


SSD_PROMPT_PARAMS = ("ssd_conv_w", "ssd_conv_b", "ssd_dt_bias", "ssd_a_log", "ssd_d_full", "ssd_norm_g")


def _ssd_prompt(proj, bsz, seqlen, w, layer, tabs, tr):
    nc = seqlen // tr
    row = lambda b, c: b * nc + c
    const2 = lambda shape: pl.BlockSpec(shape, lambda b, c: (0, 0))
    return pl.pallas_call(
        _ssd_prompt_kernel_full,
        grid=(bsz, nc),
        in_specs=[
            pl.BlockSpec((tr, D_SSD), lambda b, c: (row(b, c), COL_Z // D_SSD)),
            pl.BlockSpec((tr, SSD_CONV_DIM), lambda b, c: (row(b, c), COL_XBC // SSD_CONV_DIM)),
            pl.BlockSpec((tr, LANES), lambda b, c: (row(b, c), COL_DT // LANES)),
        ] + [_lspec(w[n], layer) for n in SSD_PROMPT_PARAMS] + [const2((CHUNK, CHUNK))],
        out_specs=[
            pl.BlockSpec((tr, D_SSD), lambda b, c: (row(b, c), 0)),
            pl.BlockSpec((1, CONV_W - 1, SSD_CONV_DIM), lambda b, c: (b, 0, 0)),
            pl.BlockSpec((1, SSD_HEADS, SSD_HD, SSD_DSTATE), lambda b, c: (b, 0, 0, 0)),
        ],
        out_shape=[
            jax.ShapeDtypeStruct((bsz * seqlen, D_SSD), BF16),
            jax.ShapeDtypeStruct((bsz, CONV_W - 1, SSD_CONV_DIM), F32),
            jax.ShapeDtypeStruct((bsz, SSD_HEADS, SSD_HD, SSD_DSTATE), F32),
        ],
        scratch_shapes=[
            pltpu.VMEM((SUBLANES, SSD_CONV_DIM), F32),
            pltpu.VMEM((PAIRS, 2 * SSD_HD, SSD_DSTATE), F32),
            pltpu.VMEM((tr, D_SSD), F32),
        ],
        compiler_params=_cparams("arbitrary", "arbitrary"),
        name="ssd_prompt",
    )(proj, proj, proj, *[w[n] for n in SSD_PROMPT_PARAMS], tabs["tril"])


FF_TILES = ((0, 1536), (1536, 1280))
FFN_PARAMS = ("w_out", "norm_ffn_g", "ffn_w_up", "ffn_conv_w", "ffn_conv_b", "ffn_w_down")


def _ffn_kernel(x_ref, ya_ref, yb_ref, yc_ref, wo_ref, g_ref, wup_ref, cw_ref, cb_ref, wdn_ref, fg_ref,
                *rest, decode, final_norm):
    if decode:
        st_ref, o_ref, conv_ref = rest
    else:
        o_ref, conv_ref, halo = rest
    tm = x_ref.shape[0]
    x1 = x_ref[...]
    x1 = x1 + _dot(ya_ref[...], wo_ref[0:D_LRU, :])
    x1 = x1 + _dot(yb_ref[...], wo_ref[D_LRU:D_LRU + D_RET, :])
    x1 = x1 + _dot(yc_ref[...], wo_ref[D_LRU + D_RET:, :])
    hn = _rms_rows(x1, g_ref[...]).astype(BF16)

    if not decode:
        @pl.when(pl.program_id(1) == 0)
        def _():
            halo[...] = jnp.zeros_like(halo)

    acc = None
    for f0, fw in FF_TILES:
        gate = _dot(hn, wup_ref[:, f0:f0 + fw])
        up = _dot(hn, wup_ref[:, D_FF + f0:D_FF + f0 + fw])
        w0 = cw_ref[0:1, f0:f0 + fw]
        w1 = cw_ref[1:2, f0:f0 + fw]
        w2 = cw_ref[2:3, f0:f0 + fw]
        cb = cb_ref[:, f0:f0 + fw]
        if decode:
            s0 = st_ref[:, f0:f0 + fw]
            s1 = st_ref[:, D_FF + f0:D_FF + f0 + fw]
            conv = cb + w0 * s0 + w1 * s1 + w2 * gate
            conv_ref[:, f0:f0 + fw] = s1
            conv_ref[:, D_FF + f0:D_FF + f0 + fw] = gate
        else:
            conv = _causal_conv(gate, halo[:, f0:f0 + fw], [w0, w1, w2], cb)
            tail = gate[tm - SUBLANES:, :]
            halo[:, f0:f0 + fw] = tail
            conv_ref[0, :, f0:f0 + fw] = tail[SUBLANES - (FFN_CONV_W - 1):, :]
        act = (_gelu(conv) * up).astype(BF16)
        d = _dot(act, wdn_ref[f0:f0 + fw, :])
        acc = d if acc is None else acc + d
    acc = x1 + acc
    if final_norm:
        acc = _rms_rows(acc, fg_ref[...])
    o_ref[...] = acc


def _ffn(x2d, ya, yb, yc, w, layer, fg, *, bsz, seqlen, tm, decode, final_norm, st=None):
    t = x2d.shape[0]
    kern = functools.partial(_ffn_kernel, decode=decode, final_norm=final_norm)
    if decode:
        grid = (t // tm,)
        rowmap = lambda i: (i, 0)
        sem = ("arbitrary",)
    else:
        nt = seqlen // tm
        grid = (bsz, nt)
        rowmap = lambda b, i: (b * nt + i, 0)
        sem = ("arbitrary", "arbitrary")
    in_specs = [
        pl.BlockSpec((tm, D_MODEL), rowmap),
        pl.BlockSpec((tm, D_LRU), rowmap),
        pl.BlockSpec((tm, D_RET), rowmap),
        pl.BlockSpec((tm, D_SSD), rowmap),
    ] + [_lspec(w[n], layer) for n in FFN_PARAMS] + [pl.BlockSpec((1, D_MODEL), lambda *_: (0, 0))]
    args = [x2d, ya, yb, yc] + [w[n] for n in FFN_PARAMS] + [fg]
    if decode:
        in_specs.append(pl.BlockSpec((tm, 2 * D_FF), rowmap))
        args.append(st)
        out_specs = [pl.BlockSpec((tm, D_MODEL), rowmap), pl.BlockSpec((tm, 2 * D_FF), rowmap)]
        out_shape = [jax.ShapeDtypeStruct((t, D_MODEL), F32), jax.ShapeDtypeStruct((t, 2 * D_FF), F32)]
        scratch = []
    else:
        out_specs = [pl.BlockSpec((tm, D_MODEL), rowmap),
                     pl.BlockSpec((1, FFN_CONV_W - 1, D_FF), lambda b, i: (b, 0, 0))]
        out_shape = [jax.ShapeDtypeStruct((t, D_MODEL), F32),
                     jax.ShapeDtypeStruct((bsz, FFN_CONV_W - 1, D_FF), F32)]
        scratch = [pltpu.VMEM((SUBLANES, D_FF), F32)]
    return pl.pallas_call(
        kern, grid=grid, in_specs=in_specs, out_specs=out_specs, out_shape=out_shape,
        scratch_shapes=scratch, compiler_params=_cparams(*sem),
        name="ffn_decode" if decode else "ffn_prompt",
    )(*args)


DEC_TILE = 8


MIX_DECODE_INPUTS = 35


def _mix_decode_kernel(*refs, n_alias):
    _mix_decode_body(*refs[:MIX_DECODE_INPUTS], *refs[MIX_DECODE_INPUTS + n_alias:])


def _mix_decode_body(lx_ref, lg_ref, q_ref, k_ref, v_ref, rg_ref, z_ref, xbc_ref, dt_ref,
                       lconv_ref, lh_ref, rs_ref, sconv_ref, ss_ref,
                       lcw_ref, lcb_ref, wr_ref, br_ref, wi_ref, bi_ref, lam_ref,
                       cos_ref, sin_ref, cdec_ref, rng_ref,
                       scw_ref, scb_ref, dtb_ref, alog_ref, dfull_ref, sng_ref,
                       eye_ref, hexp_ref, hexp_tile_ref, rsel_ref,
                       ya_ref, yb_ref, yc_ref, lconv_o, lh_o, rs_o, sconv_o, ss_o,
                       obuf, ybuf):
    x = lx_ref[...]
    st = lconv_ref[...]
    xc = lcb_ref[...] + lcw_ref[0:1, :] * st[:, 0:D_LRU] + lcw_ref[1:2, :] * st[:, D_LRU:2 * D_LRU]
    xc = xc + lcw_ref[2:3, :] * st[:, 2 * D_LRU:] + lcw_ref[3:4, :] * x
    lconv_o[:, 0:2 * D_LRU] = st[:, D_LRU:]
    lconv_o[:, 2 * D_LRU:] = x
    a, b = _lru_gates(xc, wr_ref, br_ref, wi_ref, bi_ref, lam_ref)
    h = b + a * lh_ref[...]
    lh_o[...] = h
    ya_ref[...] = (h * _gelu(lg_ref[...])).astype(ya_ref.dtype)

    cos = cos_ref[...]
    sin = sin_ref[...]
    q = _rotary_split(q_ref[...], cos, sin)
    k = _rotary_split(k_ref[...], cos, sin) * (RET_DK ** -0.5)
    v = v_ref[...]

    xin = xbc_ref[...]
    sst = sconv_ref[...]
    n = SSD_CONV_DIM
    xc = scb_ref[...] + scw_ref[0:1, :] * sst[:, 0:n] + scw_ref[1:2, :] * sst[:, n:2 * n]
    xc = xc + scw_ref[2:3, :] * sst[:, 2 * n:] + scw_ref[3:4, :] * xin
    sconv_o[:, 0:2 * n] = sst[:, n:]
    sconv_o[:, 2 * n:] = xin
    xc = _silu(xc)
    xs = xc[:, :D_SSD]
    bm = xc[:, D_SSD:D_SSD + SSD_BC]
    cm_b = xc[:, D_SSD + SSD_BC:].astype(BF16)
    dt = _softplus(dt_ref[...] + dtb_ref[...])
    dt_full = _dot_exact(dt, hexp_ref[...])
    eda_tiles = _dot_exact(jnp.exp(dt * (-jnp.exp(alog_ref[...]))), hexp_tile_ref[...])

    cols_src = jnp.concatenate([xs * dt_full, k], axis=1)
    cols_src = jnp.concatenate(
        [cols_src, jnp.zeros((LANES - DEC_TILE, cols_src.shape[1]), F32)], axis=0).astype(BF16)
    eye = eye_ref[...]
    cols_t = jnp.concatenate(
        [_dot_nt(eye, cols_src[:, j * LANES:(j + 1) * LANES]) for j in range(cols_src.shape[1] // LANES)],
        axis=0).astype(BF16)
    k_row0 = D_SSD

    head_q = _head_of_qk_lane((DEC_TILE, D_QK), 1)
    q_heads = jnp.concatenate([jnp.where(head_q == hh, q, 0.0) for hh in range(RET_HEADS)],
                              axis=0).astype(BF16)
    half = RET_DK // 2
    for r in range(DEC_TILE):
        colb = _dot(cols_t, rsel_ref[r])
        pieces = []
        for hh in range(SSD_HEADS):
            g = hh // HEADS_PER_GROUP
            e = jnp.broadcast_to(eda_tiles[r:r + 1, hh * LANES:(hh + 1) * LANES], (SSD_HD, SSD_DSTATE))
            brow = bm[r:r + 1, g * SSD_DSTATE:(g + 1) * SSD_DSTATE]
            s_new = ss_ref[0, r, hh] * e + colb[hh * SSD_HD:(hh + 1) * SSD_HD, :] * brow
            ss_o[0, r, hh] = s_new
            pieces.append(s_new.astype(BF16))
        gw = D_SSD // SSD_GROUPS
        for g in range(SSD_GROUPS):
            sg = jnp.concatenate(pieces[g * HEADS_PER_GROUP:(g + 1) * HEADS_PER_GROUP], axis=0)
            yg = _dot_nt(cm_b[:, g * SSD_DSTATE:(g + 1) * SSD_DSTATE], sg)
            ybuf[r:r + 1, g * gw:(g + 1) * gw] = yg[r:r + 1, :]
        pieces = []
        for part in range(2):
            for hh in range(RET_HEADS):
                r0 = k_row0 + part * LANES + hh * half
                vrow = v[r:r + 1, hh * RET_DV:(hh + 1) * RET_DV]
                s_old = rs_ref[0, r, hh, part * half:(part + 1) * half, :]
                s_new = s_old * cdec_ref[hh] + colb[r0:r0 + half, :] * vrow
                rs_o[0, r, hh, part * half:(part + 1) * half, :] = s_new
                pieces.append(s_new.astype(BF16))
        s_perm = jnp.concatenate(pieces, axis=0)
        res = _dot(q_heads, s_perm)
        for hh in range(RET_HEADS):
            obuf[r:r + 1, hh * RET_DV:(hh + 1) * RET_DV] = res[hh * DEC_TILE + r:hh * DEC_TILE + r + 1, :]

    o = obuf[...]
    rg = rg_ref[...]
    for hh in range(RET_HEADS):
        oh = _rms_rows(o[:, hh * RET_DV:(hh + 1) * RET_DV], rng_ref[hh:hh + 1, :])
        yb_ref[:, hh * RET_DV:(hh + 1) * RET_DV] = (
            _silu(rg[:, hh * RET_DV:(hh + 1) * RET_DV]) * oh).astype(yb_ref.dtype)
    y = ybuf[...] + dfull_ref[...] * xs
    yc_ref[...] = _ssd_gate_norm(y, z_ref[...], sng_ref[...]).astype(yc_ref.dtype)


SSD_DECODE_PARAMS = SSD_PROMPT_PARAMS


def _mix_decode(proj, states, layer, w, tabs, prev_full):
    nb = proj.shape[0]
    depth = states[2].shape[0]
    grid = (nb // DEC_TILE,)
    col = lambda width, c0: pl.BlockSpec((DEC_TILE, width), lambda i: (i, c0 // width))
    const2 = lambda shape: pl.BlockSpec(shape, lambda i: (0, 0))
    const3 = lambda shape: pl.BlockSpec(shape, lambda i: (0, 0, 0))
    lconv, lh, rs, sconv, ss = states
    ret_blk = pl.BlockSpec((1, DEC_TILE, RET_HEADS, RET_DK, RET_DV), lambda i: (layer, i, 0, 0, 0))
    ssd_blk = pl.BlockSpec((1, DEC_TILE, SSD_HEADS, SSD_HD, SSD_DSTATE), lambda i: (layer, i, 0, 0, 0))
    in_specs = [
        col(D_LRU, COL_LRU_X), col(D_LRU, COL_LRU_G), col(D_QK, COL_Q), col(D_QK, COL_K),
        col(D_RET, COL_V), col(D_RET, COL_RG), col(D_SSD, COL_Z), col(SSD_CONV_DIM, COL_XBC),
        col(LANES, COL_DT),
        pl.BlockSpec((DEC_TILE, (CONV_W - 1) * D_LRU), lambda i: (i, 0)),
        pl.BlockSpec((DEC_TILE, D_LRU), lambda i: (i, 0)),
        ret_blk,
        pl.BlockSpec((DEC_TILE, (CONV_W - 1) * SSD_CONV_DIM), lambda i: (i, 0)),
        ssd_blk,
    ] + [_lspec(w[n], layer) for n in LRU_PARAMS] + [
        const2((1, LANES)), const2((1, LANES)), const3((RET_HEADS, 1, 1)), _lspec(w["ret_norm_g"], layer),
    ] + [_lspec(w[n], layer) for n in SSD_DECODE_PARAMS] + [
        const2((LANES, LANES)), const2((LANES, D_SSD)), const2((LANES, SSD_HEADS * LANES)),
        const3((DEC_TILE, LANES, LANES)),
    ]
    args = [proj] * 9 + [lconv, lh, rs, sconv, ss] + [w[n] for n in LRU_PARAMS] + [
        tabs["cos_s"], tabs["sin_s"], tabs["cdec1"], w["ret_norm_g"]] + [
        w[n] for n in SSD_DECODE_PARAMS] + [tabs["eye"], tabs["hexp"], tabs["hexp_tile"], tabs["rsel"]]
    assert len(args) == MIX_DECODE_INPUTS
    aliases = {}
    if prev_full is not None:
        in_specs += [pl.BlockSpec(memory_space=pl.ANY)] * 2
        args += list(prev_full)
        aliases = {MIX_DECODE_INPUTS: 5, MIX_DECODE_INPUTS + 1: 7}
    out_specs = [
        pl.BlockSpec((DEC_TILE, D_LRU), lambda i: (i, 0)),
        pl.BlockSpec((DEC_TILE, D_RET), lambda i: (i, 0)),
        pl.BlockSpec((DEC_TILE, D_SSD), lambda i: (i, 0)),
        pl.BlockSpec((DEC_TILE, (CONV_W - 1) * D_LRU), lambda i: (i, 0)),
        pl.BlockSpec((DEC_TILE, D_LRU), lambda i: (i, 0)),
        ret_blk,
        pl.BlockSpec((DEC_TILE, (CONV_W - 1) * SSD_CONV_DIM), lambda i: (i, 0)),
        ssd_blk,
    ]
    out_shape = [
        jax.ShapeDtypeStruct((nb, D_LRU), BF16),
        jax.ShapeDtypeStruct((nb, D_RET), BF16),
        jax.ShapeDtypeStruct((nb, D_SSD), BF16),
        jax.ShapeDtypeStruct((nb, (CONV_W - 1) * D_LRU), F32),
        jax.ShapeDtypeStruct((nb, D_LRU), F32),
        jax.ShapeDtypeStruct((depth, nb, RET_HEADS, RET_DK, RET_DV), F32),
        jax.ShapeDtypeStruct((nb, (CONV_W - 1) * SSD_CONV_DIM), F32),
        jax.ShapeDtypeStruct((depth, nb, SSD_HEADS, SSD_HD, SSD_DSTATE), F32),
    ]
    return pl.pallas_call(
        functools.partial(_mix_decode_kernel, n_alias=len(aliases)),
        grid=grid, in_specs=in_specs, out_specs=out_specs, out_shape=out_shape,
        input_output_aliases=aliases,
        scratch_shapes=[pltpu.VMEM((DEC_TILE, D_RET), F32), pltpu.VMEM((DEC_TILE, D_SSD), F32)],
        compiler_params=_cparams("arbitrary"),
        name="mix_decode",
    )(*args)


def _qk_perm():
    half = RET_DK // 2
    idx = np.empty((D_QK,), np.int32)
    for part in range(2):
        for h in range(RET_HEADS):
            for j in range(half):
                idx[part * LANES + h * half + j] = h * RET_DK + part * half + j
    return idx


def _block_diag(wh):
    nh, hd, _ = wh.shape
    eye = jnp.eye(nh, dtype=wh.dtype)
    return (eye[:, None, :, None] * wh[:, :, None, :]).reshape(nh * hd, nh * hd)


def _prep_weights(p):
    perm = _qk_perm()
    w_in = p["w_in"]
    depth = w_in.shape[0]
    w_qk = jnp.concatenate([w_in[:, :, COL_Q:COL_Q + D_QK][:, :, perm],
                            w_in[:, :, COL_K:COL_K + D_QK][:, :, perm]], axis=2).astype(BF16)
    n_dt = w_in.shape[2] - COL_DT
    w_dt = jnp.concatenate([w_in[:, :, COL_DT:], jnp.zeros((depth, D_MODEL, LANES - n_dt), w_in.dtype)],
                           axis=2).astype(BF16)
    row = lambda v: v[:, None, :]
    pad_lanes = lambda v: row(jnp.concatenate([v, jnp.zeros((depth, LANES - v.shape[1]), v.dtype)], axis=1))
    return {
        "norm_mix_g": row(p["norm_mix_g"]),
        "w_in": w_in.astype(BF16),
        "w_qk": w_qk,
        "w_dt": w_dt,
        "lru_conv_w": p["lru_conv_w"],
        "lru_conv_b": row(p["lru_conv_b"]),
        "lru_wr": jax.vmap(_block_diag)(p["lru_wr"]).astype(BF16),
        "lru_br": row(p["lru_br"]),
        "lru_wi": jax.vmap(_block_diag)(p["lru_wi"]).astype(BF16),
        "lru_bi": row(p["lru_bi"]),
        "lru_lambda": row(p["lru_lambda"]),
        "ret_norm_g": p["ret_norm_g"],
        "ssd_conv_w": p["ssd_conv_w"],
        "ssd_conv_b": row(p["ssd_conv_b"]),
        "ssd_dt_bias": pad_lanes(p["ssd_dt_bias"]),
        "ssd_a_log": pad_lanes(p["ssd_a_log"]),
        "ssd_d_full": row(jnp.repeat(p["ssd_d"], SSD_HD, axis=1)),
        "ssd_norm_g": row(p["ssd_norm_g"]),
        "w_out": p["w_out"].astype(BF16),
        "norm_ffn_g": row(p["norm_ffn_g"]),
        "ffn_w_up": p["ffn_w_up"].astype(BF16),
        "ffn_conv_w": p["ffn_conv_w"],
        "ffn_conv_b": row(p["ffn_conv_b"]),
        "ffn_w_down": p["ffn_w_down"].astype(BF16),
    }


def _tables(seqlen, past_len):
    half = RET_DK // 2
    freqs = ROPE_BASE ** (-jnp.arange(half, dtype=F32) / half)

    def cs(pos):
        ang = pos.astype(F32)[:, None] * freqs[None, :]
        return jnp.tile(jnp.cos(ang), (1, RET_HEADS)), jnp.tile(jnp.sin(ang), (1, RET_HEADS))

    cos_p, sin_p = cs(jnp.arange(seqlen, dtype=jnp.int32))
    cos_s, sin_s = cs(past_len + jnp.arange(1, dtype=jnp.int32))
    log_g = jnp.log1p(-jnp.exp2(-5.0 - jnp.arange(RET_HEADS, dtype=F32)))
    idx = jnp.arange(CHUNK, dtype=F32)
    diff = idx[:, None] - idx[None, :]
    decay = jnp.exp(jnp.where(diff[None] >= 0, diff[None] * log_g[:, None, None], -jnp.inf))
    q_dec = jnp.exp((idx[None, :] + 1.0) * log_g[:, None])
    k_dec = jnp.exp((CHUNK - 1.0 - idx[None, :]) * log_g[:, None])
    lanes_of = lambda t: jnp.tile(jnp.repeat(t.T, half, axis=1), (1, 2))
    hexp = (jnp.arange(LANES)[:, None] == (jnp.arange(D_SSD)[None, :] // SSD_HD)).astype(F32)
    return {
        "cos_p": cos_p, "sin_p": sin_p, "cos_s": cos_s, "sin_s": sin_s,
        "decay": decay, "qdec": lanes_of(q_dec), "kdec": lanes_of(k_dec),
        "cdec": jnp.exp(CHUNK * log_g)[:, None, None],
        "cdec1": jnp.exp(log_g)[:, None, None],
        "tril": jnp.tril(jnp.ones((CHUNK, CHUNK), F32)),
        "eye": jnp.eye(LANES, dtype=BF16),
        "hexp": hexp,
        "hexp_tile": (jnp.arange(LANES)[:, None] == (jnp.arange(SSD_HEADS * LANES)[None, :] // LANES)).astype(F32),
        "rsel": (jnp.arange(LANES)[None, :, None] == jnp.arange(DEC_TILE)[:, None, None]).astype(BF16)
        * jnp.ones((1, 1, LANES), BF16),
    }


TM_PROMPT = 512
TL_LRU = 256
TR_MIX = 512


def _prompt_stack(x, w, tabs, fg, tm=TM_PROMPT, tl=TL_LRU, tr=TR_MIX):
    bsz, seqlen, _ = x.shape
    depth = w["w_in"].shape[0]
    x2d = x.reshape(bsz * seqlen, D_MODEL)
    outs = [[] for _ in range(6)]
    for l in range(depth):
        proj = _inproj(x2d, w, l, tm=tm)
        ya, lconv, lh = _lru_prompt(proj, bsz, seqlen, w, l, tl=tl)
        yb, rs = _ret_prompt(proj, bsz, seqlen, w, l, tabs, tr)
        yc, sconv, ss = _ssd_prompt(proj, bsz, seqlen, w, l, tabs, tr)
        x2d, fconv = _ffn(x2d, ya, yb, yc, w, l, fg, bsz=bsz, seqlen=seqlen, tm=tm, decode=False,
                          final_norm=(l == depth - 1))
        for lst, val in zip(outs, (lconv, lh[:, 0, :], rs, sconv, ss, fconv)):
            lst.append(val)
    return x2d.reshape(bsz, seqlen, D_MODEL), [jnp.stack(o, axis=0) for o in outs]


def _sample_stack(x, states, w, tabs, fg):
    nb = x.shape[0]
    depth = w["w_in"].shape[0]
    x2d = x.reshape(nb, D_MODEL)
    st_lconv, st_lh, st_rs, st_sconv, st_ss, st_fconv = states
    lconv_in = st_lconv.reshape(depth, nb, (CONV_W - 1) * D_LRU)
    sconv_in = st_sconv.reshape(depth, nb, (CONV_W - 1) * SSD_CONV_DIM)
    fconv_in = st_fconv.reshape(depth, nb, (FFN_CONV_W - 1) * D_FF)
    outs = [[] for _ in range(4)]
    full = None
    for l in range(depth):
        proj = _inproj(x2d, w, l, tm=nb)
        ya, yb, yc, lconv, lh, rs_full, sconv, ss_full = _mix_decode(
            proj, (lconv_in[l], st_lh[l], st_rs, sconv_in[l], st_ss), l, w, tabs, full)
        full = (rs_full, ss_full)
        x2d, fconv = _ffn(x2d, ya, yb, yc, w, l, fg, bsz=nb, seqlen=1, tm=nb, decode=True,
                          final_norm=(l == depth - 1), st=fconv_in[l])
        for lst, val in zip(outs, (lconv, lh, sconv, fconv)):
            lst.append(val)
    lconv, lh, sconv, fconv = [jnp.stack(o, axis=0) for o in outs]
    new_states = [lconv.reshape(depth, nb, CONV_W - 1, D_LRU), lh, full[0],
                  sconv.reshape(depth, nb, CONV_W - 1, SSD_CONV_DIM), full[1],
                  fconv.reshape(depth, nb, FFN_CONV_W - 1, D_FF)]
    return x2d.reshape(nb, 1, D_MODEL), new_states


def kernel(x_prompt, x_sample, state_lru_conv, state_lru_h, state_ret, state_ssd_conv, state_ssd, state_ffn_conv,
           norm_mix_g, w_in, lru_conv_w, lru_conv_b, lru_wr, lru_br, lru_wi, lru_bi, lru_lambda,
           ret_norm_g, ssd_conv_w, ssd_conv_b, ssd_dt_bias, ssd_a_log, ssd_d, ssd_norm_g,
           w_out, norm_ffn_g, ffn_w_up, ffn_conv_w, ffn_conv_b, ffn_w_down, norm_final_g):
    p = {"norm_mix_g": norm_mix_g, "w_in": w_in, "lru_conv_w": lru_conv_w, "lru_conv_b": lru_conv_b,
         "lru_wr": lru_wr, "lru_br": lru_br, "lru_wi": lru_wi, "lru_bi": lru_bi, "lru_lambda": lru_lambda,
         "ret_norm_g": ret_norm_g, "ssd_conv_w": ssd_conv_w, "ssd_conv_b": ssd_conv_b,
         "ssd_dt_bias": ssd_dt_bias, "ssd_a_log": ssd_a_log, "ssd_d": ssd_d, "ssd_norm_g": ssd_norm_g,
         "w_out": w_out, "norm_ffn_g": norm_ffn_g, "ffn_w_up": ffn_w_up, "ffn_conv_w": ffn_conv_w,
         "ffn_conv_b": ffn_conv_b, "ffn_w_down": ffn_w_down}
    weights = _prep_weights(p)
    past_len = 16384
    tabs = _tables(x_prompt.shape[1], past_len)
    fg = norm_final_g[None, :]

    y_p, ns_p = _prompt_stack(x_prompt, weights, tabs, fg)
    y_s, ns_s = _sample_stack(
        x_sample, (state_lru_conv, state_lru_h, state_ret, state_ssd_conv, state_ssd, state_ffn_conv),
        weights, tabs, fg)
    lru_conv_p, lru_h_p, ret_p, ssd_conv_p, ssd_p, ffn_conv_p = ns_p
    lru_conv_s, lru_h_s, ret_s, ssd_conv_s, ssd_s, ffn_conv_s = ns_s
    return (y_p, y_s, lru_conv_p, lru_conv_s, lru_h_p, lru_h_s, ret_p, ret_s,
            ssd_conv_p, ssd_conv_s, ssd_p, ssd_s, ffn_conv_p, ffn_conv_s)
```

```python
import functools

import jax
import jax.numpy as jnp
import numpy as np
from jax import lax
from jax.experimental import pallas as pl
from jax.experimental.pallas import tpu as pltpu

F32 = jnp.float32
BF16 = jnp.bfloat16

D_MODEL = 1024
DEPTH = 4
D_LRU = 512
LRU_HEADS = 8
LRU_HD = D_LRU // LRU_HEADS
LRU_C = 8.0
CONV_W = 4
RET_HEADS = 4
RET_DK = 64
RET_DV = 128
D_RET = RET_HEADS * RET_DV
D_QK = RET_HEADS * RET_DK
ROPE_BASE = 10000.0
SSD_HD = 64
D_SSD = 512
SSD_HEADS = 8
SSD_GROUPS = 2
SSD_DSTATE = 128
SSD_BC = SSD_GROUPS * SSD_DSTATE
SSD_CONV_DIM = D_SSD + 2 * SSD_BC
D_MIX = D_LRU + D_RET + D_SSD
D_FF = 2816
FFN_CONV_W = 3
EPS = 1e-6
CHUNK = 128

COL_LRU_X = 0
COL_LRU_G = 512
COL_Q = 1024
COL_K = 1280
COL_V = 1536
COL_RG = 2048
COL_Z = 2560
COL_XBC = 3072
COL_DT = 4096
D_IN_PAD = 4224
PROMPT_COL0 = COL_Q
D_IN_PROMPT = D_IN_PAD - PROMPT_COL0

LANES = 128
SUBLANES = 8
VMEM_LIMIT_BYTES = 56 * 1024 * 1024


def _cparams(*sem):
    return pltpu.CompilerParams(dimension_semantics=sem, vmem_limit_bytes=VMEM_LIMIT_BYTES)


def _dot(a, b):
    return jnp.dot(a, b, preferred_element_type=F32)


def _dot_nt(a, b):
    return lax.dot_general(a, b, (((1,), (1,)), ((), ())), preferred_element_type=F32)


def _dot_tn(a, b):
    return lax.dot_general(a, b, (((0,), (0,)), ((), ())), preferred_element_type=F32)


def _dot_exact(a, b):
    return jnp.dot(a, b, preferred_element_type=F32, precision=lax.Precision.HIGHEST)


def _dot_nt_exact(a, b):
    return lax.dot_general(a, b, (((1,), (1,)), ((), ())), preferred_element_type=F32,
                           precision=lax.Precision.HIGHEST)


def _rms_rows(x, g):
    return x * lax.rsqrt(jnp.mean(x * x, axis=-1, keepdims=True) + EPS) * g


def _sigmoid(x):
    return 1.0 / (1.0 + jnp.exp(-x))


def _silu(x):
    return x * _sigmoid(x)


def _gelu(x):
    return jax.nn.gelu(x, approximate=True)


def _softplus(x):
    return jnp.maximum(x, 0.0) + jnp.log1p(jnp.exp(-jnp.abs(x)))


IN_COL_CHUNK = 512


def _lspec(arr, layer):
    _, r, c = arr.shape
    return pl.BlockSpec((None, r, c), lambda *_: (layer, 0, 0), pipeline_mode=pl.Buffered(1))


def _inproj_kernel(x_ref, g_ref, wm_ref, wqk_ref, wdt_ref, o_ref):
    hn = _rms_rows(x_ref[...], g_ref[...]).astype(BF16)
    for c0 in range(0, COL_DT, IN_COL_CHUNK):
        if c0 == COL_Q:
            o_ref[:, c0:c0 + IN_COL_CHUNK] = _dot(hn, wqk_ref[...])
        else:
            o_ref[:, c0:c0 + IN_COL_CHUNK] = _dot(hn, wm_ref[:, c0:c0 + IN_COL_CHUNK])
    o_ref[:, COL_DT:] = _dot(hn, wdt_ref[...])


def _inproj(x2d, w, layer, tm):
    t = x2d.shape[0]
    return pl.pallas_call(
        _inproj_kernel,
        grid=(t // tm,),
        in_specs=[
            pl.BlockSpec((tm, D_MODEL), lambda i: (i, 0)),
            _lspec(w["norm_mix_g"], layer),
            _lspec(w["w_in"], layer),
            _lspec(w["w_qk"], layer),
            _lspec(w["w_dt"], layer),
        ],
        out_specs=pl.BlockSpec((tm, D_IN_PAD), lambda i: (i, 0)),
        out_shape=jax.ShapeDtypeStruct((t, D_IN_PAD), F32),
        compiler_params=_cparams("arbitrary"),
        name="inproj",
    )(x2d, w["norm_mix_g"], w["w_in"], w["w_qk"], w["w_dt"])


def _lru_gate_pre(xc, wr_ref, br_ref, wi_ref, bi_ref, lam_ref):
    xcb = xc.astype(BF16)
    return (_dot(xcb, wr_ref[...]) + br_ref[...], _dot(xcb, wi_ref[...]) + bi_ref[...],
            _softplus(-lam_ref[...]))


def _lru_gate_post(xc, r_pre, i_pre, sp):
    log_a = (-LRU_C * _sigmoid(r_pre)) * sp
    a = jnp.exp(log_a)
    b = jnp.sqrt(-jnp.tanh(log_a) * (a * a + 1.0)) * (_sigmoid(i_pre) * xc)
    return a, b


def _lru_gates(xc, wr_ref, br_ref, wi_ref, bi_ref, lam_ref):
    return _lru_gate_post(xc, *_lru_gate_pre(xc, wr_ref, br_ref, wi_ref, bi_ref, lam_ref))


def _scan_rows(a, b, h_in):
    n, cols = a.shape
    groups = n // SUBLANES
    a = a.reshape(groups, SUBLANES, cols)
    b = b.reshape(groups, SUBLANES, cols)
    row = lax.broadcasted_iota(jnp.int32, a.shape, 1)
    d = 1
    while d < SUBLANES:
        keep = row >= d
        a_sh = jnp.where(keep, pltpu.roll(a, d, axis=1), 1.0)
        b_sh = jnp.where(keep, pltpu.roll(b, d, axis=1), 0.0)
        b = a * b_sh + b
        a = a * a_sh
        d *= 2
    hs = []
    h_prev = h_in
    for g in range(groups):
        hg = b[g] + a[g] * h_prev
        hs.append(hg)
        h_prev = hg[SUBLANES - 1:SUBLANES, :]
    return jnp.concatenate(hs, axis=0)


def _causal_conv(x, halo, taps, bias):
    n, cols = x.shape
    groups = n // SUBLANES
    xe = jnp.concatenate([halo, x], axis=0).reshape(groups + 1, SUBLANES, cols)
    row = lax.broadcasted_iota(jnp.int32, (groups, SUBLANES, cols), 1)
    y = bias + taps[-1] * x
    for s in range(1, len(taps)):
        r = pltpu.roll(xe, s, axis=1)
        shifted = jnp.where(row >= s, r[1:], r[:-1]).reshape(n, cols)
        y = y + taps[-1 - s] * shifted
    return y


def _rotary_split(x, cos, sin):
    xa, xb = x[:, :LANES], x[:, LANES:]
    return jnp.concatenate([xa * cos - xb * sin, xa * sin + xb * cos], axis=1)


def _head_of_qk_lane(shape, axis):
    lane = lax.broadcasted_iota(jnp.int32, shape, axis)
    return (lane % LANES) // (RET_DK // 2)


def _lru_back(x, gate, front, conv_ref, h_ref, xbuf, hcar, reset, valid):
    xc, pre, halo_old, h_old = front
    tl = x.shape[0]
    a, b = _lru_gate_post(xc, *pre)
    h = _scan_rows(a, b, jnp.where(reset, 0.0, h_old))
    y = h * _gelu(gate)
    tail = jnp.where(valid, x[tl - SUBLANES:, :], halo_old)
    xbuf[...] = tail
    conv_ref[0] = tail[SUBLANES - (CONV_W - 1):, :]
    hlast = jnp.where(valid, h[tl - 1:tl, :], h_old)
    hcar[...] = hlast
    h_ref[0] = hlast
    return y


LRU_PARAMS = ("lru_conv_w", "lru_conv_b", "lru_wr", "lru_br", "lru_wi", "lru_bi", "lru_lambda")


def _inproj_lru_kernel(x_ref, g_ref, wm_ref, wqk_ref, wdt_ref,
                       cw_ref, cb_ref, wr_ref, br_ref, wi_ref, bi_ref, lam_ref,
                       o_ref, y_ref, conv_ref, h_ref, ring, xbuf, hcar, *, tiles_per_seq):
    g = pl.program_id(0)

    @pl.when(g == 0)
    def _():
        xbuf[...] = jnp.zeros_like(xbuf)
        hcar[...] = jnp.zeros_like(hcar)
        ring[...] = jnp.zeros_like(ring)

    valid = g > 0
    reset = jnp.logical_and((g + tiles_per_seq - 1) % tiles_per_seq == 0, valid)
    lru_x = ring[:, 0:D_LRU]
    lru_gate = ring[:, D_LRU:]
    halo_old = xbuf[...]
    h_old = hcar[...]
    xc = _causal_conv(lru_x, jnp.where(reset, 0.0, halo_old), [cw_ref[k:k + 1, :] for k in range(CONV_W)],
                      cb_ref[...])
    front = (xc, _lru_gate_pre(xc, wr_ref, br_ref, wi_ref, bi_ref, lam_ref), halo_old, h_old)

    hn = _rms_rows(x_ref[...], g_ref[...]).astype(BF16)
    for c0 in range(0, COL_DT, IN_COL_CHUNK):
        o0 = c0 - PROMPT_COL0
        if c0 == COL_Q:
            o_ref[:, o0:o0 + IN_COL_CHUNK] = _dot(hn, wqk_ref[...])
        elif c0 < PROMPT_COL0:
            ring[:, c0:c0 + IN_COL_CHUNK] = _dot(hn, wm_ref[:, c0:c0 + IN_COL_CHUNK])
        else:
            o_ref[:, o0:o0 + IN_COL_CHUNK] = _dot(hn, wm_ref[:, c0:c0 + IN_COL_CHUNK])
    o_ref[:, COL_DT - PROMPT_COL0:] = _dot(hn, wdt_ref[...])

    y_ref[...] = _lru_back(lru_x, lru_gate, front, conv_ref, h_ref, xbuf, hcar, reset, valid).astype(
        y_ref.dtype)


def _inproj_lru_prompt(x2d, w, layer, *, bsz, seqlen, tm):
    tps = seqlen // tm
    n_tiles = bsz * tps
    proj_tile = lambda g: jnp.minimum(g, n_tiles - 1)
    lru_tile = lambda g: jnp.maximum(g - 1, 0)
    in_names = ("norm_mix_g", "w_in", "w_qk", "w_dt")
    return pl.pallas_call(
        functools.partial(_inproj_lru_kernel, tiles_per_seq=tps),
        grid=(n_tiles + 1,),
        in_specs=[pl.BlockSpec((tm, D_MODEL), lambda g: (proj_tile(g), 0))]
        + [_lspec(w[n], layer) for n in in_names] + [_lspec(w[n], layer) for n in LRU_PARAMS],
        out_specs=[
            pl.BlockSpec((tm, D_IN_PROMPT), lambda g: (proj_tile(g), 0)),
            pl.BlockSpec((tm, D_LRU), lambda g: (lru_tile(g), 0)),
            pl.BlockSpec((1, CONV_W - 1, D_LRU), lambda g: (lru_tile(g) // tps, 0, 0)),
            pl.BlockSpec((1, 1, D_LRU), lambda g: (lru_tile(g) // tps, 0, 0)),
        ],
        out_shape=[
            jax.ShapeDtypeStruct((bsz * seqlen, D_IN_PROMPT), F32),
            jax.ShapeDtypeStruct((bsz * seqlen, D_LRU), BF16),
            jax.ShapeDtypeStruct((bsz, CONV_W - 1, D_LRU), F32),
            jax.ShapeDtypeStruct((bsz, 1, D_LRU), F32),
        ],
        scratch_shapes=[
            pltpu.VMEM((tm, 2 * D_LRU), F32),
            pltpu.VMEM((SUBLANES, D_LRU), F32),
            pltpu.VMEM((1, D_LRU), F32),
        ],
        compiler_params=_cparams("arbitrary"),
        name="inproj_lru_prompt",
    )(x2d, *[w[n] for n in in_names], *[w[n] for n in LRU_PARAMS])


def _ret_prompt_kernel(q_ref, k_ref, v_ref, g_ref, cos_ref, sin_ref, decay_ref, qdec_ref, kdec_ref,
                       cdec_ref, ng_ref, y_ref, s_ref, state):
    c = pl.program_id(1)

    @pl.when(c == 0)
    def _():
        state[...] = jnp.zeros_like(state)

    head = _head_of_qk_lane((CHUNK, D_QK), 1)
    own_block = (_head_of_qk_lane((D_QK, D_RET), 0)
                 == lax.broadcasted_iota(jnp.int32, (D_QK, D_RET), 1) // RET_DV)
    for ci in range(q_ref.shape[0] // CHUNK):
        rows = slice(ci * CHUNK, (ci + 1) * CHUNK)
        cos = cos_ref[rows, :]
        sin = sin_ref[rows, :]
        q = _rotary_split(q_ref[rows, :], cos, sin)
        k = _rotary_split(k_ref[rows, :], cos, sin) * (RET_DK ** -0.5)
        kb = k.astype(BF16)
        vb = v_ref[rows, :].astype(BF16)
        s_old = state[...]
        o_inter = _dot((q * qdec_ref[...]).astype(BF16), s_old.astype(BF16))
        upd = _dot_tn((k * kdec_ref[...]).astype(BF16), vb)
        state[...] = s_old * cdec_ref[...] + jnp.where(own_block, upd, 0.0)
        for h in range(RET_HEADS):
            cols = slice(h * RET_DV, (h + 1) * RET_DV)
            scores = _dot_nt(jnp.where(head == h, q, 0.0).astype(BF16), kb) * decay_ref[h]
            o = _dot(scores.astype(BF16), vb[:, cols]) + o_inter[:, cols]
            o = _rms_rows(o, ng_ref[h:h + 1, :])
            y_ref[rows, cols] = (_silu(g_ref[rows, cols]) * o).astype(y_ref.dtype)
    half = RET_DK // 2
    for h in range(RET_HEADS):
        cols = slice(h * RET_DV, (h + 1) * RET_DV)
        s_ref[0, h, 0:half, :] = state[h * half:(h + 1) * half, cols]
        s_ref[0, h, half:RET_DK, :] = state[LANES + h * half:LANES + (h + 1) * half, cols]


def _ret_prompt(proj, bsz, seqlen, w, layer, tabs, tr):
    nc = seqlen // tr
    row = lambda b, c: b * nc + c
    const2 = lambda shape: pl.BlockSpec(shape, lambda b, c: (0, 0))
    const3 = lambda shape: pl.BlockSpec(shape, lambda b, c: (0, 0, 0))
    return pl.pallas_call(
        _ret_prompt_kernel,
        grid=(bsz, nc),
        in_specs=[
            pl.BlockSpec((tr, D_QK), lambda b, c: (row(b, c), (COL_Q - PROMPT_COL0) // D_QK)),
            pl.BlockSpec((tr, D_QK), lambda b, c: (row(b, c), (COL_K - PROMPT_COL0) // D_QK)),
            pl.BlockSpec((tr, D_RET), lambda b, c: (row(b, c), (COL_V - PROMPT_COL0) // D_RET)),
            pl.BlockSpec((tr, D_RET), lambda b, c: (row(b, c), (COL_RG - PROMPT_COL0) // D_RET)),
            pl.BlockSpec((tr, LANES), lambda b, c: (c, 0)),
            pl.BlockSpec((tr, LANES), lambda b, c: (c, 0)),
            const3((RET_HEADS, CHUNK, CHUNK)),
            const2((CHUNK, D_QK)),
            const2((CHUNK, D_QK)),
            const2((1, D_RET)),
            _lspec(w["ret_norm_g"], layer),
        ],
        out_specs=[
            pl.BlockSpec((tr, D_RET), lambda b, c: (row(b, c), 0)),
            pl.BlockSpec((1, RET_HEADS, RET_DK, RET_DV), lambda b, c: (b, 0, 0, 0)),
        ],
        out_shape=[
            jax.ShapeDtypeStruct((bsz * seqlen, D_RET), BF16),
            jax.ShapeDtypeStruct((bsz, RET_HEADS, RET_DK, RET_DV), F32),
        ],
        scratch_shapes=[pltpu.VMEM((D_QK, D_RET), F32)],
        compiler_params=_cparams("arbitrary", "arbitrary"),
        name="ret_prompt",
    )(proj, proj, proj, proj, tabs["cos_p"], tabs["sin_p"], tabs["decay"], tabs["qdec"],
      tabs["kdec"], tabs["cdec"], w["ret_norm_g"])


HEADS_PER_GROUP = SSD_HEADS // SSD_GROUPS
PAIRS = SSD_HEADS // 2


def _ssd_tile(z_ref, xbc_ref, dt_ref, cw_ref, cb_ref, dtb_ref, alog_ref, dfull_ref, ng_ref,
              tril_ref, conv_ref, s_ref, xbuf, state, ybuf, reset):
    tr = xbc_ref.shape[0]
    x = xbc_ref[...]
    halo_in = jnp.where(reset, 0.0, xbuf[...])
    xc = _causal_conv(x, halo_in, [cw_ref[k:k + 1, :] for k in range(CONV_W)], cb_ref[...])
    tail = x[tr - SUBLANES:, :]
    xbuf[...] = tail
    conv_ref[0] = tail[SUBLANES - (CONV_W - 1):, :]
    state[...] = jnp.where(reset, 0.0, state[...])
    xc = _silu(xc)
    xs_all = xc[:, :D_SSD]
    bm_all = xc[:, D_SSD:D_SSD + SSD_BC].astype(BF16)
    cm_all = xc[:, D_SSD + SSD_BC:].astype(BF16)

    dt_all = _softplus(dt_ref[...] + dtb_ref[...])
    da_all = dt_all * (-jnp.exp(alog_ref[...]))

    causal = (lax.broadcasted_iota(jnp.int32, (CHUNK, CHUNK), 0)
              >= lax.broadcasted_iota(jnp.int32, (CHUNK, CHUNK), 1))
    lane = lax.broadcasted_iota(jnp.int32, (CHUNK, LANES), 1)
    lo = lane < SSD_HD
    rowi = lax.broadcasted_iota(jnp.int32, (2 * SSD_HD, SSD_DSTATE), 0)
    row_lo = rowi < SSD_HD

    for ci in range(tr // CHUNK):
        rows = slice(ci * CHUNK, (ci + 1) * CHUNK)
        xs, bm, cm, dt = xs_all[rows], bm_all[rows], cm_all[rows], dt_all[rows]
        cum = _dot_exact(tril_ref[...], da_all[rows])
        cum_t = cum.T
        dt_t = dt.T
        ecum = jnp.exp(cum)
        wlast = jnp.exp(cum[CHUNK - 1:CHUNK, :] - cum) * dt
        elast = jnp.exp(cum_t[:, CHUNK - 1:CHUNK])
        cbw = [_dot_nt(cm[:, g * SSD_DSTATE:(g + 1) * SSD_DSTATE], bm[:, g * SSD_DSTATE:(g + 1) * SSD_DSTATE])
               for g in range(SSD_GROUPS)]
        for p in range(PAIRS):
            g = (2 * p) // HEADS_PER_GROUP
            xp = xs[:, p * LANES:(p + 1) * LANES]
            bg = bm[:, g * SSD_DSTATE:(g + 1) * SSD_DSTATE]
            cg = cm[:, g * SSD_DSTATE:(g + 1) * SSD_DSTATE]
            y = None
            for j, sel in ((0, lo), (1, jnp.logical_not(lo))):
                h = 2 * p + j
                seg = cum[:, h:h + 1] - cum_t[h:h + 1, :]
                lmask = jnp.exp(jnp.where(causal, seg, -jnp.inf))
                wgt = cbw[g] * lmask * dt_t[h:h + 1, :]
                yj = _dot(wgt.astype(BF16), jnp.where(sel, xp, 0.0).astype(BF16))
                y = yj if y is None else y + yj
            h0, h1 = 2 * p, 2 * p + 1
            sp = state[p]
            y = y + _dot_nt(cg, sp.astype(BF16)) * jnp.where(lo, ecum[:, h0:h0 + 1], ecum[:, h1:h1 + 1])
            wl = jnp.where(lo, wlast[:, h0:h0 + 1], wlast[:, h1:h1 + 1])
            state[p] = sp * jnp.where(row_lo, elast[h0:h0 + 1, :], elast[h1:h1 + 1, :]) + _dot_tn(
                (xp * wl).astype(BF16), bg)
            ybuf[rows, p * LANES:(p + 1) * LANES] = y + dfull_ref[:, p * LANES:(p + 1) * LANES] * xp
    for p in range(PAIRS):
        s_ref[0, 2 * p] = state[p, :SSD_HD, :]
        s_ref[0, 2 * p + 1] = state[p, SSD_HD:, :]
    return _ssd_gate_norm(ybuf[...], z_ref[...], ng_ref[...])


def _ssd_gate_norm(y, z, ng):
    gated = y * _silu(z)
    gw = D_SSD // SSD_GROUPS
    outs = []
    for g in range(SSD_GROUPS):
        outs.append(_rms_rows(gated[:, g * gw:(g + 1) * gw], ng[:, g * gw:(g + 1) * gw]))
    return jnp.concatenate(outs, axis=1)


SSD_PROMPT_PARAMS = ("ssd_conv_w", "ssd_conv_b", "ssd_dt_bias", "ssd_a_log", "ssd_d_full", "ssd_norm_g")


def _ssd_prompt_kernel(z_ref, xbc_ref, dt_ref, cw_ref, cb_ref, dtb_ref, alog_ref, dfull_ref, ng_ref,
                       tril_ref, y_ref, conv_ref, s_ref, xbuf, state, ybuf):
    c = pl.program_id(1)

    @pl.when(jnp.logical_and(pl.program_id(0) == 0, c == 0))
    def _():
        xbuf[...] = jnp.zeros_like(xbuf)
        state[...] = jnp.zeros_like(state)

    y_ref[...] = _ssd_tile(z_ref, xbc_ref, dt_ref, cw_ref, cb_ref, dtb_ref, alog_ref, dfull_ref, ng_ref,
                           tril_ref, conv_ref, s_ref, xbuf, state, ybuf, c == 0).astype(y_ref.dtype)


def _ssd_prompt(proj, bsz, seqlen, w, layer, tabs, tr):
    nc = seqlen // tr
    row = lambda b, c: b * nc + c
    const2 = lambda shape: pl.BlockSpec(shape, lambda b, c: (0, 0))
    return pl.pallas_call(
        _ssd_prompt_kernel,
        grid=(bsz, nc),
        in_specs=[
            pl.BlockSpec((tr, D_SSD), lambda b, c: (row(b, c), (COL_Z - PROMPT_COL0) // D_SSD)),
            pl.BlockSpec((tr, SSD_CONV_DIM), lambda b, c: (row(b, c), (COL_XBC - PROMPT_COL0) // SSD_CONV_DIM)),
            pl.BlockSpec((tr, LANES), lambda b, c: (row(b, c), (COL_DT - PROMPT_COL0) // LANES)),
        ] + [_lspec(w[n], layer) for n in SSD_PROMPT_PARAMS] + [const2((CHUNK, CHUNK))],
        out_specs=[
            pl.BlockSpec((tr, D_SSD), lambda b, c: (row(b, c), 0)),
            pl.BlockSpec((1, CONV_W - 1, SSD_CONV_DIM), lambda b, c: (b, 0, 0)),
            pl.BlockSpec((1, SSD_HEADS, SSD_HD, SSD_DSTATE), lambda b, c: (b, 0, 0, 0)),
        ],
        out_shape=[
            jax.ShapeDtypeStruct((bsz * seqlen, D_SSD), BF16),
            jax.ShapeDtypeStruct((bsz, CONV_W - 1, SSD_CONV_DIM), F32),
            jax.ShapeDtypeStruct((bsz, SSD_HEADS, SSD_HD, SSD_DSTATE), F32),
        ],
        scratch_shapes=[
            pltpu.VMEM((SUBLANES, SSD_CONV_DIM), F32),
            pltpu.VMEM((PAIRS, 2 * SSD_HD, SSD_DSTATE), F32),
            pltpu.VMEM((tr, D_SSD), F32),
        ],
        compiler_params=_cparams("arbitrary", "arbitrary"),
        name="ssd_prompt",
    )(proj, proj, proj, *[w[n] for n in SSD_PROMPT_PARAMS], tabs["tril"])


FF_TILES = ((0, 1536), (1536, 1280))
FFN_PARAMS = ("w_out", "norm_ffn_g", "ffn_w_up", "ffn_conv_w", "ffn_conv_b", "ffn_w_down")


def _ffn_body(x, ya, yb, yc, wo_ref, g_ref, wup_ref, cw_ref, cb_ref, wdn_ref, fg_ref, conv_fn, final_norm):
    x1 = x + _dot(ya, wo_ref[0:D_LRU, :])
    x1 = x1 + _dot(yb, wo_ref[D_LRU:D_LRU + D_RET, :])
    x1 = x1 + _dot(yc, wo_ref[D_LRU + D_RET:, :])
    hn = _rms_rows(x1, g_ref[...]).astype(BF16)
    acc = None
    for f0, fw in FF_TILES:
        gate = _dot(hn, wup_ref[:, f0:f0 + fw])
        up = _dot(hn, wup_ref[:, D_FF + f0:D_FF + f0 + fw])
        taps = [cw_ref[k:k + 1, f0:f0 + fw] for k in range(FFN_CONV_W)]
        conv = conv_fn(gate, f0, fw, taps, cb_ref[:, f0:f0 + fw])
        act = (_gelu(conv) * up).astype(BF16)
        d = _dot(act, wdn_ref[f0:f0 + fw, :])
        acc = d if acc is None else acc + d
    acc = x1 + acc
    if final_norm:
        acc = _rms_rows(acc, fg_ref[...])
    return acc


def _ffn_decode_kernel(x_ref, ya_ref, yb_ref, yc_ref, wo_ref, g_ref, wup_ref, cw_ref, cb_ref, wdn_ref, fg_ref,
                       st_ref, o_ref, conv_ref, *, final_norm):
    def conv_fn(gate, f0, fw, taps, cb):
        s0 = st_ref[:, f0:f0 + fw]
        s1 = st_ref[:, D_FF + f0:D_FF + f0 + fw]
        conv_ref[:, f0:f0 + fw] = s1
        conv_ref[:, D_FF + f0:D_FF + f0 + fw] = gate
        return cb + taps[0] * s0 + taps[1] * s1 + taps[2] * gate

    o_ref[...] = _ffn_body(x_ref[...], ya_ref[...], yb_ref[...], yc_ref[...], wo_ref, g_ref, wup_ref,
                           cw_ref, cb_ref, wdn_ref, fg_ref, conv_fn, final_norm)


def _ffn_prompt_kernel(x_ref, ya_ref, yb_ref, yc_ref, wo_ref, g_ref, wup_ref, cw_ref, cb_ref, wdn_ref, fg_ref,
                       o_ref, fconv_ref, halo, *, final_norm):
    tm = x_ref.shape[0]

    @pl.when(pl.program_id(1) == 0)
    def _():
        halo[...] = jnp.zeros_like(halo)

    def conv_fn(gate, f0, fw, taps, cb):
        conv = _causal_conv(gate, halo[:, f0:f0 + fw], taps, cb)
        tail = gate[tm - SUBLANES:, :]
        halo[:, f0:f0 + fw] = tail
        fconv_ref[0, :, f0:f0 + fw] = tail[SUBLANES - (FFN_CONV_W - 1):, :]
        return conv

    o_ref[...] = _ffn_body(x_ref[...], ya_ref[...], yb_ref[...], yc_ref[...], wo_ref, g_ref, wup_ref,
                           cw_ref, cb_ref, wdn_ref, fg_ref, conv_fn, final_norm)


def _ffn_decode(x2d, ya, yb, yc, w, layer, fg, st, *, final_norm):
    t = x2d.shape[0]
    rowmap = lambda i: (0, 0)
    blk = lambda width: pl.BlockSpec((t, width), rowmap)
    return pl.pallas_call(
        functools.partial(_ffn_decode_kernel, final_norm=final_norm),
        grid=(1,),
        in_specs=[blk(D_MODEL), blk(D_LRU), blk(D_RET), blk(D_SSD)]
        + [_lspec(w[n], layer) for n in FFN_PARAMS]
        + [pl.BlockSpec((1, D_MODEL), rowmap), blk(2 * D_FF)],
        out_specs=[blk(D_MODEL), blk(2 * D_FF)],
        out_shape=[jax.ShapeDtypeStruct((t, D_MODEL), F32), jax.ShapeDtypeStruct((t, 2 * D_FF), F32)],
        compiler_params=_cparams("arbitrary"),
        name="ffn_decode",
    )(x2d, ya, yb, yc, *[w[n] for n in FFN_PARAMS], fg, st)


def _ffn_prompt(x2d, ya, yb, yc, w, layer, fg, *, bsz, seqlen, tm, final_norm):
    nt = seqlen // tm
    rows = lambda width: pl.BlockSpec((tm, width), lambda b, i: (b * nt + i, 0))
    return pl.pallas_call(
        functools.partial(_ffn_prompt_kernel, final_norm=final_norm),
        grid=(bsz, nt),
        in_specs=[rows(D_MODEL), rows(D_LRU), rows(D_RET), rows(D_SSD)]
        + [_lspec(w[n], layer) for n in FFN_PARAMS] + [pl.BlockSpec((1, D_MODEL), lambda b, i: (0, 0))],
        out_specs=[rows(D_MODEL), pl.BlockSpec((1, FFN_CONV_W - 1, D_FF), lambda b, i: (b, 0, 0))],
        out_shape=[
            jax.ShapeDtypeStruct((bsz * seqlen, D_MODEL), F32),
            jax.ShapeDtypeStruct((bsz, FFN_CONV_W - 1, D_FF), F32),
        ],
        scratch_shapes=[pltpu.VMEM((SUBLANES, D_FF), F32)],
        compiler_params=_cparams("arbitrary", "arbitrary"),
        name="ffn_prompt",
    )(x2d, ya, yb, yc, *[w[n] for n in FFN_PARAMS], fg)


DEC_TILE = 8


MIX_DECODE_INPUTS = 35


def _mix_decode_kernel(*refs, n_alias):
    _mix_decode_body(*refs[:MIX_DECODE_INPUTS], *refs[MIX_DECODE_INPUTS + n_alias:])


def _mix_decode_body(lx_ref, lg_ref, q_ref, k_ref, v_ref, rg_ref, z_ref, xbc_ref, dt_ref,
                       lconv_ref, lh_ref, rs_ref, sconv_ref, ss_ref,
                       lcw_ref, lcb_ref, wr_ref, br_ref, wi_ref, bi_ref, lam_ref,
                       cos_ref, sin_ref, cdec_ref, rng_ref,
                       scw_ref, scb_ref, dtb_ref, alog_ref, dfull_ref, sng_ref,
                       eye_ref, hexp_ref, hexp_tile_ref, rsel_ref,
                       ya_ref, yb_ref, yc_ref, lconv_o, lh_o, rs_o, sconv_o, ss_o,
                       obuf, ybuf):
    x = lx_ref[...]
    st = lconv_ref[...]
    xc = lcb_ref[...] + lcw_ref[0:1, :] * st[:, 0:D_LRU] + lcw_ref[1:2, :] * st[:, D_LRU:2 * D_LRU]
    xc = xc + lcw_ref[2:3, :] * st[:, 2 * D_LRU:] + lcw_ref[3:4, :] * x
    lconv_o[:, 0:2 * D_LRU] = st[:, D_LRU:]
    lconv_o[:, 2 * D_LRU:] = x
    a, b = _lru_gates(xc, wr_ref, br_ref, wi_ref, bi_ref, lam_ref)
    h = b + a * lh_ref[...]
    lh_o[...] = h
    ya_ref[...] = (h * _gelu(lg_ref[...])).astype(ya_ref.dtype)

    cos = cos_ref[...]
    sin = sin_ref[...]
    q = _rotary_split(q_ref[...], cos, sin)
    k = _rotary_split(k_ref[...], cos, sin) * (RET_DK ** -0.5)
    v = v_ref[...]

    xin = xbc_ref[...]
    sst = sconv_ref[...]
    n = SSD_CONV_DIM
    xc = scb_ref[...] + scw_ref[0:1, :] * sst[:, 0:n] + scw_ref[1:2, :] * sst[:, n:2 * n]
    xc = xc + scw_ref[2:3, :] * sst[:, 2 * n:] + scw_ref[3:4, :] * xin
    sconv_o[:, 0:2 * n] = sst[:, n:]
    sconv_o[:, 2 * n:] = xin
    xc = _silu(xc)
    xs = xc[:, :D_SSD]
    bm = xc[:, D_SSD:D_SSD + SSD_BC]
    cm_b = xc[:, D_SSD + SSD_BC:].astype(BF16)
    dt = _softplus(dt_ref[...] + dtb_ref[...])
    dt_full = _dot_exact(dt, hexp_ref[...])
    eda_tiles = _dot_exact(jnp.exp(dt * (-jnp.exp(alog_ref[...]))), hexp_tile_ref[...])

    cols_src = jnp.concatenate([xs * dt_full, k], axis=1)
    cols_src = jnp.concatenate(
        [cols_src, jnp.zeros((LANES - DEC_TILE, cols_src.shape[1]), F32)], axis=0).astype(BF16)
    eye = eye_ref[...]
    cols_t = jnp.concatenate(
        [_dot_nt(eye, cols_src[:, j * LANES:(j + 1) * LANES]) for j in range(cols_src.shape[1] // LANES)],
        axis=0).astype(BF16)
    k_row0 = D_SSD

    head_q = _head_of_qk_lane((DEC_TILE, D_QK), 1)
    q_heads = jnp.concatenate([jnp.where(head_q == hh, q, 0.0) for hh in range(RET_HEADS)],
                              axis=0).astype(BF16)
    half = RET_DK // 2
    for r in range(DEC_TILE):
        colb = _dot(cols_t, rsel_ref[r])
        pieces = []
        for hh in range(SSD_HEADS):
            g = hh // HEADS_PER_GROUP
            e = jnp.broadcast_to(eda_tiles[r:r + 1, hh * LANES:(hh + 1) * LANES], (SSD_HD, SSD_DSTATE))
            brow = bm[r:r + 1, g * SSD_DSTATE:(g + 1) * SSD_DSTATE]
            s_new = ss_ref[0, r, hh] * e + colb[hh * SSD_HD:(hh + 1) * SSD_HD, :] * brow
            ss_o[0, r, hh] = s_new
            pieces.append(s_new.astype(BF16))
        gw = D_SSD // SSD_GROUPS
        for g in range(SSD_GROUPS):
            sg = jnp.concatenate(pieces[g * HEADS_PER_GROUP:(g + 1) * HEADS_PER_GROUP], axis=0)
            yg = _dot_nt(cm_b[:, g * SSD_DSTATE:(g + 1) * SSD_DSTATE], sg)
            ybuf[r:r + 1, g * gw:(g + 1) * gw] = yg[r:r + 1, :]
        pieces = []
        for part in range(2):
            for hh in range(RET_HEADS):
                r0 = k_row0 + part * LANES + hh * half
                vrow = v[r:r + 1, hh * RET_DV:(hh + 1) * RET_DV]
                s_old = rs_ref[0, r, hh, part * half:(part + 1) * half, :]
                s_new = s_old * cdec_ref[hh] + colb[r0:r0 + half, :] * vrow
                rs_o[0, r, hh, part * half:(part + 1) * half, :] = s_new
                pieces.append(s_new.astype(BF16))
        s_perm = jnp.concatenate(pieces, axis=0)
        res = _dot(q_heads, s_perm)
        for hh in range(RET_HEADS):
            obuf[r:r + 1, hh * RET_DV:(hh + 1) * RET_DV] = res[hh * DEC_TILE + r:hh * DEC_TILE + r + 1, :]

    o = obuf[...]
    rg = rg_ref[...]
    for hh in range(RET_HEADS):
        oh = _rms_rows(o[:, hh * RET_DV:(hh + 1) * RET_DV], rng_ref[hh:hh + 1, :])
        yb_ref[:, hh * RET_DV:(hh + 1) * RET_DV] = (
            _silu(rg[:, hh * RET_DV:(hh + 1) * RET_DV]) * oh).astype(yb_ref.dtype)
    y = ybuf[...] + dfull_ref[...] * xs
    yc_ref[...] = _ssd_gate_norm(y, z_ref[...], sng_ref[...]).astype(yc_ref.dtype)


SSD_DECODE_PARAMS = SSD_PROMPT_PARAMS


def _mix_decode(proj, states, layer, w, tabs, prev_full):
    nb = proj.shape[0]
    depth = states[2].shape[0]
    grid = (nb // DEC_TILE,)
    col = lambda width, c0: pl.BlockSpec((DEC_TILE, width), lambda i: (i, c0 // width))
    const2 = lambda shape: pl.BlockSpec(shape, lambda i: (0, 0))
    const3 = lambda shape: pl.BlockSpec(shape, lambda i: (0, 0, 0))
    lconv, lh, rs, sconv, ss = states
    ret_blk = pl.BlockSpec((1, DEC_TILE, RET_HEADS, RET_DK, RET_DV), lambda i: (layer, i, 0, 0, 0))
    ssd_blk = pl.BlockSpec((1, DEC_TILE, SSD_HEADS, SSD_HD, SSD_DSTATE), lambda i: (layer, i, 0, 0, 0))
    in_specs = [
        col(D_LRU, COL_LRU_X), col(D_LRU, COL_LRU_G), col(D_QK, COL_Q), col(D_QK, COL_K),
        col(D_RET, COL_V), col(D_RET, COL_RG), col(D_SSD, COL_Z), col(SSD_CONV_DIM, COL_XBC),
        col(LANES, COL_DT),
        pl.BlockSpec((DEC_TILE, (CONV_W - 1) * D_LRU), lambda i: (i, 0)),
        pl.BlockSpec((DEC_TILE, D_LRU), lambda i: (i, 0)),
        ret_blk,
        pl.BlockSpec((DEC_TILE, (CONV_W - 1) * SSD_CONV_DIM), lambda i: (i, 0)),
        ssd_blk,
    ] + [_lspec(w[n], layer) for n in LRU_PARAMS] + [
        const2((1, LANES)), const2((1, LANES)), const3((RET_HEADS, 1, 1)), _lspec(w["ret_norm_g"], layer),
    ] + [_lspec(w[n], layer) for n in SSD_DECODE_PARAMS] + [
        const2((LANES, LANES)), const2((LANES, D_SSD)), const2((LANES, SSD_HEADS * LANES)),
        const3((DEC_TILE, LANES, LANES)),
    ]
    args = [proj] * 9 + [lconv, lh, rs, sconv, ss] + [w[n] for n in LRU_PARAMS] + [
        tabs["cos_s"], tabs["sin_s"], tabs["cdec1"], w["ret_norm_g"]] + [
        w[n] for n in SSD_DECODE_PARAMS] + [tabs["eye"], tabs["hexp"], tabs["hexp_tile"], tabs["rsel"]]
    assert len(args) == MIX_DECODE_INPUTS
    aliases = {}
    if prev_full is not None:
        in_specs += [pl.BlockSpec(memory_space=pl.ANY)] * 2
        args += list(prev_full)
        aliases = {MIX_DECODE_INPUTS: 5, MIX_DECODE_INPUTS + 1: 7}
    out_specs = [
        pl.BlockSpec((DEC_TILE, D_LRU), lambda i: (i, 0)),
        pl.BlockSpec((DEC_TILE, D_RET), lambda i: (i, 0)),
        pl.BlockSpec((DEC_TILE, D_SSD), lambda i: (i, 0)),
        pl.BlockSpec((DEC_TILE, (CONV_W - 1) * D_LRU), lambda i: (i, 0)),
        pl.BlockSpec((DEC_TILE, D_LRU), lambda i: (i, 0)),
        ret_blk,
        pl.BlockSpec((DEC_TILE, (CONV_W - 1) * SSD_CONV_DIM), lambda i: (i, 0)),
        ssd_blk,
    ]
    out_shape = [
        jax.ShapeDtypeStruct((nb, D_LRU), BF16),
        jax.ShapeDtypeStruct((nb, D_RET), BF16),
        jax.ShapeDtypeStruct((nb, D_SSD), BF16),
        jax.ShapeDtypeStruct((nb, (CONV_W - 1) * D_LRU), F32),
        jax.ShapeDtypeStruct((nb, D_LRU), F32),
        jax.ShapeDtypeStruct((depth, nb, RET_HEADS, RET_DK, RET_DV), F32),
        jax.ShapeDtypeStruct((nb, (CONV_W - 1) * SSD_CONV_DIM), F32),
        jax.ShapeDtypeStruct((depth, nb, SSD_HEADS, SSD_HD, SSD_DSTATE), F32),
    ]
    return pl.pallas_call(
        functools.partial(_mix_decode_kernel, n_alias=len(aliases)),
        grid=grid, in_specs=in_specs, out_specs=out_specs, out_shape=out_shape,
        input_output_aliases=aliases,
        scratch_shapes=[pltpu.VMEM((DEC_TILE, D_RET), F32), pltpu.VMEM((DEC_TILE, D_SSD), F32)],
        compiler_params=_cparams("arbitrary"),
        name="mix_decode",
    )(*args)


def _qk_perm():
    half = RET_DK // 2
    idx = np.empty((D_QK,), np.int32)
    for part in range(2):
        for h in range(RET_HEADS):
            for j in range(half):
                idx[part * LANES + h * half + j] = h * RET_DK + part * half + j
    return idx


def _block_diag(wh):
    nh, hd, _ = wh.shape
    eye = jnp.eye(nh, dtype=wh.dtype)
    return (eye[:, None, :, None] * wh[:, :, None, :]).reshape(nh * hd, nh * hd)


def _prep_weights(p):
    perm = _qk_perm()
    w_in = p["w_in"]
    depth = w_in.shape[0]
    w_qk = jnp.concatenate([w_in[:, :, COL_Q:COL_Q + D_QK][:, :, perm],
                            w_in[:, :, COL_K:COL_K + D_QK][:, :, perm]], axis=2).astype(BF16)
    n_dt = w_in.shape[2] - COL_DT
    w_dt = jnp.concatenate([w_in[:, :, COL_DT:], jnp.zeros((depth, D_MODEL, LANES - n_dt), w_in.dtype)],
                           axis=2).astype(BF16)
    row = lambda v: v[:, None, :]
    pad_lanes = lambda v: row(jnp.concatenate([v, jnp.zeros((depth, LANES - v.shape[1]), v.dtype)], axis=1))
    return {
        "norm_mix_g": row(p["norm_mix_g"]),
        "w_in": w_in.astype(BF16),
        "w_qk": w_qk,
        "w_dt": w_dt,
        "lru_conv_w": p["lru_conv_w"],
        "lru_conv_b": row(p["lru_conv_b"]),
        "lru_wr": jax.vmap(_block_diag)(p["lru_wr"]).astype(BF16),
        "lru_br": row(p["lru_br"]),
        "lru_wi": jax.vmap(_block_diag)(p["lru_wi"]).astype(BF16),
        "lru_bi": row(p["lru_bi"]),
        "lru_lambda": row(p["lru_lambda"]),
        "ret_norm_g": p["ret_norm_g"],
        "ssd_conv_w": p["ssd_conv_w"],
        "ssd_conv_b": row(p["ssd_conv_b"]),
        "ssd_dt_bias": pad_lanes(p["ssd_dt_bias"]),
        "ssd_a_log": pad_lanes(p["ssd_a_log"]),
        "ssd_d_full": row(jnp.repeat(p["ssd_d"], SSD_HD, axis=1)),
        "ssd_norm_g": row(p["ssd_norm_g"]),
        "w_out": p["w_out"].astype(BF16),
        "norm_ffn_g": row(p["norm_ffn_g"]),
        "ffn_w_up": p["ffn_w_up"].astype(BF16),
        "ffn_conv_w": p["ffn_conv_w"],
        "ffn_conv_b": row(p["ffn_conv_b"]),
        "ffn_w_down": p["ffn_w_down"].astype(BF16),
    }


def _tables(seqlen, past_len):
    half = RET_DK // 2
    freqs = ROPE_BASE ** (-jnp.arange(half, dtype=F32) / half)

    def cs(pos):
        ang = pos.astype(F32)[:, None] * freqs[None, :]
        return jnp.tile(jnp.cos(ang), (1, RET_HEADS)), jnp.tile(jnp.sin(ang), (1, RET_HEADS))

    cos_p, sin_p = cs(jnp.arange(seqlen, dtype=jnp.int32))
    cos_s, sin_s = cs(past_len + jnp.arange(1, dtype=jnp.int32))
    log_g = jnp.log1p(-jnp.exp2(-5.0 - jnp.arange(RET_HEADS, dtype=F32)))
    idx = jnp.arange(CHUNK, dtype=F32)
    diff = idx[:, None] - idx[None, :]
    decay = jnp.exp(jnp.where(diff[None] >= 0, diff[None] * log_g[:, None, None], -jnp.inf))
    q_dec = jnp.exp((idx[None, :] + 1.0) * log_g[:, None])
    k_dec = jnp.exp((CHUNK - 1.0 - idx[None, :]) * log_g[:, None])
    lanes_of = lambda t: jnp.tile(jnp.repeat(t.T, half, axis=1), (1, 2))
    hexp = (jnp.arange(LANES)[:, None] == (jnp.arange(D_SSD)[None, :] // SSD_HD)).astype(F32)
    return {
        "cos_p": cos_p, "sin_p": sin_p, "cos_s": cos_s, "sin_s": sin_s,
        "decay": decay, "qdec": lanes_of(q_dec), "kdec": lanes_of(k_dec),
        "cdec": jnp.repeat(jnp.exp(CHUNK * log_g), RET_DV)[None, :],
        "cdec1": jnp.exp(log_g)[:, None, None],
        "tril": jnp.tril(jnp.ones((CHUNK, CHUNK), F32)),
        "eye": jnp.eye(LANES, dtype=BF16),
        "hexp": hexp,
        "hexp_tile": (jnp.arange(LANES)[:, None] == (jnp.arange(SSD_HEADS * LANES)[None, :] // LANES)).astype(F32),
        "rsel": (jnp.arange(LANES)[None, :, None] == jnp.arange(DEC_TILE)[:, None, None]).astype(BF16)
        * jnp.ones((1, 1, LANES), BF16),
    }


TM_PROMPT = 512
TR_MIX = 512


def _prompt_stack(x, w, tabs, fg, tm=TM_PROMPT, tr=TR_MIX):
    bsz, seqlen, _ = x.shape
    depth = w["w_in"].shape[0]
    x2d = x.reshape(bsz * seqlen, D_MODEL)
    outs = [[] for _ in range(6)]
    for l in range(depth):
        proj, ya, lconv, lh = _inproj_lru_prompt(x2d, w, l, bsz=bsz, seqlen=seqlen, tm=tm)
        yb, rs = _ret_prompt(proj, bsz, seqlen, w, l, tabs, tr)
        yc, sconv, ss = _ssd_prompt(proj, bsz, seqlen, w, l, tabs, tr)
        x2d, fconv = _ffn_prompt(x2d, ya, yb, yc, w, l, fg, bsz=bsz, seqlen=seqlen, tm=tm,
                                 final_norm=(l == depth - 1))
        for lst, val in zip(outs, (lconv, lh[:, 0, :], rs, sconv, ss, fconv)):
            lst.append(val)
    return x2d.reshape(bsz, seqlen, D_MODEL), [jnp.stack(o, axis=0) for o in outs]


def _sample_stack(x, states, w, tabs, fg):
    nb = x.shape[0]
    depth = w["w_in"].shape[0]
    x2d = x.reshape(nb, D_MODEL)
    st_lconv, st_lh, st_rs, st_sconv, st_ss, st_fconv = states
    lconv_in = st_lconv.reshape(depth, nb, (CONV_W - 1) * D_LRU)
    sconv_in = st_sconv.reshape(depth, nb, (CONV_W - 1) * SSD_CONV_DIM)
    fconv_in = st_fconv.reshape(depth, nb, (FFN_CONV_W - 1) * D_FF)
    outs = [[] for _ in range(4)]
    full = None
    for l in range(depth):
        proj = _inproj(x2d, w, l, tm=nb)
        ya, yb, yc, lconv, lh, rs_full, sconv, ss_full = _mix_decode(
            proj, (lconv_in[l], st_lh[l], st_rs, sconv_in[l], st_ss), l, w, tabs, full)
        full = (rs_full, ss_full)
        x2d, fconv = _ffn_decode(x2d, ya, yb, yc, w, l, fg, fconv_in[l], final_norm=(l == depth - 1))
        for lst, val in zip(outs, (lconv, lh, sconv, fconv)):
            lst.append(val)
    lconv, lh, sconv, fconv = [jnp.stack(o, axis=0) for o in outs]
    new_states = [lconv.reshape(depth, nb, CONV_W - 1, D_LRU), lh, full[0],
                  sconv.reshape(depth, nb, CONV_W - 1, SSD_CONV_DIM), full[1],
                  fconv.reshape(depth, nb, FFN_CONV_W - 1, D_FF)]
    return x2d.reshape(nb, 1, D_MODEL), new_states


def kernel(x_prompt, x_sample, state_lru_conv, state_lru_h, state_ret, state_ssd_conv, state_ssd, state_ffn_conv,
           norm_mix_g, w_in, lru_conv_w, lru_conv_b, lru_wr, lru_br, lru_wi, lru_bi, lru_lambda,
           ret_norm_g, ssd_conv_w, ssd_conv_b, ssd_dt_bias, ssd_a_log, ssd_d, ssd_norm_g,
           w_out, norm_ffn_g, ffn_w_up, ffn_conv_w, ffn_conv_b, ffn_w_down, norm_final_g):
    p = {"norm_mix_g": norm_mix_g, "w_in": w_in, "lru_conv_w": lru_conv_w, "lru_conv_b": lru_conv_b,
         "lru_wr": lru_wr, "lru_br": lru_br, "lru_wi": lru_wi, "lru_bi": lru_bi, "lru_lambda": lru_lambda,
         "ret_norm_g": ret_norm_g, "ssd_conv_w": ssd_conv_w, "ssd_conv_b": ssd_conv_b,
         "ssd_dt_bias": ssd_dt_bias, "ssd_a_log": ssd_a_log, "ssd_d": ssd_d, "ssd_norm_g": ssd_norm_g,
         "w_out": w_out, "norm_ffn_g": norm_ffn_g, "ffn_w_up": ffn_w_up, "ffn_conv_w": ffn_conv_w,
         "ffn_conv_b": ffn_conv_b, "ffn_w_down": ffn_w_down}
    weights = _prep_weights(p)
    past_len = 16384
    tabs = _tables(x_prompt.shape[1], past_len)
    fg = norm_final_g[None, :]

    y_p, ns_p = _prompt_stack(x_prompt, weights, tabs, fg)
    y_s, ns_s = _sample_stack(
        x_sample, (state_lru_conv, state_lru_h, state_ret, state_ssd_conv, state_ssd, state_ffn_conv),
        weights, tabs, fg)
    lru_conv_p, lru_h_p, ret_p, ssd_conv_p, ssd_p, ffn_conv_p = ns_p
    lru_conv_s, lru_h_s, ret_s, ssd_conv_s, ssd_s, ffn_conv_s = ns_s
    return (y_p, y_s, lru_conv_p, lru_conv_s, lru_h_p, lru_h_s, ret_p, ret_s,
            ssd_conv_p, ssd_conv_s, ssd_p, ssd_s, ffn_conv_p, ffn_conv_s)
```

```python
import functools

import jax
import jax.numpy as jnp
import numpy as np
from jax import lax
from jax.experimental import pallas as pl
from jax.experimental.pallas import tpu as pltpu

F32 = jnp.float32
BF16 = jnp.bfloat16

D_MODEL = 1024
DEPTH = 4
D_LRU = 512
LRU_HEADS = 8
LRU_HD = D_LRU // LRU_HEADS
LRU_C = 8.0
CONV_W = 4
RET_HEADS = 4
RET_DK = 64
RET_DV = 128
D_RET = RET_HEADS * RET_DV
D_QK = RET_HEADS * RET_DK
ROPE_BASE = 10000.0
SSD_HD = 64
D_SSD = 512
SSD_HEADS = 8
SSD_GROUPS = 2
SSD_DSTATE = 128
SSD_BC = SSD_GROUPS * SSD_DSTATE
SSD_CONV_DIM = D_SSD + 2 * SSD_BC
D_MIX = D_LRU + D_RET + D_SSD
D_FF = 2816
FFN_CONV_W = 3
EPS = 1e-6
CHUNK = 128

COL_LRU_X = 0
COL_LRU_G = 512
COL_Q = 1024
COL_K = 1280
COL_V = 1536
COL_RG = 2048
COL_Z = 2560
COL_XBC = 3072
COL_DT = 4096
D_IN_PAD = 4224
PROMPT_COL0 = COL_Q
D_IN_PROMPT = D_IN_PAD - PROMPT_COL0

LANES = 128
SUBLANES = 8
VMEM_LIMIT_BYTES = 56 * 1024 * 1024


def _cparams(*sem):
    return pltpu.CompilerParams(dimension_semantics=sem, vmem_limit_bytes=VMEM_LIMIT_BYTES)


def _dot(a, b):
    return jnp.dot(a, b, preferred_element_type=F32)


def _dot_nt(a, b):
    return lax.dot_general(a, b, (((1,), (1,)), ((), ())), preferred_element_type=F32)


def _dot_tn(a, b):
    return lax.dot_general(a, b, (((0,), (0,)), ((), ())), preferred_element_type=F32)


def _dot_exact(a, b):
    return jnp.dot(a, b, preferred_element_type=F32, precision=lax.Precision.HIGHEST)


def _dot_nt_exact(a, b):
    return lax.dot_general(a, b, (((1,), (1,)), ((), ())), preferred_element_type=F32,
                           precision=lax.Precision.HIGHEST)


def _rms_rows(x, g):
    return x * lax.rsqrt(jnp.mean(x * x, axis=-1, keepdims=True) + EPS) * g


def _sigmoid(x):
    return 1.0 / (1.0 + jnp.exp(-x))


def _silu(x):
    return x * _sigmoid(x)


def _gelu(x):
    return jax.nn.gelu(x, approximate=True)


def _softplus(x):
    return jnp.maximum(x, 0.0) + jnp.log1p(jnp.exp(-jnp.abs(x)))


IN_COL_CHUNK = 512


def _lspec(arr, layer):
    _, r, c = arr.shape
    return pl.BlockSpec((None, r, c), lambda *_: (layer, 0, 0), pipeline_mode=pl.Buffered(1))


def _inproj_kernel(x_ref, g_ref, wm_ref, wqk_ref, wdt_ref, o_ref):
    hn = _rms_rows(x_ref[...], g_ref[...]).astype(BF16)
    for c0 in range(0, COL_DT, IN_COL_CHUNK):
        if c0 == COL_Q:
            o_ref[:, c0:c0 + IN_COL_CHUNK] = _dot(hn, wqk_ref[...])
        else:
            o_ref[:, c0:c0 + IN_COL_CHUNK] = _dot(hn, wm_ref[:, c0:c0 + IN_COL_CHUNK])
    o_ref[:, COL_DT:] = _dot(hn, wdt_ref[...])


def _inproj(x2d, w, layer, tm):
    t = x2d.shape[0]
    return pl.pallas_call(
        _inproj_kernel,
        grid=(t // tm,),
        in_specs=[
            pl.BlockSpec((tm, D_MODEL), lambda i: (i, 0)),
            _lspec(w["norm_mix_g"], layer),
            _lspec(w["w_in"], layer),
            _lspec(w["w_qk"], layer),
            _lspec(w["w_dt"], layer),
        ],
        out_specs=pl.BlockSpec((tm, D_IN_PAD), lambda i: (i, 0)),
        out_shape=jax.ShapeDtypeStruct((t, D_IN_PAD), F32),
        compiler_params=_cparams("arbitrary"),
        name="inproj",
    )(x2d, w["norm_mix_g"], w["w_in"], w["w_qk"], w["w_dt"])


def _lru_gate_pre(xc, wr_ref, br_ref, wi_ref, bi_ref, lam_ref):
    xcb = xc.astype(BF16)
    return (_dot(xcb, wr_ref[...]) + br_ref[...], _dot(xcb, wi_ref[...]) + bi_ref[...],
            _softplus(-lam_ref[...]))


def _lru_gate_post(xc, r_pre, i_pre, sp):
    log_a = (-LRU_C * _sigmoid(r_pre)) * sp
    a = jnp.exp(log_a)
    b = jnp.sqrt(-jnp.tanh(log_a) * (a * a + 1.0)) * (_sigmoid(i_pre) * xc)
    return a, b


def _lru_gates(xc, wr_ref, br_ref, wi_ref, bi_ref, lam_ref):
    return _lru_gate_post(xc, *_lru_gate_pre(xc, wr_ref, br_ref, wi_ref, bi_ref, lam_ref))


def _scan_rows(a, b, h_in):
    n, cols = a.shape
    groups = n // SUBLANES
    a = a.reshape(groups, SUBLANES, cols)
    b = b.reshape(groups, SUBLANES, cols)
    row = lax.broadcasted_iota(jnp.int32, a.shape, 1)
    d = 1
    while d < SUBLANES:
        keep = row >= d
        a_sh = jnp.where(keep, pltpu.roll(a, d, axis=1), 1.0)
        b_sh = jnp.where(keep, pltpu.roll(b, d, axis=1), 0.0)
        b = a * b_sh + b
        a = a * a_sh
        d *= 2
    hs = []
    h_prev = h_in
    for g in range(groups):
        hg = b[g] + a[g] * h_prev
        hs.append(hg)
        h_prev = hg[SUBLANES - 1:SUBLANES, :]
    return jnp.concatenate(hs, axis=0)


def _causal_conv(x, halo, taps, bias):
    n, cols = x.shape
    groups = n // SUBLANES
    xe = jnp.concatenate([halo, x], axis=0).reshape(groups + 1, SUBLANES, cols)
    row = lax.broadcasted_iota(jnp.int32, (groups, SUBLANES, cols), 1)
    y = bias + taps[-1] * x
    for s in range(1, len(taps)):
        r = pltpu.roll(xe, s, axis=1)
        shifted = jnp.where(row >= s, r[1:], r[:-1]).reshape(n, cols)
        y = y + taps[-1 - s] * shifted
    return y


def _rotary_split(x, cos, sin):
    xa, xb = x[:, :LANES], x[:, LANES:]
    return jnp.concatenate([xa * cos - xb * sin, xa * sin + xb * cos], axis=1)


def _head_of_qk_lane(shape, axis):
    lane = lax.broadcasted_iota(jnp.int32, shape, axis)
    return (lane % LANES) // (RET_DK // 2)


def _lru_back(x, gate, front, conv_ref, h_ref, xbuf, hcar, reset, valid):
    xc, pre, halo_old, h_old = front
    tl = x.shape[0]
    a, b = _lru_gate_post(xc, *pre)
    h = _scan_rows(a, b, jnp.where(reset, 0.0, h_old))
    y = h * _gelu(gate)
    tail = jnp.where(valid, x[tl - SUBLANES:, :], halo_old)
    xbuf[...] = tail
    conv_ref[0] = tail[SUBLANES - (CONV_W - 1):, :]
    hlast = jnp.where(valid, h[tl - 1:tl, :], h_old)
    hcar[...] = hlast
    h_ref[0] = hlast
    return y


LRU_PARAMS = ("lru_conv_w", "lru_conv_b", "lru_wr", "lru_br", "lru_wi", "lru_bi", "lru_lambda")


def _inproj_lru_kernel(x_ref, g_ref, wm_ref, wqk_ref, wdt_ref,
                       cw_ref, cb_ref, wr_ref, br_ref, wi_ref, bi_ref, lam_ref,
                       o_ref, y_ref, conv_ref, h_ref, ring, xbuf, hcar, *, tiles_per_seq):
    g = pl.program_id(0)

    @pl.when(g == 0)
    def _():
        xbuf[...] = jnp.zeros_like(xbuf)
        hcar[...] = jnp.zeros_like(hcar)
        ring[...] = jnp.zeros_like(ring)

    valid = g > 0
    reset = jnp.logical_and((g + tiles_per_seq - 1) % tiles_per_seq == 0, valid)
    lru_x = ring[:, 0:D_LRU]
    lru_gate = ring[:, D_LRU:]
    halo_old = xbuf[...]
    h_old = hcar[...]
    xc = _causal_conv(lru_x, jnp.where(reset, 0.0, halo_old), [cw_ref[k:k + 1, :] for k in range(CONV_W)],
                      cb_ref[...])
    front = (xc, _lru_gate_pre(xc, wr_ref, br_ref, wi_ref, bi_ref, lam_ref), halo_old, h_old)

    hn = _rms_rows(x_ref[...], g_ref[...]).astype(BF16)
    for c0 in range(0, COL_DT, IN_COL_CHUNK):
        o0 = c0 - PROMPT_COL0
        if c0 == COL_Q:
            o_ref[:, o0:o0 + IN_COL_CHUNK] = _dot(hn, wqk_ref[...])
        elif c0 < PROMPT_COL0:
            ring[:, c0:c0 + IN_COL_CHUNK] = _dot(hn, wm_ref[:, c0:c0 + IN_COL_CHUNK])
        else:
            o_ref[:, o0:o0 + IN_COL_CHUNK] = _dot(hn, wm_ref[:, c0:c0 + IN_COL_CHUNK])
    o_ref[:, COL_DT - PROMPT_COL0:] = _dot(hn, wdt_ref[...])

    y_ref[...] = _lru_back(lru_x, lru_gate, front, conv_ref, h_ref, xbuf, hcar, reset, valid).astype(
        y_ref.dtype)


def _inproj_lru_prompt(x2d, w, layer, *, bsz, seqlen, tm):
    tps = seqlen // tm
    n_tiles = bsz * tps
    proj_tile = lambda g: jnp.minimum(g, n_tiles - 1)
    lru_tile = lambda g: jnp.maximum(g - 1, 0)
    in_names = ("norm_mix_g", "w_in", "w_qk", "w_dt")
    return pl.pallas_call(
        functools.partial(_inproj_lru_kernel, tiles_per_seq=tps),
        grid=(n_tiles + 1,),
        in_specs=[pl.BlockSpec((tm, D_MODEL), lambda g: (proj_tile(g), 0))]
        + [_lspec(w[n], layer) for n in in_names] + [_lspec(w[n], layer) for n in LRU_PARAMS],
        out_specs=[
            pl.BlockSpec((tm, D_IN_PROMPT), lambda g: (proj_tile(g), 0)),
            pl.BlockSpec((tm, D_LRU), lambda g: (lru_tile(g), 0)),
            pl.BlockSpec((1, CONV_W - 1, D_LRU), lambda g: (lru_tile(g) // tps, 0, 0)),
            pl.BlockSpec((1, 1, D_LRU), lambda g: (lru_tile(g) // tps, 0, 0)),
        ],
        out_shape=[
            jax.ShapeDtypeStruct((bsz * seqlen, D_IN_PROMPT), F32),
            jax.ShapeDtypeStruct((bsz * seqlen, D_LRU), BF16),
            jax.ShapeDtypeStruct((bsz, CONV_W - 1, D_LRU), F32),
            jax.ShapeDtypeStruct((bsz, 1, D_LRU), F32),
        ],
        scratch_shapes=[
            pltpu.VMEM((tm, 2 * D_LRU), F32),
            pltpu.VMEM((SUBLANES, D_LRU), F32),
            pltpu.VMEM((1, D_LRU), F32),
        ],
        compiler_params=_cparams("arbitrary"),
        name="inproj_lru_prompt",
    )(x2d, *[w[n] for n in in_names], *[w[n] for n in LRU_PARAMS])


def _ret_prompt_kernel(q_ref, k_ref, v_ref, g_ref, cos_ref, sin_ref, decay_ref, qdec_ref, kdec_ref,
                       cdec_ref, ng_ref, y_ref, s_ref, state):
    c = pl.program_id(1)

    @pl.when(c == 0)
    def _():
        state[...] = jnp.zeros_like(state)

    head = _head_of_qk_lane((CHUNK, D_QK), 1)
    own_block = (_head_of_qk_lane((D_QK, D_RET), 0)
                 == lax.broadcasted_iota(jnp.int32, (D_QK, D_RET), 1) // RET_DV)
    for ci in range(q_ref.shape[0] // CHUNK):
        rows = slice(ci * CHUNK, (ci + 1) * CHUNK)
        cos = cos_ref[rows, :]
        sin = sin_ref[rows, :]
        q = _rotary_split(q_ref[rows, :], cos, sin)
        k = _rotary_split(k_ref[rows, :], cos, sin) * (RET_DK ** -0.5)
        kb = k.astype(BF16)
        vb = v_ref[rows, :].astype(BF16)
        s_old = state[...]
        o_inter = _dot((q * qdec_ref[...]).astype(BF16), s_old.astype(BF16))
        upd = _dot_tn((k * kdec_ref[...]).astype(BF16), vb)
        state[...] = s_old * cdec_ref[...] + jnp.where(own_block, upd, 0.0)
        for h in range(RET_HEADS):
            cols = slice(h * RET_DV, (h + 1) * RET_DV)
            scores = _dot_nt(jnp.where(head == h, q, 0.0).astype(BF16), kb) * decay_ref[h]
            o = _dot(scores.astype(BF16), vb[:, cols]) + o_inter[:, cols]
            o = _rms_rows(o, ng_ref[h:h + 1, :])
            y_ref[rows, cols] = (_silu(g_ref[rows, cols]) * o).astype(y_ref.dtype)
    half = RET_DK // 2
    for h in range(RET_HEADS):
        cols = slice(h * RET_DV, (h + 1) * RET_DV)
        s_ref[0, h, 0:half, :] = state[h * half:(h + 1) * half, cols]
        s_ref[0, h, half:RET_DK, :] = state[LANES + h * half:LANES + (h + 1) * half, cols]


def _ret_prompt(proj, bsz, seqlen, w, layer, tabs, tr):
    nc = seqlen // tr
    row = lambda b, c: b * nc + c
    const2 = lambda shape: pl.BlockSpec(shape, lambda b, c: (0, 0))
    const3 = lambda shape: pl.BlockSpec(shape, lambda b, c: (0, 0, 0))
    return pl.pallas_call(
        _ret_prompt_kernel,
        grid=(bsz, nc),
        in_specs=[
            pl.BlockSpec((tr, D_QK), lambda b, c: (row(b, c), (COL_Q - PROMPT_COL0) // D_QK)),
            pl.BlockSpec((tr, D_QK), lambda b, c: (row(b, c), (COL_K - PROMPT_COL0) // D_QK)),
            pl.BlockSpec((tr, D_RET), lambda b, c: (row(b, c), (COL_V - PROMPT_COL0) // D_RET)),
            pl.BlockSpec((tr, D_RET), lambda b, c: (row(b, c), (COL_RG - PROMPT_COL0) // D_RET)),
            pl.BlockSpec((tr, LANES), lambda b, c: (c, 0)),
            pl.BlockSpec((tr, LANES), lambda b, c: (c, 0)),
            const3((RET_HEADS, CHUNK, CHUNK)),
            const2((CHUNK, D_QK)),
            const2((CHUNK, D_QK)),
            const2((1, D_RET)),
            _lspec(w["ret_norm_g"], layer),
        ],
        out_specs=[
            pl.BlockSpec((tr, D_RET), lambda b, c: (row(b, c), 0)),
            pl.BlockSpec((1, RET_HEADS, RET_DK, RET_DV), lambda b, c: (b, 0, 0, 0)),
        ],
        out_shape=[
            jax.ShapeDtypeStruct((bsz * seqlen, D_RET), BF16),
            jax.ShapeDtypeStruct((bsz, RET_HEADS, RET_DK, RET_DV), F32),
        ],
        scratch_shapes=[pltpu.VMEM((D_QK, D_RET), F32)],
        compiler_params=_cparams("arbitrary", "arbitrary"),
        name="ret_prompt",
    )(proj, proj, proj, proj, tabs["cos_p"], tabs["sin_p"], tabs["decay"], tabs["qdec"],
      tabs["kdec"], tabs["cdec"], w["ret_norm_g"])


HEADS_PER_GROUP = SSD_HEADS // SSD_GROUPS
PAIRS = SSD_HEADS // 2


def _ssd_tile(z_ref, xbc_ref, dt_ref, cw_ref, cb_ref, dtb_ref, alog_ref, dfull_ref, ng_ref,
              tril_ref, conv_ref, s_ref, xbuf, state, ybuf, reset):
    tr = xbc_ref.shape[0]
    x = xbc_ref[...]
    halo_in = jnp.where(reset, 0.0, xbuf[...])
    xc = _causal_conv(x, halo_in, [cw_ref[k:k + 1, :] for k in range(CONV_W)], cb_ref[...])
    tail = x[tr - SUBLANES:, :]
    xbuf[...] = tail
    conv_ref[0] = tail[SUBLANES - (CONV_W - 1):, :]
    state[...] = jnp.where(reset, 0.0, state[...])
    xc = _silu(xc)
    xs_all = xc[:, :D_SSD]
    bm_all = xc[:, D_SSD:D_SSD + SSD_BC].astype(BF16)
    cm_all = xc[:, D_SSD + SSD_BC:].astype(BF16)

    dt_all = _softplus(dt_ref[...] + dtb_ref[...])
    da_all = dt_all * (-jnp.exp(alog_ref[...]))

    causal = (lax.broadcasted_iota(jnp.int32, (CHUNK, CHUNK), 0)
              >= lax.broadcasted_iota(jnp.int32, (CHUNK, CHUNK), 1))
    lane = lax.broadcasted_iota(jnp.int32, (CHUNK, LANES), 1)
    lo = lane < SSD_HD
    rowi = lax.broadcasted_iota(jnp.int32, (2 * SSD_HD, SSD_DSTATE), 0)
    row_lo = rowi < SSD_HD

    for ci in range(tr // CHUNK):
        rows = slice(ci * CHUNK, (ci + 1) * CHUNK)
        xs, bm, cm, dt = xs_all[rows], bm_all[rows], cm_all[rows], dt_all[rows]
        cum = _dot_exact(tril_ref[...], da_all[rows])
        cum_t = cum.T
        dt_t = dt.T
        ecum = jnp.exp(cum)
        wlast = jnp.exp(cum[CHUNK - 1:CHUNK, :] - cum) * dt
        elast = jnp.exp(cum_t[:, CHUNK - 1:CHUNK])
        cbw = [_dot_nt(cm[:, g * SSD_DSTATE:(g + 1) * SSD_DSTATE], bm[:, g * SSD_DSTATE:(g + 1) * SSD_DSTATE])
               for g in range(SSD_GROUPS)]
        for p in range(PAIRS):
            g = (2 * p) // HEADS_PER_GROUP
            xp = xs[:, p * LANES:(p + 1) * LANES]
            bg = bm[:, g * SSD_DSTATE:(g + 1) * SSD_DSTATE]
            cg = cm[:, g * SSD_DSTATE:(g + 1) * SSD_DSTATE]
            y = None
            for j, sel in ((0, lo), (1, jnp.logical_not(lo))):
                h = 2 * p + j
                seg = cum[:, h:h + 1] - cum_t[h:h + 1, :]
                lmask = jnp.exp(jnp.where(causal, seg, -jnp.inf))
                wgt = cbw[g] * lmask * dt_t[h:h + 1, :]
                yj = _dot(wgt.astype(BF16), jnp.where(sel, xp, 0.0).astype(BF16))
                y = yj if y is None else y + yj
            h0, h1 = 2 * p, 2 * p + 1
            sp = state[p]
            y = y + _dot_nt(cg, sp.astype(BF16)) * jnp.where(lo, ecum[:, h0:h0 + 1], ecum[:, h1:h1 + 1])
            wl = jnp.where(lo, wlast[:, h0:h0 + 1], wlast[:, h1:h1 + 1])
            state[p] = sp * jnp.where(row_lo, elast[h0:h0 + 1, :], elast[h1:h1 + 1, :]) + _dot_tn(
                (xp * wl).astype(BF16), bg)
            ybuf[rows, p * LANES:(p + 1) * LANES] = y + dfull_ref[:, p * LANES:(p + 1) * LANES] * xp
    for p in range(PAIRS):
        s_ref[0, 2 * p] = state[p, :SSD_HD, :]
        s_ref[0, 2 * p + 1] = state[p, SSD_HD:, :]
    return _ssd_gate_norm(ybuf[...], z_ref[...], ng_ref[...])


def _ssd_gate_norm(y, z, ng):
    gated = y * _silu(z)
    gw = D_SSD // SSD_GROUPS
    outs = []
    for g in range(SSD_GROUPS):
        outs.append(_rms_rows(gated[:, g * gw:(g + 1) * gw], ng[:, g * gw:(g + 1) * gw]))
    return jnp.concatenate(outs, axis=1)


SSD_PROMPT_PARAMS = ("ssd_conv_w", "ssd_conv_b", "ssd_dt_bias", "ssd_a_log", "ssd_d_full", "ssd_norm_g")


def _ssd_prompt_kernel(z_ref, xbc_ref, dt_ref, cw_ref, cb_ref, dtb_ref, alog_ref, dfull_ref, ng_ref,
                       tril_ref, y_ref, conv_ref, s_ref, xbuf, state, ybuf):
    c = pl.program_id(1)

    @pl.when(jnp.logical_and(pl.program_id(0) == 0, c == 0))
    def _():
        xbuf[...] = jnp.zeros_like(xbuf)
        state[...] = jnp.zeros_like(state)

    y_ref[...] = _ssd_tile(z_ref, xbc_ref, dt_ref, cw_ref, cb_ref, dtb_ref, alog_ref, dfull_ref, ng_ref,
                           tril_ref, conv_ref, s_ref, xbuf, state, ybuf, c == 0).astype(y_ref.dtype)


def _ssd_prompt(proj, bsz, seqlen, w, layer, tabs, tr):
    nc = seqlen // tr
    row = lambda b, c: b * nc + c
    const2 = lambda shape: pl.BlockSpec(shape, lambda b, c: (0, 0))
    return pl.pallas_call(
        _ssd_prompt_kernel,
        grid=(bsz, nc),
        in_specs=[
            pl.BlockSpec((tr, D_SSD), lambda b, c: (row(b, c), (COL_Z - PROMPT_COL0) // D_SSD)),
            pl.BlockSpec((tr, SSD_CONV_DIM), lambda b, c: (row(b, c), (COL_XBC - PROMPT_COL0) // SSD_CONV_DIM)),
            pl.BlockSpec((tr, LANES), lambda b, c: (row(b, c), (COL_DT - PROMPT_COL0) // LANES)),
        ] + [_lspec(w[n], layer) for n in SSD_PROMPT_PARAMS] + [const2((CHUNK, CHUNK))],
        out_specs=[
            pl.BlockSpec((tr, D_SSD), lambda b, c: (row(b, c), 0)),
            pl.BlockSpec((1, CONV_W - 1, SSD_CONV_DIM), lambda b, c: (b, 0, 0)),
            pl.BlockSpec((1, SSD_HEADS, SSD_HD, SSD_DSTATE), lambda b, c: (b, 0, 0, 0)),
        ],
        out_shape=[
            jax.ShapeDtypeStruct((bsz * seqlen, D_SSD), BF16),
            jax.ShapeDtypeStruct((bsz, CONV_W - 1, SSD_CONV_DIM), F32),
            jax.ShapeDtypeStruct((bsz, SSD_HEADS, SSD_HD, SSD_DSTATE), F32),
        ],
        scratch_shapes=[
            pltpu.VMEM((SUBLANES, SSD_CONV_DIM), F32),
            pltpu.VMEM((PAIRS, 2 * SSD_HD, SSD_DSTATE), F32),
            pltpu.VMEM((tr, D_SSD), F32),
        ],
        compiler_params=_cparams("arbitrary", "arbitrary"),
        name="ssd_prompt",
    )(proj, proj, proj, *[w[n] for n in SSD_PROMPT_PARAMS], tabs["tril"])


FF_TILES = ((0, 1536), (1536, 1280))
FFN_PARAMS = ("w_out", "norm_ffn_g", "ffn_w_up", "ffn_conv_w", "ffn_conv_b", "ffn_w_down")


def _ffn_body(x, ya, yb, yc, wo_ref, g_ref, wup_ref, cw_ref, cb_ref, wdn_ref, fg_ref, conv_fn, final_norm):
    x1 = x + _dot(ya, wo_ref[0:D_LRU, :])
    x1 = x1 + _dot(yb, wo_ref[D_LRU:D_LRU + D_RET, :])
    x1 = x1 + _dot(yc, wo_ref[D_LRU + D_RET:, :])
    hn = _rms_rows(x1, g_ref[...]).astype(BF16)
    acc = None
    for f0, fw in FF_TILES:
        gate = _dot(hn, wup_ref[:, f0:f0 + fw])
        up = _dot(hn, wup_ref[:, D_FF + f0:D_FF + f0 + fw])
        taps = [cw_ref[k:k + 1, f0:f0 + fw] for k in range(FFN_CONV_W)]
        conv = conv_fn(gate, f0, fw, taps, cb_ref[:, f0:f0 + fw])
        act = (_gelu(conv) * up).astype(BF16)
        d = _dot(act, wdn_ref[f0:f0 + fw, :])
        acc = d if acc is None else acc + d
    acc = x1 + acc
    if final_norm:
        acc = _rms_rows(acc, fg_ref[...])
    return acc


def _ffn_decode_kernel(x_ref, ya_ref, yb_ref, yc_ref, wo_ref, g_ref, wup_ref, cw_ref, cb_ref, wdn_ref, fg_ref,
                       st_ref, o_ref, conv_ref, *, final_norm):
    def conv_fn(gate, f0, fw, taps, cb):
        s0 = st_ref[:, f0:f0 + fw]
        s1 = st_ref[:, D_FF + f0:D_FF + f0 + fw]
        conv_ref[:, f0:f0 + fw] = s1
        conv_ref[:, D_FF + f0:D_FF + f0 + fw] = gate
        return cb + taps[0] * s0 + taps[1] * s1 + taps[2] * gate

    o_ref[...] = _ffn_body(x_ref[...], ya_ref[...], yb_ref[...], yc_ref[...], wo_ref, g_ref, wup_ref,
                           cw_ref, cb_ref, wdn_ref, fg_ref, conv_fn, final_norm)


def _ffn_prompt_kernel(x_ref, ya_ref, yb_ref, yc_ref, wo_ref, g_ref, wup_ref, cw_ref, cb_ref, wdn_ref, fg_ref,
                       o_ref, fconv_ref, halo, *, final_norm):
    tm = x_ref.shape[0]

    @pl.when(pl.program_id(1) == 0)
    def _():
        halo[...] = jnp.zeros_like(halo)

    def conv_fn(gate, f0, fw, taps, cb):
        conv = _causal_conv(gate, halo[:, f0:f0 + fw], taps, cb)
        tail = gate[tm - SUBLANES:, :]
        halo[:, f0:f0 + fw] = tail
        fconv_ref[0, :, f0:f0 + fw] = tail[SUBLANES - (FFN_CONV_W - 1):, :]
        return conv

    o_ref[...] = _ffn_body(x_ref[...], ya_ref[...], yb_ref[...], yc_ref[...], wo_ref, g_ref, wup_ref,
                           cw_ref, cb_ref, wdn_ref, fg_ref, conv_fn, final_norm)


def _ffn_decode(x2d, ya, yb, yc, w, layer, fg, st, *, final_norm):
    t = x2d.shape[0]
    rowmap = lambda i: (0, 0)
    blk = lambda width: pl.BlockSpec((t, width), rowmap)
    return pl.pallas_call(
        functools.partial(_ffn_decode_kernel, final_norm=final_norm),
        grid=(1,),
        in_specs=[blk(D_MODEL), blk(D_LRU), blk(D_RET), blk(D_SSD)]
        + [_lspec(w[n], layer) for n in FFN_PARAMS]
        + [pl.BlockSpec((1, D_MODEL), rowmap), blk(2 * D_FF)],
        out_specs=[blk(D_MODEL), blk(2 * D_FF)],
        out_shape=[jax.ShapeDtypeStruct((t, D_MODEL), F32), jax.ShapeDtypeStruct((t, 2 * D_FF), F32)],
        compiler_params=_cparams("arbitrary"),
        name="ffn_decode",
    )(x2d, ya, yb, yc, *[w[n] for n in FFN_PARAMS], fg, st)


def _ffn_prompt(x2d, ya, yb, yc, w, layer, fg, *, bsz, seqlen, tm, final_norm):
    nt = seqlen // tm
    rows = lambda width: pl.BlockSpec((tm, width), lambda b, i: (b * nt + i, 0))
    return pl.pallas_call(
        functools.partial(_ffn_prompt_kernel, final_norm=final_norm),
        grid=(bsz, nt),
        in_specs=[rows(D_MODEL), rows(D_LRU), rows(D_RET), rows(D_SSD)]
        + [_lspec(w[n], layer) for n in FFN_PARAMS] + [pl.BlockSpec((1, D_MODEL), lambda b, i: (0, 0))],
        out_specs=[rows(D_MODEL), pl.BlockSpec((1, FFN_CONV_W - 1, D_FF), lambda b, i: (b, 0, 0))],
        out_shape=[
            jax.ShapeDtypeStruct((bsz * seqlen, D_MODEL), F32),
            jax.ShapeDtypeStruct((bsz, FFN_CONV_W - 1, D_FF), F32),
        ],
        scratch_shapes=[pltpu.VMEM((SUBLANES, D_FF), F32)],
        compiler_params=_cparams("arbitrary", "arbitrary"),
        name="ffn_prompt",
    )(x2d, ya, yb, yc, *[w[n] for n in FFN_PARAMS], fg)


DEC_TILE = 16


MIX_DECODE_INPUTS = 35


def _mix_decode_kernel(*refs, n_alias):
    _mix_decode_body(*refs[:MIX_DECODE_INPUTS], *refs[MIX_DECODE_INPUTS + n_alias:])


def _mix_decode_body(lx_ref, lg_ref, q_ref, k_ref, v_ref, rg_ref, z_ref, xbc_ref, dt_ref,
                       lconv_ref, lh_ref, rs_ref, sconv_ref, ss_ref,
                       lcw_ref, lcb_ref, wr_ref, br_ref, wi_ref, bi_ref, lam_ref,
                       cos_ref, sin_ref, cdec_ref, rng_ref,
                       scw_ref, scb_ref, dtb_ref, alog_ref, dfull_ref, sng_ref,
                       eye_ref, hexp_ref, hexp_tile_ref, rsel_ref,
                       ya_ref, yb_ref, yc_ref, lconv_o, lh_o, rs_o, sconv_o, ss_o,
                       obuf, ybuf):
    x = lx_ref[...]
    st = lconv_ref[...]
    xc = lcb_ref[...] + lcw_ref[0:1, :] * st[:, 0:D_LRU] + lcw_ref[1:2, :] * st[:, D_LRU:2 * D_LRU]
    xc = xc + lcw_ref[2:3, :] * st[:, 2 * D_LRU:] + lcw_ref[3:4, :] * x
    lconv_o[:, 0:2 * D_LRU] = st[:, D_LRU:]
    lconv_o[:, 2 * D_LRU:] = x
    a, b = _lru_gates(xc, wr_ref, br_ref, wi_ref, bi_ref, lam_ref)
    h = b + a * lh_ref[...]
    lh_o[...] = h
    ya_ref[...] = (h * _gelu(lg_ref[...])).astype(ya_ref.dtype)

    cos = cos_ref[...]
    sin = sin_ref[...]
    q = _rotary_split(q_ref[...], cos, sin)
    k = _rotary_split(k_ref[...], cos, sin) * (RET_DK ** -0.5)
    v = v_ref[...]

    xin = xbc_ref[...]
    sst = sconv_ref[...]
    n = SSD_CONV_DIM
    xc = scb_ref[...] + scw_ref[0:1, :] * sst[:, 0:n] + scw_ref[1:2, :] * sst[:, n:2 * n]
    xc = xc + scw_ref[2:3, :] * sst[:, 2 * n:] + scw_ref[3:4, :] * xin
    sconv_o[:, 0:2 * n] = sst[:, n:]
    sconv_o[:, 2 * n:] = xin
    xc = _silu(xc)
    xs = xc[:, :D_SSD]
    bm = xc[:, D_SSD:D_SSD + SSD_BC]
    cm_b = xc[:, D_SSD + SSD_BC:].astype(BF16)
    dt = _softplus(dt_ref[...] + dtb_ref[...])
    dt_full = _dot_exact(dt, hexp_ref[...])
    eda_tiles = _dot_exact(jnp.exp(dt * (-jnp.exp(alog_ref[...]))), hexp_tile_ref[...])

    cols_src = jnp.concatenate([xs * dt_full, k], axis=1)
    cols_src = jnp.concatenate(
        [cols_src, jnp.zeros((LANES - DEC_TILE, cols_src.shape[1]), F32)], axis=0).astype(BF16)
    eye = eye_ref[...]
    cols_t = jnp.concatenate(
        [_dot_nt(eye, cols_src[:, j * LANES:(j + 1) * LANES]) for j in range(cols_src.shape[1] // LANES)],
        axis=0).astype(BF16)
    k_row0 = D_SSD

    head_q = _head_of_qk_lane((DEC_TILE, D_QK), 1)
    q_heads = jnp.concatenate([jnp.where(head_q == hh, q, 0.0) for hh in range(RET_HEADS)],
                              axis=0).astype(BF16)
    half = RET_DK // 2
    for r in range(DEC_TILE):
        colb = _dot(cols_t, rsel_ref[r])
        pieces = []
        for hh in range(SSD_HEADS):
            g = hh // HEADS_PER_GROUP
            e = jnp.broadcast_to(eda_tiles[r:r + 1, hh * LANES:(hh + 1) * LANES], (SSD_HD, SSD_DSTATE))
            brow = bm[r:r + 1, g * SSD_DSTATE:(g + 1) * SSD_DSTATE]
            s_new = ss_ref[0, r, hh] * e + colb[hh * SSD_HD:(hh + 1) * SSD_HD, :] * brow
            ss_o[0, r, hh] = s_new
            pieces.append(s_new.astype(BF16))
        gw = D_SSD // SSD_GROUPS
        for g in range(SSD_GROUPS):
            sg = jnp.concatenate(pieces[g * HEADS_PER_GROUP:(g + 1) * HEADS_PER_GROUP], axis=0)
            yg = _dot_nt(cm_b[:, g * SSD_DSTATE:(g + 1) * SSD_DSTATE], sg)
            ybuf[r:r + 1, g * gw:(g + 1) * gw] = yg[r:r + 1, :]
        pieces = []
        for part in range(2):
            for hh in range(RET_HEADS):
                r0 = k_row0 + part * LANES + hh * half
                vrow = v[r:r + 1, hh * RET_DV:(hh + 1) * RET_DV]
                s_old = rs_ref[0, r, hh, part * half:(part + 1) * half, :]
                s_new = s_old * cdec_ref[hh] + colb[r0:r0 + half, :] * vrow
                rs_o[0, r, hh, part * half:(part + 1) * half, :] = s_new
                pieces.append(s_new.astype(BF16))
        s_perm = jnp.concatenate(pieces, axis=0)
        res = _dot(q_heads, s_perm)
        for hh in range(RET_HEADS):
            obuf[r:r + 1, hh * RET_DV:(hh + 1) * RET_DV] = res[hh * DEC_TILE + r:hh * DEC_TILE + r + 1, :]

    o = obuf[...]
    rg = rg_ref[...]
    for hh in range(RET_HEADS):
        oh = _rms_rows(o[:, hh * RET_DV:(hh + 1) * RET_DV], rng_ref[hh:hh + 1, :])
        yb_ref[:, hh * RET_DV:(hh + 1) * RET_DV] = (
            _silu(rg[:, hh * RET_DV:(hh + 1) * RET_DV]) * oh).astype(yb_ref.dtype)
    y = ybuf[...] + dfull_ref[...] * xs
    yc_ref[...] = _ssd_gate_norm(y, z_ref[...], sng_ref[...]).astype(yc_ref.dtype)


SSD_DECODE_PARAMS = SSD_PROMPT_PARAMS


def _mix_decode(proj, states, layer, w, tabs, prev_full):
    nb = proj.shape[0]
    depth = states[2].shape[0]
    grid = (nb // DEC_TILE,)
    col = lambda width, c0: pl.BlockSpec((DEC_TILE, width), lambda i: (i, c0 // width))
    const2 = lambda shape: pl.BlockSpec(shape, lambda i: (0, 0))
    const3 = lambda shape: pl.BlockSpec(shape, lambda i: (0, 0, 0))
    lconv, lh, rs, sconv, ss = states
    ret_blk = pl.BlockSpec((1, DEC_TILE, RET_HEADS, RET_DK, RET_DV), lambda i: (layer, i, 0, 0, 0))
    ssd_blk = pl.BlockSpec((1, DEC_TILE, SSD_HEADS, SSD_HD, SSD_DSTATE), lambda i: (layer, i, 0, 0, 0))
    in_specs = [
        col(D_LRU, COL_LRU_X), col(D_LRU, COL_LRU_G), col(D_QK, COL_Q), col(D_QK, COL_K),
        col(D_RET, COL_V), col(D_RET, COL_RG), col(D_SSD, COL_Z), col(SSD_CONV_DIM, COL_XBC),
        col(LANES, COL_DT),
        pl.BlockSpec((DEC_TILE, (CONV_W - 1) * D_LRU), lambda i: (i, 0)),
        pl.BlockSpec((DEC_TILE, D_LRU), lambda i: (i, 0)),
        ret_blk,
        pl.BlockSpec((DEC_TILE, (CONV_W - 1) * SSD_CONV_DIM), lambda i: (i, 0)),
        ssd_blk,
    ] + [_lspec(w[n], layer) for n in LRU_PARAMS] + [
        const2((1, LANES)), const2((1, LANES)), const3((RET_HEADS, 1, 1)), _lspec(w["ret_norm_g"], layer),
    ] + [_lspec(w[n], layer) for n in SSD_DECODE_PARAMS] + [
        const2((LANES, LANES)), const2((LANES, D_SSD)), const2((LANES, SSD_HEADS * LANES)),
        const3((DEC_TILE, LANES, LANES)),
    ]
    args = [proj] * 9 + [lconv, lh, rs, sconv, ss] + [w[n] for n in LRU_PARAMS] + [
        tabs["cos_s"], tabs["sin_s"], tabs["cdec1"], w["ret_norm_g"]] + [
        w[n] for n in SSD_DECODE_PARAMS] + [tabs["eye"], tabs["hexp"], tabs["hexp_tile"], tabs["rsel"]]
    assert len(args) == MIX_DECODE_INPUTS
    aliases = {}
    if prev_full is not None:
        in_specs += [pl.BlockSpec(memory_space=pl.ANY)] * 2
        args += list(prev_full)
        aliases = {MIX_DECODE_INPUTS: 5, MIX_DECODE_INPUTS + 1: 7}
    out_specs = [
        pl.BlockSpec((DEC_TILE, D_LRU), lambda i: (i, 0)),
        pl.BlockSpec((DEC_TILE, D_RET), lambda i: (i, 0)),
        pl.BlockSpec((DEC_TILE, D_SSD), lambda i: (i, 0)),
        pl.BlockSpec((DEC_TILE, (CONV_W - 1) * D_LRU), lambda i: (i, 0)),
        pl.BlockSpec((DEC_TILE, D_LRU), lambda i: (i, 0)),
        ret_blk,
        pl.BlockSpec((DEC_TILE, (CONV_W - 1) * SSD_CONV_DIM), lambda i: (i, 0)),
        ssd_blk,
    ]
    out_shape = [
        jax.ShapeDtypeStruct((nb, D_LRU), BF16),
        jax.ShapeDtypeStruct((nb, D_RET), BF16),
        jax.ShapeDtypeStruct((nb, D_SSD), BF16),
        jax.ShapeDtypeStruct((nb, (CONV_W - 1) * D_LRU), F32),
        jax.ShapeDtypeStruct((nb, D_LRU), F32),
        jax.ShapeDtypeStruct((depth, nb, RET_HEADS, RET_DK, RET_DV), F32),
        jax.ShapeDtypeStruct((nb, (CONV_W - 1) * SSD_CONV_DIM), F32),
        jax.ShapeDtypeStruct((depth, nb, SSD_HEADS, SSD_HD, SSD_DSTATE), F32),
    ]
    return pl.pallas_call(
        functools.partial(_mix_decode_kernel, n_alias=len(aliases)),
        grid=grid, in_specs=in_specs, out_specs=out_specs, out_shape=out_shape,
        input_output_aliases=aliases,
        scratch_shapes=[pltpu.VMEM((DEC_TILE, D_RET), F32), pltpu.VMEM((DEC_TILE, D_SSD), F32)],
        compiler_params=_cparams("arbitrary"),
        name="mix_decode",
    )(*args)


def _qk_perm():
    half = RET_DK // 2
    idx = np.empty((D_QK,), np.int32)
    for part in range(2):
        for h in range(RET_HEADS):
            for j in range(half):
                idx[part * LANES + h * half + j] = h * RET_DK + part * half + j
    return idx


def _block_diag(wh):
    nh, hd, _ = wh.shape
    bands = [jnp.pad(wh[h], ((0, 0), (h * hd, (nh - 1 - h) * hd))) for h in range(nh)]
    return jnp.concatenate(bands, axis=0)


def _prep_weights(p):
    perm = _qk_perm()
    w_in = p["w_in"]
    depth = w_in.shape[0]
    w_qk = jnp.concatenate([w_in[:, :, COL_Q:COL_Q + D_QK][:, :, perm],
                            w_in[:, :, COL_K:COL_K + D_QK][:, :, perm]], axis=2).astype(BF16)
    n_dt = w_in.shape[2] - COL_DT
    w_dt = jnp.concatenate([w_in[:, :, COL_DT:], jnp.zeros((depth, D_MODEL, LANES - n_dt), w_in.dtype)],
                           axis=2).astype(BF16)
    row = lambda v: v[:, None, :]
    pad_lanes = lambda v: row(jnp.concatenate([v, jnp.zeros((depth, LANES - v.shape[1]), v.dtype)], axis=1))
    return {
        "norm_mix_g": row(p["norm_mix_g"]),
        "w_in": w_in.astype(BF16),
        "w_qk": w_qk,
        "w_dt": w_dt,
        "lru_conv_w": p["lru_conv_w"],
        "lru_conv_b": row(p["lru_conv_b"]),
        "lru_wr": jax.vmap(_block_diag)(p["lru_wr"]).astype(BF16),
        "lru_br": row(p["lru_br"]),
        "lru_wi": jax.vmap(_block_diag)(p["lru_wi"]).astype(BF16),
        "lru_bi": row(p["lru_bi"]),
        "lru_lambda": row(p["lru_lambda"]),
        "ret_norm_g": p["ret_norm_g"],
        "ssd_conv_w": p["ssd_conv_w"],
        "ssd_conv_b": row(p["ssd_conv_b"]),
        "ssd_dt_bias": pad_lanes(p["ssd_dt_bias"]),
        "ssd_a_log": pad_lanes(p["ssd_a_log"]),
        "ssd_d_full": row(jnp.repeat(p["ssd_d"], SSD_HD, axis=1)),
        "ssd_norm_g": row(p["ssd_norm_g"]),
        "w_out": p["w_out"].astype(BF16),
        "norm_ffn_g": row(p["norm_ffn_g"]),
        "ffn_w_up": p["ffn_w_up"].astype(BF16),
        "ffn_conv_w": p["ffn_conv_w"],
        "ffn_conv_b": row(p["ffn_conv_b"]),
        "ffn_w_down": p["ffn_w_down"].astype(BF16),
    }


def _tables(seqlen, past_len):
    half = RET_DK // 2
    freqs = ROPE_BASE ** (-jnp.arange(half, dtype=F32) / half)

    def cs(pos):
        ang = pos.astype(F32)[:, None] * freqs[None, :]
        return jnp.tile(jnp.cos(ang), (1, RET_HEADS)), jnp.tile(jnp.sin(ang), (1, RET_HEADS))

    cos_p, sin_p = cs(jnp.arange(seqlen, dtype=jnp.int32))
    cos_s, sin_s = cs(past_len + jnp.arange(1, dtype=jnp.int32))
    log_g = jnp.log1p(-jnp.exp2(-5.0 - jnp.arange(RET_HEADS, dtype=F32)))
    idx = jnp.arange(CHUNK, dtype=F32)
    diff = idx[:, None] - idx[None, :]
    decay = jnp.exp(jnp.where(diff[None] >= 0, diff[None] * log_g[:, None, None], -jnp.inf))
    q_dec = jnp.exp((idx[None, :] + 1.0) * log_g[:, None])
    k_dec = jnp.exp((CHUNK - 1.0 - idx[None, :]) * log_g[:, None])
    lanes_of = lambda t: jnp.tile(jnp.repeat(t.T, half, axis=1), (1, 2))
    hexp = (jnp.arange(LANES)[:, None] == (jnp.arange(D_SSD)[None, :] // SSD_HD)).astype(F32)
    return {
        "cos_p": cos_p, "sin_p": sin_p, "cos_s": cos_s, "sin_s": sin_s,
        "decay": decay, "qdec": lanes_of(q_dec), "kdec": lanes_of(k_dec),
        "cdec": jnp.repeat(jnp.exp(CHUNK * log_g), RET_DV)[None, :],
        "cdec1": jnp.exp(log_g)[:, None, None],
        "tril": jnp.tril(jnp.ones((CHUNK, CHUNK), F32)),
        "eye": jnp.eye(LANES, dtype=BF16),
        "hexp": hexp,
        "hexp_tile": (jnp.arange(LANES)[:, None] == (jnp.arange(SSD_HEADS * LANES)[None, :] // LANES)).astype(F32),
        "rsel": (jnp.arange(LANES)[None, :, None] == jnp.arange(DEC_TILE)[:, None, None]).astype(BF16)
        * jnp.ones((1, 1, LANES), BF16),
    }


TM_PROMPT = 512
TR_MIX = 512
TR_RET = 1024


def _prompt_stack(x, w, tabs, fg, tm=TM_PROMPT, tr=TR_MIX, tr_ret=TR_RET):
    bsz, seqlen, _ = x.shape
    depth = w["w_in"].shape[0]
    x2d = x.reshape(bsz * seqlen, D_MODEL)
    outs = [[] for _ in range(6)]
    for l in range(depth):
        proj, ya, lconv, lh = _inproj_lru_prompt(x2d, w, l, bsz=bsz, seqlen=seqlen, tm=tm)
        yb, rs = _ret_prompt(proj, bsz, seqlen, w, l, tabs, tr_ret)
        yc, sconv, ss = _ssd_prompt(proj, bsz, seqlen, w, l, tabs, tr)
        x2d, fconv = _ffn_prompt(x2d, ya, yb, yc, w, l, fg, bsz=bsz, seqlen=seqlen, tm=tm,
                                 final_norm=(l == depth - 1))
        for lst, val in zip(outs, (lconv, lh[:, 0, :], rs, sconv, ss, fconv)):
            lst.append(val)
    return x2d.reshape(bsz, seqlen, D_MODEL), [jnp.stack(o, axis=0) for o in outs]


def _sample_stack(x, states, w, tabs, fg):
    nb = x.shape[0]
    depth = w["w_in"].shape[0]
    x2d = x.reshape(nb, D_MODEL)
    st_lconv, st_lh, st_rs, st_sconv, st_ss, st_fconv = states
    lconv_in = st_lconv.reshape(depth, nb, (CONV_W - 1) * D_LRU)
    sconv_in = st_sconv.reshape(depth, nb, (CONV_W - 1) * SSD_CONV_DIM)
    fconv_in = st_fconv.reshape(depth, nb, (FFN_CONV_W - 1) * D_FF)
    outs = [[] for _ in range(4)]
    full = None
    for l in range(depth):
        proj = _inproj(x2d, w, l, tm=nb)
        ya, yb, yc, lconv, lh, rs_full, sconv, ss_full = _mix_decode(
            proj, (lconv_in[l], st_lh[l], st_rs, sconv_in[l], st_ss), l, w, tabs, full)
        full = (rs_full, ss_full)
        x2d, fconv = _ffn_decode(x2d, ya, yb, yc, w, l, fg, fconv_in[l], final_norm=(l == depth - 1))
        for lst, val in zip(outs, (lconv, lh, sconv, fconv)):
            lst.append(val)
    lconv, lh, sconv, fconv = [jnp.stack(o, axis=0) for o in outs]
    new_states = [lconv.reshape(depth, nb, CONV_W - 1, D_LRU), lh, full[0],
                  sconv.reshape(depth, nb, CONV_W - 1, SSD_CONV_DIM), full[1],
                  fconv.reshape(depth, nb, FFN_CONV_W - 1, D_FF)]
    return x2d.reshape(nb, 1, D_MODEL), new_states


def kernel(x_prompt, x_sample, state_lru_conv, state_lru_h, state_ret, state_ssd_conv, state_ssd, state_ffn_conv,
           norm_mix_g, w_in, lru_conv_w, lru_conv_b, lru_wr, lru_br, lru_wi, lru_bi, lru_lambda,
           ret_norm_g, ssd_conv_w, ssd_conv_b, ssd_dt_bias, ssd_a_log, ssd_d, ssd_norm_g,
           w_out, norm_ffn_g, ffn_w_up, ffn_conv_w, ffn_conv_b, ffn_w_down, norm_final_g):
    p = {"norm_mix_g": norm_mix_g, "w_in": w_in, "lru_conv_w": lru_conv_w, "lru_conv_b": lru_conv_b,
         "lru_wr": lru_wr, "lru_br": lru_br, "lru_wi": lru_wi, "lru_bi": lru_bi, "lru_lambda": lru_lambda,
         "ret_norm_g": ret_norm_g, "ssd_conv_w": ssd_conv_w, "ssd_conv_b": ssd_conv_b,
         "ssd_dt_bias": ssd_dt_bias, "ssd_a_log": ssd_a_log, "ssd_d": ssd_d, "ssd_norm_g": ssd_norm_g,
         "w_out": w_out, "norm_ffn_g": norm_ffn_g, "ffn_w_up": ffn_w_up, "ffn_conv_w": ffn_conv_w,
         "ffn_conv_b": ffn_conv_b, "ffn_w_down": ffn_w_down}
    weights = _prep_weights(p)
    past_len = 16384
    tabs = _tables(x_prompt.shape[1], past_len)
    fg = norm_final_g[None, :]

    y_p, ns_p = _prompt_stack(x_prompt, weights, tabs, fg)
    y_s, ns_s = _sample_stack(
        x_sample, (state_lru_conv, state_lru_h, state_ret, state_ssd_conv, state_ssd, state_ffn_conv),
        weights, tabs, fg)
    lru_conv_p, lru_h_p, ret_p, ssd_conv_p, ssd_p, ffn_conv_p = ns_p
    lru_conv_s, lru_h_s, ret_s, ssd_conv_s, ssd_s, ffn_conv_s = ns_s
    return (y_p, y_s, lru_conv_p, lru_conv_s, lru_h_p, lru_h_s, ret_p, ret_s,
            ssd_conv_p, ssd_conv_s, ssd_p, ssd_s, ffn_conv_p, ffn_conv_s)
```

```python
import functools

import jax
import jax.numpy as jnp
import numpy as np
from jax import lax
from jax.experimental import pallas as pl
from jax.experimental.pallas import tpu as pltpu

F32 = jnp.float32
BF16 = jnp.bfloat16

D_MODEL = 1024
DEPTH = 4
D_LRU = 512
LRU_HEADS = 8
LRU_HD = D_LRU // LRU_HEADS
LRU_C = 8.0
CONV_W = 4
RET_HEADS = 4
RET_DK = 64
RET_DV = 128
D_RET = RET_HEADS * RET_DV
D_QK = RET_HEADS * RET_DK
ROPE_BASE = 10000.0
SSD_HD = 64
D_SSD = 512
SSD_HEADS = 8
SSD_GROUPS = 2
SSD_DSTATE = 128
SSD_BC = SSD_GROUPS * SSD_DSTATE
SSD_CONV_DIM = D_SSD + 2 * SSD_BC
D_MIX = D_LRU + D_RET + D_SSD
D_FF = 2816
FFN_CONV_W = 3
EPS = 1e-6
CHUNK = 128

COL_LRU_X = 0
COL_LRU_G = 512
COL_Q = 1024
COL_K = 1280
COL_V = 1536
COL_RG = 2048
COL_Z = 2560
COL_XBC = 3072
COL_DT = 4096
D_IN_PAD = 4224
PROMPT_COL0 = COL_Q
D_IN_PROMPT = D_IN_PAD - PROMPT_COL0

LANES = 128
SUBLANES = 8
VMEM_LIMIT_BYTES = 56 * 1024 * 1024


def _cparams(*sem):
    return pltpu.CompilerParams(dimension_semantics=sem, vmem_limit_bytes=VMEM_LIMIT_BYTES)


def _dot(a, b):
    return jnp.dot(a, b, preferred_element_type=F32)


def _dot_nt(a, b):
    return lax.dot_general(a, b, (((1,), (1,)), ((), ())), preferred_element_type=F32)


def _dot_tn(a, b):
    return lax.dot_general(a, b, (((0,), (0,)), ((), ())), preferred_element_type=F32)


def _dot_exact(a, b):
    return jnp.dot(a, b, preferred_element_type=F32, precision=lax.Precision.HIGHEST)


def _dot_nt_exact(a, b):
    return lax.dot_general(a, b, (((1,), (1,)), ((), ())), preferred_element_type=F32,
                           precision=lax.Precision.HIGHEST)


def _rms_rows(x, g):
    return x * lax.rsqrt(jnp.mean(x * x, axis=-1, keepdims=True) + EPS) * g


def _sigmoid(x):
    return 1.0 / (1.0 + jnp.exp(-x))


def _silu(x):
    return x * _sigmoid(x)


def _gelu(x):
    return jax.nn.gelu(x, approximate=True)


def _softplus(x):
    return jnp.maximum(x, 0.0) + jnp.log1p(jnp.exp(-jnp.abs(x)))


IN_COL_CHUNK = 512


def _lspec(arr, layer):
    _, r, c = arr.shape
    return pl.BlockSpec((None, r, c), lambda *_: (layer, 0, 0), pipeline_mode=pl.Buffered(1))


def _inproj_kernel(x_ref, g_ref, wm_ref, wqk_ref, wdt_ref, o_ref):
    hn = _rms_rows(x_ref[...], g_ref[...]).astype(BF16)
    for c0 in range(0, COL_DT, IN_COL_CHUNK):
        if c0 == COL_Q:
            o_ref[:, c0:c0 + IN_COL_CHUNK] = _dot(hn, wqk_ref[...])
        else:
            o_ref[:, c0:c0 + IN_COL_CHUNK] = _dot(hn, wm_ref[:, c0:c0 + IN_COL_CHUNK])
    o_ref[:, COL_DT:] = _dot(hn, wdt_ref[...])


def _inproj(x2d, w, layer, tm):
    t = x2d.shape[0]
    return pl.pallas_call(
        _inproj_kernel,
        grid=(t // tm,),
        in_specs=[
            pl.BlockSpec((tm, D_MODEL), lambda i: (i, 0)),
            _lspec(w["norm_mix_g"], layer),
            _lspec(w["w_in"], layer),
            _lspec(w["w_qk"], layer),
            _lspec(w["w_dt"], layer),
        ],
        out_specs=pl.BlockSpec((tm, D_IN_PAD), lambda i: (i, 0)),
        out_shape=jax.ShapeDtypeStruct((t, D_IN_PAD), F32),
        compiler_params=_cparams("arbitrary"),
        name="inproj",
    )(x2d, w["norm_mix_g"], w["w_in"], w["w_qk"], w["w_dt"])


def _lru_gate_pre(xc, wr_ref, br_ref, wi_ref, bi_ref, lam_ref):
    xcb = xc.astype(BF16)
    return (_dot(xcb, wr_ref[...]) + br_ref[...], _dot(xcb, wi_ref[...]) + bi_ref[...],
            _softplus(-lam_ref[...]))


def _lru_gate_post(xc, r_pre, i_pre, sp):
    log_a = (-LRU_C * _sigmoid(r_pre)) * sp
    a = jnp.exp(log_a)
    b = jnp.sqrt(-jnp.tanh(log_a) * (a * a + 1.0)) * (_sigmoid(i_pre) * xc)
    return a, b


def _lru_gates(xc, wr_ref, br_ref, wi_ref, bi_ref, lam_ref):
    return _lru_gate_post(xc, *_lru_gate_pre(xc, wr_ref, br_ref, wi_ref, bi_ref, lam_ref))


def _scan_rows(a, b, h_in):
    n, cols = a.shape
    groups = n // SUBLANES
    a = a.reshape(groups, SUBLANES, cols)
    b = b.reshape(groups, SUBLANES, cols)
    row = lax.broadcasted_iota(jnp.int32, a.shape, 1)
    d = 1
    while d < SUBLANES:
        keep = row >= d
        a_sh = jnp.where(keep, pltpu.roll(a, d, axis=1), 1.0)
        b_sh = jnp.where(keep, pltpu.roll(b, d, axis=1), 0.0)
        b = a * b_sh + b
        a = a * a_sh
        d *= 2
    hs = []
    h_prev = h_in
    for g in range(groups):
        hg = b[g] + a[g] * h_prev
        hs.append(hg)
        h_prev = hg[SUBLANES - 1:SUBLANES, :]
    return jnp.concatenate(hs, axis=0)


def _causal_conv(x, halo, taps, bias):
    n, cols = x.shape
    groups = n // SUBLANES
    xe = jnp.concatenate([halo, x], axis=0).reshape(groups + 1, SUBLANES, cols)
    row = lax.broadcasted_iota(jnp.int32, (groups, SUBLANES, cols), 1)
    y = bias + taps[-1] * x
    for s in range(1, len(taps)):
        r = pltpu.roll(xe, s, axis=1)
        shifted = jnp.where(row >= s, r[1:], r[:-1]).reshape(n, cols)
        y = y + taps[-1 - s] * shifted
    return y


def _rotary_split(x, cos, sin):
    xa, xb = x[:, :LANES], x[:, LANES:]
    return jnp.concatenate([xa * cos - xb * sin, xa * sin + xb * cos], axis=1)


def _head_of_qk_lane(shape, axis):
    lane = lax.broadcasted_iota(jnp.int32, shape, axis)
    return (lane % LANES) // (RET_DK // 2)


def _lru_back(x, gate, front, conv_ref, h_ref, xbuf, hcar, reset, valid):
    xc, pre, halo_old, h_old = front
    tl = x.shape[0]
    a, b = _lru_gate_post(xc, *pre)
    h = _scan_rows(a, b, jnp.where(reset, 0.0, h_old))
    y = h * _gelu(gate)
    tail = jnp.where(valid, x[tl - SUBLANES:, :], halo_old)
    xbuf[...] = tail
    conv_ref[0] = tail[SUBLANES - (CONV_W - 1):, :]
    hlast = jnp.where(valid, h[tl - 1:tl, :], h_old)
    hcar[...] = hlast
    h_ref[0] = hlast
    return y


LRU_PARAMS = ("lru_conv_w", "lru_conv_b", "lru_wr", "lru_br", "lru_wi", "lru_bi", "lru_lambda")


def _inproj_lru_kernel(x_ref, g_ref, wm_ref, wqk_ref, wdt_ref,
                       cw_ref, cb_ref, wr_ref, br_ref, wi_ref, bi_ref, lam_ref,
                       o_ref, y_ref, conv_ref, h_ref, ring, xbuf, hcar, *, tiles_per_seq):
    g = pl.program_id(0)

    @pl.when(g == 0)
    def _():
        xbuf[...] = jnp.zeros_like(xbuf)
        hcar[...] = jnp.zeros_like(hcar)
        ring[...] = jnp.zeros_like(ring)

    valid = g > 0
    reset = jnp.logical_and((g + tiles_per_seq - 1) % tiles_per_seq == 0, valid)
    lru_x = ring[:, 0:D_LRU]
    lru_gate = ring[:, D_LRU:]
    halo_old = xbuf[...]
    h_old = hcar[...]
    xc = _causal_conv(lru_x, jnp.where(reset, 0.0, halo_old), [cw_ref[k:k + 1, :] for k in range(CONV_W)],
                      cb_ref[...])
    front = (xc, _lru_gate_pre(xc, wr_ref, br_ref, wi_ref, bi_ref, lam_ref), halo_old, h_old)

    hn = _rms_rows(x_ref[...], g_ref[...]).astype(BF16)
    for c0 in range(0, COL_DT, IN_COL_CHUNK):
        o0 = c0 - PROMPT_COL0
        if c0 == COL_Q:
            o_ref[:, o0:o0 + IN_COL_CHUNK] = _dot(hn, wqk_ref[...])
        elif c0 < PROMPT_COL0:
            ring[:, c0:c0 + IN_COL_CHUNK] = _dot(hn, wm_ref[:, c0:c0 + IN_COL_CHUNK])
        else:
            o_ref[:, o0:o0 + IN_COL_CHUNK] = _dot(hn, wm_ref[:, c0:c0 + IN_COL_CHUNK])
    o_ref[:, COL_DT - PROMPT_COL0:] = _dot(hn, wdt_ref[...])

    y_ref[...] = _lru_back(lru_x, lru_gate, front, conv_ref, h_ref, xbuf, hcar, reset, valid).astype(
        y_ref.dtype)


def _inproj_lru_prompt(x2d, w, layer, *, bsz, seqlen, tm):
    tps = seqlen // tm
    n_tiles = bsz * tps
    proj_tile = lambda g: jnp.minimum(g, n_tiles - 1)
    lru_tile = lambda g: jnp.maximum(g - 1, 0)
    in_names = ("norm_mix_g", "w_in", "w_qk", "w_dt")
    return pl.pallas_call(
        functools.partial(_inproj_lru_kernel, tiles_per_seq=tps),
        grid=(n_tiles + 1,),
        in_specs=[pl.BlockSpec((tm, D_MODEL), lambda g: (proj_tile(g), 0))]
        + [_lspec(w[n], layer) for n in in_names] + [_lspec(w[n], layer) for n in LRU_PARAMS],
        out_specs=[
            pl.BlockSpec((tm, D_IN_PROMPT), lambda g: (proj_tile(g), 0)),
            pl.BlockSpec((tm, D_LRU), lambda g: (lru_tile(g), 0)),
            pl.BlockSpec((1, CONV_W - 1, D_LRU), lambda g: (lru_tile(g) // tps, 0, 0)),
            pl.BlockSpec((1, 1, D_LRU), lambda g: (lru_tile(g) // tps, 0, 0)),
        ],
        out_shape=[
            jax.ShapeDtypeStruct((bsz * seqlen, D_IN_PROMPT), F32),
            jax.ShapeDtypeStruct((bsz * seqlen, D_LRU), BF16),
            jax.ShapeDtypeStruct((bsz, CONV_W - 1, D_LRU), F32),
            jax.ShapeDtypeStruct((bsz, 1, D_LRU), F32),
        ],
        scratch_shapes=[
            pltpu.VMEM((tm, 2 * D_LRU), F32),
            pltpu.VMEM((SUBLANES, D_LRU), F32),
            pltpu.VMEM((1, D_LRU), F32),
        ],
        compiler_params=_cparams("arbitrary"),
        name="inproj_lru_prompt",
    )(x2d, *[w[n] for n in in_names], *[w[n] for n in LRU_PARAMS])


def _ret_prompt_kernel(q_ref, k_ref, v_ref, g_ref, cos_ref, sin_ref, decay_ref, qdec_ref, kdec_ref,
                       cdec_ref, ng_ref, y_ref, s_ref, state):
    c = pl.program_id(1)

    @pl.when(c == 0)
    def _():
        state[...] = jnp.zeros_like(state)

    head = _head_of_qk_lane((CHUNK, D_QK), 1)
    own_block = (_head_of_qk_lane((D_QK, D_RET), 0)
                 == lax.broadcasted_iota(jnp.int32, (D_QK, D_RET), 1) // RET_DV)
    for ci in range(q_ref.shape[0] // CHUNK):
        rows = slice(ci * CHUNK, (ci + 1) * CHUNK)
        cos = cos_ref[rows, :]
        sin = sin_ref[rows, :]
        q = _rotary_split(q_ref[rows, :], cos, sin)
        k = _rotary_split(k_ref[rows, :], cos, sin) * (RET_DK ** -0.5)
        kb = k.astype(BF16)
        vb = v_ref[rows, :].astype(BF16)
        s_old = state[...]
        o_inter = _dot((q * qdec_ref[...]).astype(BF16), s_old.astype(BF16))
        upd = _dot_tn((k * kdec_ref[...]).astype(BF16), vb)
        state[...] = s_old * cdec_ref[...] + jnp.where(own_block, upd, 0.0)
        for h in range(RET_HEADS):
            cols = slice(h * RET_DV, (h + 1) * RET_DV)
            scores = _dot_nt(jnp.where(head == h, q, 0.0).astype(BF16), kb) * decay_ref[h]
            o = _dot(scores.astype(BF16), vb[:, cols]) + o_inter[:, cols]
            o = _rms_rows(o, ng_ref[h:h + 1, :])
            y_ref[rows, cols] = (_silu(g_ref[rows, cols]) * o).astype(y_ref.dtype)
    half = RET_DK // 2
    for h in range(RET_HEADS):
        cols = slice(h * RET_DV, (h + 1) * RET_DV)
        s_ref[0, h, 0:half, :] = state[h * half:(h + 1) * half, cols]
        s_ref[0, h, half:RET_DK, :] = state[LANES + h * half:LANES + (h + 1) * half, cols]


def _ret_prompt(proj, bsz, seqlen, w, layer, tabs, tr):
    nc = seqlen // tr
    row = lambda b, c: b * nc + c
    const2 = lambda shape: pl.BlockSpec(shape, lambda b, c: (0, 0))
    const3 = lambda shape: pl.BlockSpec(shape, lambda b, c: (0, 0, 0))
    return pl.pallas_call(
        _ret_prompt_kernel,
        grid=(bsz, nc),
        in_specs=[
            pl.BlockSpec((tr, D_QK), lambda b, c: (row(b, c), (COL_Q - PROMPT_COL0) // D_QK)),
            pl.BlockSpec((tr, D_QK), lambda b, c: (row(b, c), (COL_K - PROMPT_COL0) // D_QK)),
            pl.BlockSpec((tr, D_RET), lambda b, c: (row(b, c), (COL_V - PROMPT_COL0) // D_RET)),
            pl.BlockSpec((tr, D_RET), lambda b, c: (row(b, c), (COL_RG - PROMPT_COL0) // D_RET)),
            pl.BlockSpec((tr, LANES), lambda b, c: (c, 0)),
            pl.BlockSpec((tr, LANES), lambda b, c: (c, 0)),
            const3((RET_HEADS, CHUNK, CHUNK)),
            const2((CHUNK, D_QK)),
            const2((CHUNK, D_QK)),
            const2((1, D_RET)),
            _lspec(w["ret_norm_g"], layer),
        ],
        out_specs=[
            pl.BlockSpec((tr, D_RET), lambda b, c: (row(b, c), 0)),
            pl.BlockSpec((1, RET_HEADS, RET_DK, RET_DV), lambda b, c: (b, 0, 0, 0)),
        ],
        out_shape=[
            jax.ShapeDtypeStruct((bsz * seqlen, D_RET), BF16),
            jax.ShapeDtypeStruct((bsz, RET_HEADS, RET_DK, RET_DV), F32),
        ],
        scratch_shapes=[pltpu.VMEM((D_QK, D_RET), F32)],
        compiler_params=_cparams("arbitrary", "arbitrary"),
        name="ret_prompt",
    )(proj, proj, proj, proj, tabs["cos_p"], tabs["sin_p"], tabs["decay"], tabs["qdec"],
      tabs["kdec"], tabs["cdec"], w["ret_norm_g"])


HEADS_PER_GROUP = SSD_HEADS // SSD_GROUPS
PAIRS = SSD_HEADS // 2


def _ssd_tile(z_ref, xbc_ref, dt_ref, cw_ref, cb_ref, dtb_ref, alog_ref, dfull_ref, ng_ref,
              tril_ref, conv_ref, s_ref, xbuf, state, ybuf, reset):
    tr = xbc_ref.shape[0]
    x = xbc_ref[...]
    halo_in = jnp.where(reset, 0.0, xbuf[...])
    xc = _causal_conv(x, halo_in, [cw_ref[k:k + 1, :] for k in range(CONV_W)], cb_ref[...])
    tail = x[tr - SUBLANES:, :]
    xbuf[...] = tail
    conv_ref[0] = tail[SUBLANES - (CONV_W - 1):, :]
    state[...] = jnp.where(reset, 0.0, state[...])
    xc = _silu(xc)
    xs_all = xc[:, :D_SSD]
    bm_all = xc[:, D_SSD:D_SSD + SSD_BC].astype(BF16)
    cm_all = xc[:, D_SSD + SSD_BC:].astype(BF16)

    dt_all = _softplus(dt_ref[...] + dtb_ref[...])
    da_all = dt_all * (-jnp.exp(alog_ref[...]))

    causal = (lax.broadcasted_iota(jnp.int32, (CHUNK, CHUNK), 0)
              >= lax.broadcasted_iota(jnp.int32, (CHUNK, CHUNK), 1))
    lane = lax.broadcasted_iota(jnp.int32, (CHUNK, LANES), 1)
    lo = lane < SSD_HD
    rowi = lax.broadcasted_iota(jnp.int32, (2 * SSD_HD, SSD_DSTATE), 0)
    row_lo = rowi < SSD_HD

    for ci in range(tr // CHUNK):
        rows = slice(ci * CHUNK, (ci + 1) * CHUNK)
        xs, bm, cm, dt = xs_all[rows], bm_all[rows], cm_all[rows], dt_all[rows]
        cum = _dot_exact(tril_ref[...], da_all[rows])
        cum_t = cum.T
        dt_t = dt.T
        ecum = jnp.exp(cum)
        wlast = jnp.exp(cum[CHUNK - 1:CHUNK, :] - cum) * dt
        elast = jnp.exp(cum_t[:, CHUNK - 1:CHUNK])
        cbw = [_dot_nt(cm[:, g * SSD_DSTATE:(g + 1) * SSD_DSTATE], bm[:, g * SSD_DSTATE:(g + 1) * SSD_DSTATE])
               for g in range(SSD_GROUPS)]
        for p in range(PAIRS):
            g = (2 * p) // HEADS_PER_GROUP
            xp = xs[:, p * LANES:(p + 1) * LANES]
            bg = bm[:, g * SSD_DSTATE:(g + 1) * SSD_DSTATE]
            cg = cm[:, g * SSD_DSTATE:(g + 1) * SSD_DSTATE]
            y = None
            for j, sel in ((0, lo), (1, jnp.logical_not(lo))):
                h = 2 * p + j
                seg = cum[:, h:h + 1] - cum_t[h:h + 1, :]
                lmask = jnp.exp(jnp.where(causal, seg, -jnp.inf))
                wgt = cbw[g] * lmask * dt_t[h:h + 1, :]
                yj = _dot(wgt.astype(BF16), jnp.where(sel, xp, 0.0).astype(BF16))
                y = yj if y is None else y + yj
            h0, h1 = 2 * p, 2 * p + 1
            sp = state[p]
            y = y + _dot_nt(cg, sp.astype(BF16)) * jnp.where(lo, ecum[:, h0:h0 + 1], ecum[:, h1:h1 + 1])
            wl = jnp.where(lo, wlast[:, h0:h0 + 1], wlast[:, h1:h1 + 1])
            state[p] = sp * jnp.where(row_lo, elast[h0:h0 + 1, :], elast[h1:h1 + 1, :]) + _dot_tn(
                (xp * wl).astype(BF16), bg)
            ybuf[rows, p * LANES:(p + 1) * LANES] = y + dfull_ref[:, p * LANES:(p + 1) * LANES] * xp
    for p in range(PAIRS):
        s_ref[0, 2 * p] = state[p, :SSD_HD, :]
        s_ref[0, 2 * p + 1] = state[p, SSD_HD:, :]
    return _ssd_gate_norm(ybuf[...], z_ref[...], ng_ref[...])


def _ssd_gate_norm(y, z, ng):
    gated = y * _silu(z)
    gw = D_SSD // SSD_GROUPS
    outs = []
    for g in range(SSD_GROUPS):
        outs.append(_rms_rows(gated[:, g * gw:(g + 1) * gw], ng[:, g * gw:(g + 1) * gw]))
    return jnp.concatenate(outs, axis=1)


SSD_PROMPT_PARAMS = ("ssd_conv_w", "ssd_conv_b", "ssd_dt_bias", "ssd_a_log", "ssd_d_full", "ssd_norm_g")


def _ssd_prompt_kernel(z_ref, xbc_ref, dt_ref, cw_ref, cb_ref, dtb_ref, alog_ref, dfull_ref, ng_ref,
                       tril_ref, y_ref, conv_ref, s_ref, xbuf, state, ybuf):
    c = pl.program_id(1)

    @pl.when(jnp.logical_and(pl.program_id(0) == 0, c == 0))
    def _():
        xbuf[...] = jnp.zeros_like(xbuf)
        state[...] = jnp.zeros_like(state)

    y_ref[...] = _ssd_tile(z_ref, xbc_ref, dt_ref, cw_ref, cb_ref, dtb_ref, alog_ref, dfull_ref, ng_ref,
                           tril_ref, conv_ref, s_ref, xbuf, state, ybuf, c == 0).astype(y_ref.dtype)


def _ssd_prompt(proj, bsz, seqlen, w, layer, tabs, tr):
    nc = seqlen // tr
    row = lambda b, c: b * nc + c
    const2 = lambda shape: pl.BlockSpec(shape, lambda b, c: (0, 0))
    return pl.pallas_call(
        _ssd_prompt_kernel,
        grid=(bsz, nc),
        in_specs=[
            pl.BlockSpec((tr, D_SSD), lambda b, c: (row(b, c), (COL_Z - PROMPT_COL0) // D_SSD)),
            pl.BlockSpec((tr, SSD_CONV_DIM), lambda b, c: (row(b, c), (COL_XBC - PROMPT_COL0) // SSD_CONV_DIM)),
            pl.BlockSpec((tr, LANES), lambda b, c: (row(b, c), (COL_DT - PROMPT_COL0) // LANES)),
        ] + [_lspec(w[n], layer) for n in SSD_PROMPT_PARAMS] + [const2((CHUNK, CHUNK))],
        out_specs=[
            pl.BlockSpec((tr, D_SSD), lambda b, c: (row(b, c), 0)),
            pl.BlockSpec((1, CONV_W - 1, SSD_CONV_DIM), lambda b, c: (b, 0, 0)),
            pl.BlockSpec((1, SSD_HEADS, SSD_HD, SSD_DSTATE), lambda b, c: (b, 0, 0, 0)),
        ],
        out_shape=[
            jax.ShapeDtypeStruct((bsz * seqlen, D_SSD), BF16),
            jax.ShapeDtypeStruct((bsz, CONV_W - 1, SSD_CONV_DIM), F32),
            jax.ShapeDtypeStruct((bsz, SSD_HEADS, SSD_HD, SSD_DSTATE), F32),
        ],
        scratch_shapes=[
            pltpu.VMEM((SUBLANES, SSD_CONV_DIM), F32),
            pltpu.VMEM((PAIRS, 2 * SSD_HD, SSD_DSTATE), F32),
            pltpu.VMEM((tr, D_SSD), F32),
        ],
        compiler_params=_cparams("arbitrary", "arbitrary"),
        name="ssd_prompt",
    )(proj, proj, proj, *[w[n] for n in SSD_PROMPT_PARAMS], tabs["tril"])


FF_TILES = ((0, 1536), (1536, 1280))
FFN_PARAMS = ("w_out", "norm_ffn_g", "ffn_w_up", "ffn_conv_w", "ffn_conv_b", "ffn_w_down")


def _ffn_body(x, ya, yb, yc, wo_ref, g_ref, wup_ref, cw_ref, cb_ref, wdn_ref, fg_ref, conv_fn, final_norm):
    x1 = x + _dot(ya, wo_ref[0:D_LRU, :])
    x1 = x1 + _dot(yb, wo_ref[D_LRU:D_LRU + D_RET, :])
    x1 = x1 + _dot(yc, wo_ref[D_LRU + D_RET:, :])
    hn = _rms_rows(x1, g_ref[...]).astype(BF16)
    acc = None
    for f0, fw in FF_TILES:
        gate = _dot(hn, wup_ref[:, f0:f0 + fw])
        up = _dot(hn, wup_ref[:, D_FF + f0:D_FF + f0 + fw])
        taps = [cw_ref[k:k + 1, f0:f0 + fw] for k in range(FFN_CONV_W)]
        conv = conv_fn(gate, f0, fw, taps, cb_ref[:, f0:f0 + fw])
        act = (_gelu(conv) * up).astype(BF16)
        d = _dot(act, wdn_ref[f0:f0 + fw, :])
        acc = d if acc is None else acc + d
    acc = x1 + acc
    if final_norm:
        acc = _rms_rows(acc, fg_ref[...])
    return acc


def _ffn_decode_kernel(x_ref, ya_ref, yb_ref, yc_ref, wo_ref, g_ref, wup_ref, cw_ref, cb_ref, wdn_ref, fg_ref,
                       st_ref, o_ref, conv_ref, *, final_norm):
    def conv_fn(gate, f0, fw, taps, cb):
        s0 = st_ref[:, f0:f0 + fw]
        s1 = st_ref[:, D_FF + f0:D_FF + f0 + fw]
        conv_ref[:, f0:f0 + fw] = s1
        conv_ref[:, D_FF + f0:D_FF + f0 + fw] = gate
        return cb + taps[0] * s0 + taps[1] * s1 + taps[2] * gate

    o_ref[...] = _ffn_body(x_ref[...], ya_ref[...], yb_ref[...], yc_ref[...], wo_ref, g_ref, wup_ref,
                           cw_ref, cb_ref, wdn_ref, fg_ref, conv_fn, final_norm)


def _ffn_prompt_kernel(x_ref, ya_ref, yb_ref, yc_ref, wo_ref, g_ref, wup_ref, cw_ref, cb_ref, wdn_ref, fg_ref,
                       o_ref, fconv_ref, halo, *, final_norm):
    tm = x_ref.shape[0]

    @pl.when(pl.program_id(1) == 0)
    def _():
        halo[...] = jnp.zeros_like(halo)

    def conv_fn(gate, f0, fw, taps, cb):
        conv = _causal_conv(gate, halo[:, f0:f0 + fw], taps, cb)
        tail = gate[tm - SUBLANES:, :]
        halo[:, f0:f0 + fw] = tail
        fconv_ref[0, :, f0:f0 + fw] = tail[SUBLANES - (FFN_CONV_W - 1):, :]
        return conv

    o_ref[...] = _ffn_body(x_ref[...], ya_ref[...], yb_ref[...], yc_ref[...], wo_ref, g_ref, wup_ref,
                           cw_ref, cb_ref, wdn_ref, fg_ref, conv_fn, final_norm)


def _ffn_decode(x2d, ya, yb, yc, w, layer, fg, st, *, final_norm):
    t = x2d.shape[0]
    rowmap = lambda i: (0, 0)
    blk = lambda width: pl.BlockSpec((t, width), rowmap)
    return pl.pallas_call(
        functools.partial(_ffn_decode_kernel, final_norm=final_norm),
        grid=(1,),
        in_specs=[blk(D_MODEL), blk(D_LRU), blk(D_RET), blk(D_SSD)]
        + [_lspec(w[n], layer) for n in FFN_PARAMS]
        + [pl.BlockSpec((1, D_MODEL), rowmap),
           pl.BlockSpec((None, t, 2 * D_FF), lambda i: (layer, 0, 0))],
        out_specs=[blk(D_MODEL), blk(2 * D_FF)],
        out_shape=[jax.ShapeDtypeStruct((t, D_MODEL), F32), jax.ShapeDtypeStruct((t, 2 * D_FF), F32)],
        compiler_params=_cparams("arbitrary"),
        name="ffn_decode",
    )(x2d, ya, yb, yc, *[w[n] for n in FFN_PARAMS], fg, st)


def _ffn_prompt(x2d, ya, yb, yc, w, layer, fg, *, bsz, seqlen, tm, final_norm):
    nt = seqlen // tm
    rows = lambda width: pl.BlockSpec((tm, width), lambda b, i: (b * nt + i, 0))
    return pl.pallas_call(
        functools.partial(_ffn_prompt_kernel, final_norm=final_norm),
        grid=(bsz, nt),
        in_specs=[rows(D_MODEL), rows(D_LRU), rows(D_RET), rows(D_SSD)]
        + [_lspec(w[n], layer) for n in FFN_PARAMS] + [pl.BlockSpec((1, D_MODEL), lambda b, i: (0, 0))],
        out_specs=[rows(D_MODEL), pl.BlockSpec((1, FFN_CONV_W - 1, D_FF), lambda b, i: (b, 0, 0))],
        out_shape=[
            jax.ShapeDtypeStruct((bsz * seqlen, D_MODEL), F32),
            jax.ShapeDtypeStruct((bsz, FFN_CONV_W - 1, D_FF), F32),
        ],
        scratch_shapes=[pltpu.VMEM((SUBLANES, D_FF), F32)],
        compiler_params=_cparams("arbitrary", "arbitrary"),
        name="ffn_prompt",
    )(x2d, ya, yb, yc, *[w[n] for n in FFN_PARAMS], fg)


DEC_TILE = 16


MIX_DECODE_INPUTS = 35


def _mix_decode_kernel(*refs, n_alias):
    _mix_decode_body(*refs[:MIX_DECODE_INPUTS], *refs[MIX_DECODE_INPUTS + n_alias:])


def _mix_decode_body(lx_ref, lg_ref, q_ref, k_ref, v_ref, rg_ref, z_ref, xbc_ref, dt_ref,
                       lconv_ref, lh_ref, rs_ref, sconv_ref, ss_ref,
                       lcw_ref, lcb_ref, wr_ref, br_ref, wi_ref, bi_ref, lam_ref,
                       cos_ref, sin_ref, cdec_ref, rng_ref,
                       scw_ref, scb_ref, dtb_ref, alog_ref, dfull_ref, sng_ref,
                       eye_ref, hexp_ref, hexp_tile_ref, rsel_ref,
                       ya_ref, yb_ref, yc_ref, lconv_o, lh_o, rs_o, sconv_o, ss_o,
                       obuf, ybuf):
    x = lx_ref[...]
    st = lconv_ref[...]
    xc = lcb_ref[...] + lcw_ref[0:1, :] * st[:, 0:D_LRU] + lcw_ref[1:2, :] * st[:, D_LRU:2 * D_LRU]
    xc = xc + lcw_ref[2:3, :] * st[:, 2 * D_LRU:] + lcw_ref[3:4, :] * x
    lconv_o[:, 0:2 * D_LRU] = st[:, D_LRU:]
    lconv_o[:, 2 * D_LRU:] = x
    a, b = _lru_gates(xc, wr_ref, br_ref, wi_ref, bi_ref, lam_ref)
    h = b + a * lh_ref[...]
    lh_o[...] = h
    ya_ref[...] = (h * _gelu(lg_ref[...])).astype(ya_ref.dtype)

    cos = cos_ref[...]
    sin = sin_ref[...]
    q = _rotary_split(q_ref[...], cos, sin)
    k = _rotary_split(k_ref[...], cos, sin) * (RET_DK ** -0.5)
    v = v_ref[...]

    xin = xbc_ref[...]
    sst = sconv_ref[...]
    n = SSD_CONV_DIM
    xc = scb_ref[...] + scw_ref[0:1, :] * sst[:, 0:n] + scw_ref[1:2, :] * sst[:, n:2 * n]
    xc = xc + scw_ref[2:3, :] * sst[:, 2 * n:] + scw_ref[3:4, :] * xin
    sconv_o[:, 0:2 * n] = sst[:, n:]
    sconv_o[:, 2 * n:] = xin
    xc = _silu(xc)
    xs = xc[:, :D_SSD]
    bm = xc[:, D_SSD:D_SSD + SSD_BC]
    cm_b = xc[:, D_SSD + SSD_BC:].astype(BF16)
    dt = _softplus(dt_ref[...] + dtb_ref[...])
    dt_full = _dot_exact(dt, hexp_ref[...])
    eda_tiles = _dot_exact(jnp.exp(dt * (-jnp.exp(alog_ref[...]))), hexp_tile_ref[...])

    cols_src = jnp.concatenate([xs * dt_full, k], axis=1)
    cols_src = jnp.concatenate(
        [cols_src, jnp.zeros((LANES - DEC_TILE, cols_src.shape[1]), F32)], axis=0).astype(BF16)
    eye = eye_ref[...]
    cols_t = jnp.concatenate(
        [_dot_nt(eye, cols_src[:, j * LANES:(j + 1) * LANES]) for j in range(cols_src.shape[1] // LANES)],
        axis=0).astype(BF16)
    k_row0 = D_SSD

    head_q = _head_of_qk_lane((DEC_TILE, D_QK), 1)
    q_heads = jnp.concatenate([jnp.where(head_q == hh, q, 0.0) for hh in range(RET_HEADS)],
                              axis=0).astype(BF16)
    half = RET_DK // 2
    for r in range(DEC_TILE):
        colb = _dot(cols_t, rsel_ref[r])
        pieces = []
        for hh in range(SSD_HEADS):
            g = hh // HEADS_PER_GROUP
            e = jnp.broadcast_to(eda_tiles[r:r + 1, hh * LANES:(hh + 1) * LANES], (SSD_HD, SSD_DSTATE))
            brow = bm[r:r + 1, g * SSD_DSTATE:(g + 1) * SSD_DSTATE]
            s_new = ss_ref[0, r, hh] * e + colb[hh * SSD_HD:(hh + 1) * SSD_HD, :] * brow
            ss_o[0, r, hh] = s_new
            pieces.append(s_new.astype(BF16))
        gw = D_SSD // SSD_GROUPS
        for g in range(SSD_GROUPS):
            sg = jnp.concatenate(pieces[g * HEADS_PER_GROUP:(g + 1) * HEADS_PER_GROUP], axis=0)
            yg = _dot_nt(cm_b[:, g * SSD_DSTATE:(g + 1) * SSD_DSTATE], sg)
            ybuf[r:r + 1, g * gw:(g + 1) * gw] = yg[r:r + 1, :]
        pieces = []
        for part in range(2):
            for hh in range(RET_HEADS):
                r0 = k_row0 + part * LANES + hh * half
                vrow = v[r:r + 1, hh * RET_DV:(hh + 1) * RET_DV]
                s_old = rs_ref[0, r, hh, part * half:(part + 1) * half, :]
                s_new = s_old * cdec_ref[hh] + colb[r0:r0 + half, :] * vrow
                rs_o[0, r, hh, part * half:(part + 1) * half, :] = s_new
                pieces.append(s_new.astype(BF16))
        s_perm = jnp.concatenate(pieces, axis=0)
        res = _dot(q_heads, s_perm)
        for hh in range(RET_HEADS):
            obuf[r:r + 1, hh * RET_DV:(hh + 1) * RET_DV] = res[hh * DEC_TILE + r:hh * DEC_TILE + r + 1, :]

    o = obuf[...]
    rg = rg_ref[...]
    for hh in range(RET_HEADS):
        oh = _rms_rows(o[:, hh * RET_DV:(hh + 1) * RET_DV], rng_ref[hh:hh + 1, :])
        yb_ref[:, hh * RET_DV:(hh + 1) * RET_DV] = (
            _silu(rg[:, hh * RET_DV:(hh + 1) * RET_DV]) * oh).astype(yb_ref.dtype)
    y = ybuf[...] + dfull_ref[...] * xs
    yc_ref[...] = _ssd_gate_norm(y, z_ref[...], sng_ref[...]).astype(yc_ref.dtype)


SSD_DECODE_PARAMS = SSD_PROMPT_PARAMS


def _mix_decode(proj, states, layer, w, tabs, prev_full):
    nb = proj.shape[0]
    depth = states[2].shape[0]
    grid = (nb // DEC_TILE,)
    col = lambda width, c0: pl.BlockSpec((DEC_TILE, width), lambda i: (i, c0 // width))
    const2 = lambda shape: pl.BlockSpec(shape, lambda i: (0, 0))
    const3 = lambda shape: pl.BlockSpec(shape, lambda i: (0, 0, 0))
    lconv, lh, rs, sconv, ss = states
    ret_blk = pl.BlockSpec((1, DEC_TILE, RET_HEADS, RET_DK, RET_DV), lambda i: (layer, i, 0, 0, 0))
    ssd_blk = pl.BlockSpec((1, DEC_TILE, SSD_HEADS, SSD_HD, SSD_DSTATE), lambda i: (layer, i, 0, 0, 0))
    in_specs = [
        col(D_LRU, COL_LRU_X), col(D_LRU, COL_LRU_G), col(D_QK, COL_Q), col(D_QK, COL_K),
        col(D_RET, COL_V), col(D_RET, COL_RG), col(D_SSD, COL_Z), col(SSD_CONV_DIM, COL_XBC),
        col(LANES, COL_DT),
        pl.BlockSpec((None, DEC_TILE, (CONV_W - 1) * D_LRU), lambda i: (layer, i, 0)),
        pl.BlockSpec((None, DEC_TILE, D_LRU), lambda i: (layer, i, 0)),
        ret_blk,
        pl.BlockSpec((None, DEC_TILE, (CONV_W - 1) * SSD_CONV_DIM), lambda i: (layer, i, 0)),
        ssd_blk,
    ] + [_lspec(w[n], layer) for n in LRU_PARAMS] + [
        const2((1, LANES)), const2((1, LANES)), const3((RET_HEADS, 1, 1)), _lspec(w["ret_norm_g"], layer),
    ] + [_lspec(w[n], layer) for n in SSD_DECODE_PARAMS] + [
        const2((LANES, LANES)), const2((LANES, D_SSD)), const2((LANES, SSD_HEADS * LANES)),
        const3((DEC_TILE, LANES, LANES)),
    ]
    args = [proj] * 9 + [lconv, lh, rs, sconv, ss] + [w[n] for n in LRU_PARAMS] + [
        tabs["cos_s"], tabs["sin_s"], tabs["cdec1"], w["ret_norm_g"]] + [
        w[n] for n in SSD_DECODE_PARAMS] + [tabs["eye"], tabs["hexp"], tabs["hexp_tile"], tabs["rsel"]]
    assert len(args) == MIX_DECODE_INPUTS
    aliases = {}
    if prev_full is not None:
        in_specs += [pl.BlockSpec(memory_space=pl.ANY)] * 2
        args += list(prev_full)
        aliases = {MIX_DECODE_INPUTS: 5, MIX_DECODE_INPUTS + 1: 7}
    out_specs = [
        pl.BlockSpec((DEC_TILE, D_LRU), lambda i: (i, 0)),
        pl.BlockSpec((DEC_TILE, D_RET), lambda i: (i, 0)),
        pl.BlockSpec((DEC_TILE, D_SSD), lambda i: (i, 0)),
        pl.BlockSpec((DEC_TILE, (CONV_W - 1) * D_LRU), lambda i: (i, 0)),
        pl.BlockSpec((DEC_TILE, D_LRU), lambda i: (i, 0)),
        ret_blk,
        pl.BlockSpec((DEC_TILE, (CONV_W - 1) * SSD_CONV_DIM), lambda i: (i, 0)),
        ssd_blk,
    ]
    out_shape = [
        jax.ShapeDtypeStruct((nb, D_LRU), BF16),
        jax.ShapeDtypeStruct((nb, D_RET), BF16),
        jax.ShapeDtypeStruct((nb, D_SSD), BF16),
        jax.ShapeDtypeStruct((nb, (CONV_W - 1) * D_LRU), F32),
        jax.ShapeDtypeStruct((nb, D_LRU), F32),
        jax.ShapeDtypeStruct((depth, nb, RET_HEADS, RET_DK, RET_DV), F32),
        jax.ShapeDtypeStruct((nb, (CONV_W - 1) * SSD_CONV_DIM), F32),
        jax.ShapeDtypeStruct((depth, nb, SSD_HEADS, SSD_HD, SSD_DSTATE), F32),
    ]
    return pl.pallas_call(
        functools.partial(_mix_decode_kernel, n_alias=len(aliases)),
        grid=grid, in_specs=in_specs, out_specs=out_specs, out_shape=out_shape,
        input_output_aliases=aliases,
        scratch_shapes=[pltpu.VMEM((DEC_TILE, D_RET), F32), pltpu.VMEM((DEC_TILE, D_SSD), F32)],
        compiler_params=_cparams("arbitrary"),
        name="mix_decode",
    )(*args)


def _qk_perm():
    half = RET_DK // 2
    idx = np.empty((D_QK,), np.int32)
    for part in range(2):
        for h in range(RET_HEADS):
            for j in range(half):
                idx[part * LANES + h * half + j] = h * RET_DK + part * half + j
    return idx


def _block_diag(wh):
    nh, hd, _ = wh.shape
    bands = [jnp.pad(wh[h], ((0, 0), (h * hd, (nh - 1 - h) * hd))) for h in range(nh)]
    return jnp.concatenate(bands, axis=0)


def _prep_weights(p):
    perm = _qk_perm()
    w_in = p["w_in"]
    depth = w_in.shape[0]
    w_qk = jnp.concatenate([w_in[:, :, COL_Q:COL_Q + D_QK][:, :, perm],
                            w_in[:, :, COL_K:COL_K + D_QK][:, :, perm]], axis=2).astype(BF16)
    n_dt = w_in.shape[2] - COL_DT
    w_dt = jnp.concatenate([w_in[:, :, COL_DT:], jnp.zeros((depth, D_MODEL, LANES - n_dt), w_in.dtype)],
                           axis=2).astype(BF16)
    row = lambda v: v[:, None, :]
    pad_lanes = lambda v: row(jnp.concatenate([v, jnp.zeros((depth, LANES - v.shape[1]), v.dtype)], axis=1))
    return {
        "norm_mix_g": row(p["norm_mix_g"]),
        "w_in": w_in.astype(BF16),
        "w_qk": w_qk,
        "w_dt": w_dt,
        "lru_conv_w": p["lru_conv_w"],
        "lru_conv_b": row(p["lru_conv_b"]),
        "lru_wr": jax.vmap(_block_diag)(p["lru_wr"]).astype(BF16),
        "lru_br": row(p["lru_br"]),
        "lru_wi": jax.vmap(_block_diag)(p["lru_wi"]).astype(BF16),
        "lru_bi": row(p["lru_bi"]),
        "lru_lambda": row(p["lru_lambda"]),
        "ret_norm_g": p["ret_norm_g"],
        "ssd_conv_w": p["ssd_conv_w"],
        "ssd_conv_b": row(p["ssd_conv_b"]),
        "ssd_dt_bias": pad_lanes(p["ssd_dt_bias"]),
        "ssd_a_log": pad_lanes(p["ssd_a_log"]),
        "ssd_d_full": row(jnp.repeat(p["ssd_d"], SSD_HD, axis=1)),
        "ssd_norm_g": row(p["ssd_norm_g"]),
        "w_out": p["w_out"].astype(BF16),
        "norm_ffn_g": row(p["norm_ffn_g"]),
        "ffn_w_up": p["ffn_w_up"].astype(BF16),
        "ffn_conv_w": p["ffn_conv_w"],
        "ffn_conv_b": row(p["ffn_conv_b"]),
        "ffn_w_down": p["ffn_w_down"].astype(BF16),
    }


def _tables(seqlen, past_len):
    half = RET_DK // 2
    freqs = ROPE_BASE ** (-np.arange(half, dtype=np.float64) / half)

    def cs(pos):
        ang = pos.astype(np.float64)[:, None] * freqs[None, :]
        return np.tile(np.cos(ang), (1, RET_HEADS)), np.tile(np.sin(ang), (1, RET_HEADS))

    cos_p, sin_p = cs(np.arange(seqlen))
    cos_s, sin_s = cs(past_len + np.arange(1))
    log_g = np.log1p(-np.exp2(-5.0 - np.arange(RET_HEADS, dtype=np.float64)))
    idx = np.arange(CHUNK, dtype=np.float64)
    diff = idx[:, None] - idx[None, :]
    decay = np.where(diff[None] >= 0, np.exp(diff[None] * log_g[:, None, None]), 0.0)
    q_dec = np.exp((idx[None, :] + 1.0) * log_g[:, None])
    k_dec = np.exp((CHUNK - 1.0 - idx[None, :]) * log_g[:, None])
    lanes_of = lambda t: np.tile(np.repeat(t.T, half, axis=1), (1, 2))
    f32 = lambda t: jnp.asarray(np.asarray(t, np.float32))
    hexp = np.arange(LANES)[:, None] == (np.arange(D_SSD)[None, :] // SSD_HD)
    hexp_tile = np.arange(LANES)[:, None] == (np.arange(SSD_HEADS * LANES)[None, :] // LANES)
    rsel = np.broadcast_to(np.arange(LANES)[None, :, None] == np.arange(DEC_TILE)[:, None, None],
                           (DEC_TILE, LANES, LANES))
    return {
        "cos_p": f32(cos_p), "sin_p": f32(sin_p), "cos_s": f32(cos_s), "sin_s": f32(sin_s),
        "decay": f32(decay), "qdec": f32(lanes_of(q_dec)), "kdec": f32(lanes_of(k_dec)),
        "cdec": f32(np.repeat(np.exp(CHUNK * log_g), RET_DV)[None, :]),
        "cdec1": f32(np.exp(log_g)[:, None, None]),
        "tril": f32(np.tril(np.ones((CHUNK, CHUNK)))),
        "eye": f32(np.eye(LANES)).astype(BF16),
        "hexp": f32(hexp),
        "hexp_tile": f32(hexp_tile),
        "rsel": f32(rsel).astype(BF16),
    }


TM_PROMPT = 512
TR_MIX = 512
TR_RET = 1024


def _prompt_stack(x, w, tabs, fg, tm=TM_PROMPT, tr=TR_MIX, tr_ret=TR_RET):
    bsz, seqlen, _ = x.shape
    depth = w["w_in"].shape[0]
    x2d = x.reshape(bsz * seqlen, D_MODEL)
    outs = [[] for _ in range(6)]
    for l in range(depth):
        proj, ya, lconv, lh = _inproj_lru_prompt(x2d, w, l, bsz=bsz, seqlen=seqlen, tm=tm)
        yb, rs = _ret_prompt(proj, bsz, seqlen, w, l, tabs, tr_ret)
        yc, sconv, ss = _ssd_prompt(proj, bsz, seqlen, w, l, tabs, tr)
        x2d, fconv = _ffn_prompt(x2d, ya, yb, yc, w, l, fg, bsz=bsz, seqlen=seqlen, tm=tm,
                                 final_norm=(l == depth - 1))
        for lst, val in zip(outs, (lconv, lh[:, 0, :], rs, sconv, ss, fconv)):
            lst.append(val)
    return x2d.reshape(bsz, seqlen, D_MODEL), [jnp.stack(o, axis=0) for o in outs]


def _sample_stack(x, states, w, tabs, fg):
    nb = x.shape[0]
    depth = w["w_in"].shape[0]
    x2d = x.reshape(nb, D_MODEL)
    st_lconv, st_lh, st_rs, st_sconv, st_ss, st_fconv = states
    lconv_in = st_lconv.reshape(depth, nb, (CONV_W - 1) * D_LRU)
    sconv_in = st_sconv.reshape(depth, nb, (CONV_W - 1) * SSD_CONV_DIM)
    fconv_in = st_fconv.reshape(depth, nb, (FFN_CONV_W - 1) * D_FF)
    outs = [[] for _ in range(4)]
    full = None
    for l in range(depth):
        proj = _inproj(x2d, w, l, tm=nb)
        ya, yb, yc, lconv, lh, rs_full, sconv, ss_full = _mix_decode(
            proj, (lconv_in, st_lh, st_rs, sconv_in, st_ss), l, w, tabs, full)
        full = (rs_full, ss_full)
        x2d, fconv = _ffn_decode(x2d, ya, yb, yc, w, l, fg, fconv_in, final_norm=(l == depth - 1))
        for lst, val in zip(outs, (lconv, lh, sconv, fconv)):
            lst.append(val)
    lconv, lh, sconv, fconv = [jnp.stack(o, axis=0) for o in outs]
    new_states = [lconv.reshape(depth, nb, CONV_W - 1, D_LRU), lh, full[0],
                  sconv.reshape(depth, nb, CONV_W - 1, SSD_CONV_DIM), full[1],
                  fconv.reshape(depth, nb, FFN_CONV_W - 1, D_FF)]
    return x2d.reshape(nb, 1, D_MODEL), new_states


def kernel(x_prompt, x_sample, state_lru_conv, state_lru_h, state_ret, state_ssd_conv, state_ssd, state_ffn_conv,
           norm_mix_g, w_in, lru_conv_w, lru_conv_b, lru_wr, lru_br, lru_wi, lru_bi, lru_lambda,
           ret_norm_g, ssd_conv_w, ssd_conv_b, ssd_dt_bias, ssd_a_log, ssd_d, ssd_norm_g,
           w_out, norm_ffn_g, ffn_w_up, ffn_conv_w, ffn_conv_b, ffn_w_down, norm_final_g):
    p = {"norm_mix_g": norm_mix_g, "w_in": w_in, "lru_conv_w": lru_conv_w, "lru_conv_b": lru_conv_b,
         "lru_wr": lru_wr, "lru_br": lru_br, "lru_wi": lru_wi, "lru_bi": lru_bi, "lru_lambda": lru_lambda,
         "ret_norm_g": ret_norm_g, "ssd_conv_w": ssd_conv_w, "ssd_conv_b": ssd_conv_b,
         "ssd_dt_bias": ssd_dt_bias, "ssd_a_log": ssd_a_log, "ssd_d": ssd_d, "ssd_norm_g": ssd_norm_g,
         "w_out": w_out, "norm_ffn_g": norm_ffn_g, "ffn_w_up": ffn_w_up, "ffn_conv_w": ffn_conv_w,
         "ffn_conv_b": ffn_conv_b, "ffn_w_down": ffn_w_down}
    weights = _prep_weights(p)
    past_len = 16384
    tabs = _tables(x_prompt.shape[1], past_len)
    fg = norm_final_g[None, :]

    y_p, ns_p = _prompt_stack(x_prompt, weights, tabs, fg)
    y_s, ns_s = _sample_stack(
        x_sample, (state_lru_conv, state_lru_h, state_ret, state_ssd_conv, state_ssd, state_ffn_conv),
        weights, tabs, fg)
    lru_conv_p, lru_h_p, ret_p, ssd_conv_p, ssd_p, ffn_conv_p = ns_p
    lru_conv_s, lru_h_s, ret_s, ssd_conv_s, ssd_s, ffn_conv_s = ns_s
    return (y_p, y_s, lru_conv_p, lru_conv_s, lru_h_p, lru_h_s, ret_p, ret_s,
            ssd_conv_p, ssd_conv_s, ssd_p, ssd_s, ffn_conv_p, ffn_conv_s)
```

```python
import functools

import jax
import jax.numpy as jnp
import numpy as np
from jax import lax
from jax.experimental import pallas as pl
from jax.experimental.pallas import tpu as pltpu

F32 = jnp.float32
BF16 = jnp.bfloat16

D_MODEL = 1024
DEPTH = 4
D_LRU = 512
LRU_HEADS = 8
LRU_HD = D_LRU // LRU_HEADS
LRU_C = 8.0
CONV_W = 4
RET_HEADS = 4
RET_DK = 64
RET_DV = 128
D_RET = RET_HEADS * RET_DV
D_QK = RET_HEADS * RET_DK
ROPE_BASE = 10000.0
SSD_HD = 64
D_SSD = 512
SSD_HEADS = 8
SSD_GROUPS = 2
SSD_DSTATE = 128
SSD_BC = SSD_GROUPS * SSD_DSTATE
SSD_CONV_DIM = D_SSD + 2 * SSD_BC
D_MIX = D_LRU + D_RET + D_SSD
D_FF = 2816
FFN_CONV_W = 3
EPS = 1e-6
CHUNK = 128

COL_LRU_X = 0
COL_LRU_G = 512
COL_Q = 1024
COL_K = 1280
COL_V = 1536
COL_RG = 2048
COL_Z = 2560
COL_XBC = 3072
COL_DT = 4096
D_IN_PAD = 4224
PROMPT_COL0 = COL_Q
D_IN_PROMPT = D_IN_PAD - PROMPT_COL0

LANES = 128
SUBLANES = 8
VMEM_LIMIT_BYTES = 56 * 1024 * 1024


def _cparams(*sem):
    return pltpu.CompilerParams(dimension_semantics=sem, vmem_limit_bytes=VMEM_LIMIT_BYTES)


def _dot(a, b):
    return jnp.dot(a, b, preferred_element_type=F32)


def _dot_nt(a, b):
    return lax.dot_general(a, b, (((1,), (1,)), ((), ())), preferred_element_type=F32)


def _dot_tn(a, b):
    return lax.dot_general(a, b, (((0,), (0,)), ((), ())), preferred_element_type=F32)


def _dot_exact(a, b):
    return jnp.dot(a, b, preferred_element_type=F32, precision=lax.Precision.HIGHEST)


def _dot_nt_exact(a, b):
    return lax.dot_general(a, b, (((1,), (1,)), ((), ())), preferred_element_type=F32,
                           precision=lax.Precision.HIGHEST)


def _rms_rows(x, g):
    return x * lax.rsqrt(jnp.mean(x * x, axis=-1, keepdims=True) + EPS) * g


def _sigmoid(x):
    return 1.0 / (1.0 + jnp.exp(-x))


def _silu(x):
    return x * _sigmoid(x)


def _gelu(x):
    return jax.nn.gelu(x, approximate=True)


def _softplus(x):
    return jnp.maximum(x, 0.0) + jnp.log1p(jnp.exp(-jnp.abs(x)))


IN_COL_CHUNK = 512


def _lspec(arr, layer):
    _, r, c = arr.shape
    return pl.BlockSpec((None, r, c), lambda *_: (layer, 0, 0), pipeline_mode=pl.Buffered(1))


def _inproj_kernel(x_ref, g_ref, wm_ref, wqk_ref, wdt_ref, o_ref):
    hn = _rms_rows(x_ref[...], g_ref[...]).astype(BF16)
    for c0 in range(0, COL_DT, IN_COL_CHUNK):
        if c0 == COL_Q:
            o_ref[:, c0:c0 + IN_COL_CHUNK] = _dot(hn, wqk_ref[...])
        else:
            o_ref[:, c0:c0 + IN_COL_CHUNK] = _dot(hn, wm_ref[:, c0:c0 + IN_COL_CHUNK])
    o_ref[:, COL_DT:] = _dot(hn, wdt_ref[...])


def _inproj(x2d, w, layer, tm):
    t = x2d.shape[0]
    return pl.pallas_call(
        _inproj_kernel,
        grid=(t // tm,),
        in_specs=[
            pl.BlockSpec((tm, D_MODEL), lambda i: (i, 0)),
            _lspec(w["norm_mix_g"], layer),
            _lspec(w["w_in"], layer),
            _lspec(w["w_qk"], layer),
            _lspec(w["w_dt"], layer),
        ],
        out_specs=pl.BlockSpec((tm, D_IN_PAD), lambda i: (i, 0)),
        out_shape=jax.ShapeDtypeStruct((t, D_IN_PAD), F32),
        compiler_params=_cparams("arbitrary"),
        name="inproj",
    )(x2d, w["norm_mix_g"], w["w_in"], w["w_qk"], w["w_dt"])


def _lru_gate_pre(xc, wr_ref, br_ref, wi_ref, bi_ref, lam_ref):
    xcb = xc.astype(BF16)
    return (_dot(xcb, wr_ref[...]) + br_ref[...], _dot(xcb, wi_ref[...]) + bi_ref[...],
            _softplus(-lam_ref[...]))


def _lru_gate_post(xc, r_pre, i_pre, sp):
    log_a = (-LRU_C * _sigmoid(r_pre)) * sp
    a = jnp.exp(log_a)
    b = jnp.sqrt(-jnp.tanh(log_a) * (a * a + 1.0)) * (_sigmoid(i_pre) * xc)
    return a, b


def _lru_gates(xc, wr_ref, br_ref, wi_ref, bi_ref, lam_ref):
    return _lru_gate_post(xc, *_lru_gate_pre(xc, wr_ref, br_ref, wi_ref, bi_ref, lam_ref))


def _scan_rows(a, b, h_in):
    n, cols = a.shape
    groups = n // SUBLANES
    a = a.reshape(groups, SUBLANES, cols)
    b = b.reshape(groups, SUBLANES, cols)
    row = lax.broadcasted_iota(jnp.int32, a.shape, 1)
    d = 1
    while d < SUBLANES:
        keep = row >= d
        a_sh = jnp.where(keep, pltpu.roll(a, d, axis=1), 1.0)
        b_sh = jnp.where(keep, pltpu.roll(b, d, axis=1), 0.0)
        b = a * b_sh + b
        a = a * a_sh
        d *= 2
    hs = []
    h_prev = h_in
    for g in range(groups):
        hg = b[g] + a[g] * h_prev
        hs.append(hg)
        h_prev = hg[SUBLANES - 1:SUBLANES, :]
    return jnp.concatenate(hs, axis=0)


def _causal_conv(x, halo, taps, bias):
    n, cols = x.shape
    groups = n // SUBLANES
    xe = jnp.concatenate([halo, x], axis=0).reshape(groups + 1, SUBLANES, cols)
    row = lax.broadcasted_iota(jnp.int32, (groups, SUBLANES, cols), 1)
    if len(taps) == 4:
        w0, w1, w2, w3 = taps
        r1 = pltpu.roll(xe, 1, axis=1)
        sx = jnp.concatenate([r1[0:1], jnp.where(row >= 1, r1[1:], r1[:-1])], axis=0)
        r2 = pltpu.roll(w1 * xe + w0 * sx, 2, axis=1)
        s2u = jnp.where(row >= 2, r2[1:], r2[:-1])
        return bias + w3 * x + (w2 * sx[1:] + s2u).reshape(n, cols)
    y = bias + taps[-1] * x
    for s in range(1, len(taps)):
        r = pltpu.roll(xe, s, axis=1)
        shifted = jnp.where(row >= s, r[1:], r[:-1]).reshape(n, cols)
        y = y + taps[-1 - s] * shifted
    return y


def _rotary_split(x, cos, sin):
    xa, xb = x[:, :LANES], x[:, LANES:]
    return jnp.concatenate([xa * cos - xb * sin, xa * sin + xb * cos], axis=1)


def _head_of_qk_lane(shape, axis):
    lane = lax.broadcasted_iota(jnp.int32, shape, axis)
    return (lane % LANES) // (RET_DK // 2)


def _lru_back(x, gate, front, conv_ref, h_ref, xbuf, hcar, reset, valid):
    xc, pre, halo_old, h_old = front
    tl = x.shape[0]
    a, b = _lru_gate_post(xc, *pre)
    h = _scan_rows(a, b, jnp.where(reset, 0.0, h_old))
    y = h * _gelu(gate)
    tail = jnp.where(valid, x[tl - SUBLANES:, :], halo_old)
    xbuf[...] = tail
    conv_ref[0] = tail[SUBLANES - (CONV_W - 1):, :]
    hlast = jnp.where(valid, h[tl - 1:tl, :], h_old)
    hcar[...] = hlast
    h_ref[0] = hlast
    return y


LRU_PARAMS = ("lru_conv_w", "lru_conv_b", "lru_wr", "lru_br", "lru_wi", "lru_bi", "lru_lambda")


def _inproj_lru_kernel(x_ref, g_ref, wm_ref, wqk_ref, wdt_ref,
                       cw_ref, cb_ref, wr_ref, br_ref, wi_ref, bi_ref, lam_ref,
                       o_ref, y_ref, conv_ref, h_ref, ring, xbuf, hcar, *, tiles_per_seq):
    g = pl.program_id(0)

    @pl.when(g == 0)
    def _():
        xbuf[...] = jnp.zeros_like(xbuf)
        hcar[...] = jnp.zeros_like(hcar)
        ring[...] = jnp.zeros_like(ring)

    valid = g > 0
    reset = jnp.logical_and((g + tiles_per_seq - 1) % tiles_per_seq == 0, valid)
    lru_x = ring[:, 0:D_LRU]
    lru_gate = ring[:, D_LRU:]
    halo_old = xbuf[...]
    h_old = hcar[...]
    xc = _causal_conv(lru_x, jnp.where(reset, 0.0, halo_old), [cw_ref[k:k + 1, :] for k in range(CONV_W)],
                      cb_ref[...])
    front = (xc, _lru_gate_pre(xc, wr_ref, br_ref, wi_ref, bi_ref, lam_ref), halo_old, h_old)

    hn = _rms_rows(x_ref[...], g_ref[...]).astype(BF16)
    for c0 in range(0, COL_DT, IN_COL_CHUNK):
        o0 = c0 - PROMPT_COL0
        if c0 == COL_Q:
            o_ref[:, o0:o0 + IN_COL_CHUNK] = _dot(hn, wqk_ref[...])
        elif c0 < PROMPT_COL0:
            ring[:, c0:c0 + IN_COL_CHUNK] = _dot(hn, wm_ref[:, c0:c0 + IN_COL_CHUNK])
        else:
            o_ref[:, o0:o0 + IN_COL_CHUNK] = _dot(hn, wm_ref[:, c0:c0 + IN_COL_CHUNK])
    o_ref[:, COL_DT - PROMPT_COL0:] = _dot(hn, wdt_ref[...])

    y_ref[...] = _lru_back(lru_x, lru_gate, front, conv_ref, h_ref, xbuf, hcar, reset, valid).astype(
        y_ref.dtype)


def _inproj_lru_prompt(x2d, w, layer, *, bsz, seqlen, tm):
    tps = seqlen // tm
    n_tiles = bsz * tps
    proj_tile = lambda g: jnp.minimum(g, n_tiles - 1)
    lru_tile = lambda g: jnp.maximum(g - 1, 0)
    in_names = ("norm_mix_g", "w_in", "w_qk", "w_dt")
    return pl.pallas_call(
        functools.partial(_inproj_lru_kernel, tiles_per_seq=tps),
        grid=(n_tiles + 1,),
        in_specs=[pl.BlockSpec((tm, D_MODEL), lambda g: (proj_tile(g), 0))]
        + [_lspec(w[n], layer) for n in in_names] + [_lspec(w[n], layer) for n in LRU_PARAMS],
        out_specs=[
            pl.BlockSpec((tm, D_IN_PROMPT), lambda g: (proj_tile(g), 0)),
            pl.BlockSpec((tm, D_LRU), lambda g: (lru_tile(g), 0)),
            pl.BlockSpec((1, CONV_W - 1, D_LRU), lambda g: (lru_tile(g) // tps, 0, 0)),
            pl.BlockSpec((1, 1, D_LRU), lambda g: (lru_tile(g) // tps, 0, 0)),
        ],
        out_shape=[
            jax.ShapeDtypeStruct((bsz * seqlen, D_IN_PROMPT), F32),
            jax.ShapeDtypeStruct((bsz * seqlen, D_LRU), BF16),
            jax.ShapeDtypeStruct((bsz, CONV_W - 1, D_LRU), F32),
            jax.ShapeDtypeStruct((bsz, 1, D_LRU), F32),
        ],
        scratch_shapes=[
            pltpu.VMEM((tm, 2 * D_LRU), F32),
            pltpu.VMEM((SUBLANES, D_LRU), F32),
            pltpu.VMEM((1, D_LRU), F32),
        ],
        compiler_params=_cparams("arbitrary"),
        name="inproj_lru_prompt",
    )(x2d, *[w[n] for n in in_names], *[w[n] for n in LRU_PARAMS])


def _ret_prompt_kernel(q_ref, k_ref, v_ref, g_ref, cos_ref, sin_ref, decay_ref, qdec_ref, kdec_ref,
                       cdec_ref, ng_ref, y_ref, s_ref, state):
    c = pl.program_id(1)

    @pl.when(c == 0)
    def _():
        state[...] = jnp.zeros_like(state)

    head = _head_of_qk_lane((CHUNK, D_QK), 1)
    own_block = (_head_of_qk_lane((D_QK, D_RET), 0)
                 == lax.broadcasted_iota(jnp.int32, (D_QK, D_RET), 1) // RET_DV)
    for ci in range(q_ref.shape[0] // CHUNK):
        rows = slice(ci * CHUNK, (ci + 1) * CHUNK)
        cos = cos_ref[rows, :]
        sin = sin_ref[rows, :]
        q = _rotary_split(q_ref[rows, :], cos, sin)
        k = _rotary_split(k_ref[rows, :], cos, sin) * (RET_DK ** -0.5)
        kb = k.astype(BF16)
        vb = v_ref[rows, :].astype(BF16)
        s_old = state[...]
        o_inter = _dot((q * qdec_ref[...]).astype(BF16), s_old.astype(BF16))
        upd = _dot_tn((k * kdec_ref[...]).astype(BF16), vb)
        state[...] = s_old * cdec_ref[...] + jnp.where(own_block, upd, 0.0)
        for h in range(RET_HEADS):
            cols = slice(h * RET_DV, (h + 1) * RET_DV)
            scores = _dot_nt(jnp.where(head == h, q, 0.0).astype(BF16), kb) * decay_ref[h]
            o = _dot(scores.astype(BF16), vb[:, cols]) + o_inter[:, cols]
            o = _rms_rows(o, ng_ref[h:h + 1, :])
            y_ref[rows, cols] = (_silu(g_ref[rows, cols]) * o).astype(y_ref.dtype)
    half = RET_DK // 2
    for h in range(RET_HEADS):
        cols = slice(h * RET_DV, (h + 1) * RET_DV)
        s_ref[0, h, 0:half, :] = state[h * half:(h + 1) * half, cols]
        s_ref[0, h, half:RET_DK, :] = state[LANES + h * half:LANES + (h + 1) * half, cols]


def _ret_prompt(proj, bsz, seqlen, w, layer, tabs, tr):
    nc = seqlen // tr
    row = lambda b, c: b * nc + c
    const2 = lambda shape: pl.BlockSpec(shape, lambda b, c: (0, 0))
    const3 = lambda shape: pl.BlockSpec(shape, lambda b, c: (0, 0, 0))
    return pl.pallas_call(
        _ret_prompt_kernel,
        grid=(bsz, nc),
        in_specs=[
            pl.BlockSpec((tr, D_QK), lambda b, c: (row(b, c), (COL_Q - PROMPT_COL0) // D_QK)),
            pl.BlockSpec((tr, D_QK), lambda b, c: (row(b, c), (COL_K - PROMPT_COL0) // D_QK)),
            pl.BlockSpec((tr, D_RET), lambda b, c: (row(b, c), (COL_V - PROMPT_COL0) // D_RET)),
            pl.BlockSpec((tr, D_RET), lambda b, c: (row(b, c), (COL_RG - PROMPT_COL0) // D_RET)),
            pl.BlockSpec((tr, LANES), lambda b, c: (c, 0)),
            pl.BlockSpec((tr, LANES), lambda b, c: (c, 0)),
            const3((RET_HEADS, CHUNK, CHUNK)),
            const2((CHUNK, D_QK)),
            const2((CHUNK, D_QK)),
            const2((1, D_RET)),
            _lspec(w["ret_norm_g"], layer),
        ],
        out_specs=[
            pl.BlockSpec((tr, D_RET), lambda b, c: (row(b, c), 0)),
            pl.BlockSpec((1, RET_HEADS, RET_DK, RET_DV), lambda b, c: (b, 0, 0, 0)),
        ],
        out_shape=[
            jax.ShapeDtypeStruct((bsz * seqlen, D_RET), BF16),
            jax.ShapeDtypeStruct((bsz, RET_HEADS, RET_DK, RET_DV), F32),
        ],
        scratch_shapes=[pltpu.VMEM((D_QK, D_RET), F32)],
        compiler_params=_cparams("arbitrary", "arbitrary"),
        name="ret_prompt",
    )(proj, proj, proj, proj, tabs["cos_p"], tabs["sin_p"], tabs["decay"], tabs["qdec"],
      tabs["kdec"], tabs["cdec"], w["ret_norm_g"])


HEADS_PER_GROUP = SSD_HEADS // SSD_GROUPS
PAIRS = SSD_HEADS // 2


def _ssd_tile(z_ref, xbc_ref, dt_ref, cw_ref, cb_ref, dtb_ref, alog_ref, dfull_ref, ng_ref,
              tril_ref, conv_ref, s_ref, xbuf, state, ybuf, reset):
    tr = xbc_ref.shape[0]
    x = xbc_ref[...]
    halo_in = jnp.where(reset, 0.0, xbuf[...])
    xc = _causal_conv(x, halo_in, [cw_ref[k:k + 1, :] for k in range(CONV_W)], cb_ref[...])
    tail = x[tr - SUBLANES:, :]
    xbuf[...] = tail
    conv_ref[0] = tail[SUBLANES - (CONV_W - 1):, :]
    state[...] = jnp.where(reset, 0.0, state[...])
    xc = _silu(xc)
    xs_all = xc[:, :D_SSD]
    bm_all = xc[:, D_SSD:D_SSD + SSD_BC].astype(BF16)
    cm_all = xc[:, D_SSD + SSD_BC:].astype(BF16)

    dt_all = _softplus(dt_ref[...] + dtb_ref[...])
    da_all = dt_all * (-jnp.exp(alog_ref[...]))

    causal = (lax.broadcasted_iota(jnp.int32, (CHUNK, CHUNK), 0)
              >= lax.broadcasted_iota(jnp.int32, (CHUNK, CHUNK), 1))
    lane = lax.broadcasted_iota(jnp.int32, (CHUNK, LANES), 1)
    lo = lane < SSD_HD
    rowi = lax.broadcasted_iota(jnp.int32, (2 * SSD_HD, SSD_DSTATE), 0)
    row_lo = rowi < SSD_HD

    for ci in range(tr // CHUNK):
        rows = slice(ci * CHUNK, (ci + 1) * CHUNK)
        xs, bm, cm, dt = xs_all[rows], bm_all[rows], cm_all[rows], dt_all[rows]
        cum = _dot_exact(tril_ref[...], da_all[rows])
        cum_t = cum.T
        dt_t = dt.T
        ecum = jnp.exp(cum)
        wlast = jnp.exp(cum[CHUNK - 1:CHUNK, :] - cum) * dt
        elast = jnp.exp(cum_t[:, CHUNK - 1:CHUNK])
        cbw = [_dot_nt(cm[:, g * SSD_DSTATE:(g + 1) * SSD_DSTATE], bm[:, g * SSD_DSTATE:(g + 1) * SSD_DSTATE])
               for g in range(SSD_GROUPS)]
        for p in range(PAIRS):
            g = (2 * p) // HEADS_PER_GROUP
            xp = xs[:, p * LANES:(p + 1) * LANES]
            bg = bm[:, g * SSD_DSTATE:(g + 1) * SSD_DSTATE]
            cg = cm[:, g * SSD_DSTATE:(g + 1) * SSD_DSTATE]
            y = None
            for j, sel in ((0, lo), (1, jnp.logical_not(lo))):
                h = 2 * p + j
                seg = cum[:, h:h + 1] - cum_t[h:h + 1, :]
                lmask = jnp.exp(jnp.where(causal, seg, -jnp.inf))
                wgt = cbw[g] * lmask * dt_t[h:h + 1, :]
                yj = _dot(wgt.astype(BF16), jnp.where(sel, xp, 0.0).astype(BF16))
                y = yj if y is None else y + yj
            h0, h1 = 2 * p, 2 * p + 1
            sp = state[p]
            y = y + _dot_nt(cg, sp.astype(BF16)) * jnp.where(lo, ecum[:, h0:h0 + 1], ecum[:, h1:h1 + 1])
            wl = jnp.where(lo, wlast[:, h0:h0 + 1], wlast[:, h1:h1 + 1])
            state[p] = sp * jnp.where(row_lo, elast[h0:h0 + 1, :], elast[h1:h1 + 1, :]) + _dot_tn(
                (xp * wl).astype(BF16), bg)
            ybuf[rows, p * LANES:(p + 1) * LANES] = y + dfull_ref[:, p * LANES:(p + 1) * LANES] * xp
    for p in range(PAIRS):
        s_ref[0, 2 * p] = state[p, :SSD_HD, :]
        s_ref[0, 2 * p + 1] = state[p, SSD_HD:, :]
    return _ssd_gate_norm(ybuf[...], z_ref[...], ng_ref[...])


def _ssd_gate_norm(y, z, ng):
    gated = y * _silu(z)
    gw = D_SSD // SSD_GROUPS
    outs = []
    for g in range(SSD_GROUPS):
        outs.append(_rms_rows(gated[:, g * gw:(g + 1) * gw], ng[:, g * gw:(g + 1) * gw]))
    return jnp.concatenate(outs, axis=1)


SSD_PROMPT_PARAMS = ("ssd_conv_w", "ssd_conv_b", "ssd_dt_bias", "ssd_a_log", "ssd_d_full", "ssd_norm_g")


def _ssd_prompt_kernel(z_ref, xbc_ref, dt_ref, cw_ref, cb_ref, dtb_ref, alog_ref, dfull_ref, ng_ref,
                       tril_ref, y_ref, conv_ref, s_ref, xbuf, state, ybuf):
    c = pl.program_id(1)

    @pl.when(jnp.logical_and(pl.program_id(0) == 0, c == 0))
    def _():
        xbuf[...] = jnp.zeros_like(xbuf)
        state[...] = jnp.zeros_like(state)

    y_ref[...] = _ssd_tile(z_ref, xbc_ref, dt_ref, cw_ref, cb_ref, dtb_ref, alog_ref, dfull_ref, ng_ref,
                           tril_ref, conv_ref, s_ref, xbuf, state, ybuf, c == 0).astype(y_ref.dtype)


def _ssd_prompt(proj, bsz, seqlen, w, layer, tabs, tr):
    nc = seqlen // tr
    row = lambda b, c: b * nc + c
    const2 = lambda shape: pl.BlockSpec(shape, lambda b, c: (0, 0))
    return pl.pallas_call(
        _ssd_prompt_kernel,
        grid=(bsz, nc),
        in_specs=[
            pl.BlockSpec((tr, D_SSD), lambda b, c: (row(b, c), (COL_Z - PROMPT_COL0) // D_SSD)),
            pl.BlockSpec((tr, SSD_CONV_DIM), lambda b, c: (row(b, c), (COL_XBC - PROMPT_COL0) // SSD_CONV_DIM)),
            pl.BlockSpec((tr, LANES), lambda b, c: (row(b, c), (COL_DT - PROMPT_COL0) // LANES)),
        ] + [_lspec(w[n], layer) for n in SSD_PROMPT_PARAMS] + [const2((CHUNK, CHUNK))],
        out_specs=[
            pl.BlockSpec((tr, D_SSD), lambda b, c: (row(b, c), 0)),
            pl.BlockSpec((1, CONV_W - 1, SSD_CONV_DIM), lambda b, c: (b, 0, 0)),
            pl.BlockSpec((1, SSD_HEADS, SSD_HD, SSD_DSTATE), lambda b, c: (b, 0, 0, 0)),
        ],
        out_shape=[
            jax.ShapeDtypeStruct((bsz * seqlen, D_SSD), BF16),
            jax.ShapeDtypeStruct((bsz, CONV_W - 1, SSD_CONV_DIM), F32),
            jax.ShapeDtypeStruct((bsz, SSD_HEADS, SSD_HD, SSD_DSTATE), F32),
        ],
        scratch_shapes=[
            pltpu.VMEM((SUBLANES, SSD_CONV_DIM), F32),
            pltpu.VMEM((PAIRS, 2 * SSD_HD, SSD_DSTATE), F32),
            pltpu.VMEM((tr, D_SSD), F32),
        ],
        compiler_params=_cparams("arbitrary", "arbitrary"),
        name="ssd_prompt",
    )(proj, proj, proj, *[w[n] for n in SSD_PROMPT_PARAMS], tabs["tril"])


FF_TILES = ((0, 2816),)
FFN_PARAMS = ("w_out", "norm_ffn_g", "ffn_w_up", "ffn_conv_w", "ffn_conv_b", "ffn_w_down")


def _ffn_body(x, ya, yb, yc, wo_ref, g_ref, wup_ref, cw_ref, cb_ref, wdn_ref, fg_ref, conv_fn, final_norm):
    x1 = x + _dot(ya, wo_ref[0:D_LRU, :])
    x1 = x1 + _dot(yb, wo_ref[D_LRU:D_LRU + D_RET, :])
    x1 = x1 + _dot(yc, wo_ref[D_LRU + D_RET:, :])
    hn = _rms_rows(x1, g_ref[...]).astype(BF16)
    acc = None
    for f0, fw in FF_TILES:
        gate = _dot(hn, wup_ref[:, f0:f0 + fw])
        up = _dot(hn, wup_ref[:, D_FF + f0:D_FF + f0 + fw])
        taps = [cw_ref[k:k + 1, f0:f0 + fw] for k in range(FFN_CONV_W)]
        conv = conv_fn(gate, f0, fw, taps, cb_ref[:, f0:f0 + fw])
        act = (_gelu(conv) * up).astype(BF16)
        d = _dot(act, wdn_ref[f0:f0 + fw, :])
        acc = d if acc is None else acc + d
    acc = x1 + acc
    if final_norm:
        acc = _rms_rows(acc, fg_ref[...])
    return acc


def _ffn_decode_kernel(x_ref, ya_ref, yb_ref, yc_ref, wo_ref, g_ref, wup_ref, cw_ref, cb_ref, wdn_ref, fg_ref,
                       st_ref, o_ref, conv_ref, *, final_norm):
    def conv_fn(gate, f0, fw, taps, cb):
        s0 = st_ref[:, f0:f0 + fw]
        s1 = st_ref[:, D_FF + f0:D_FF + f0 + fw]
        conv_ref[:, f0:f0 + fw] = s1
        conv_ref[:, D_FF + f0:D_FF + f0 + fw] = gate
        return cb + taps[0] * s0 + taps[1] * s1 + taps[2] * gate

    o_ref[...] = _ffn_body(x_ref[...], ya_ref[...], yb_ref[...], yc_ref[...], wo_ref, g_ref, wup_ref,
                           cw_ref, cb_ref, wdn_ref, fg_ref, conv_fn, final_norm)


def _ffn_prompt_kernel(x_ref, ya_ref, yb_ref, yc_ref, wo_ref, g_ref, wup_ref, cw_ref, cb_ref, wdn_ref, fg_ref,
                       o_ref, fconv_ref, halo, *, final_norm):
    tm = x_ref.shape[0]

    @pl.when(pl.program_id(1) == 0)
    def _():
        halo[...] = jnp.zeros_like(halo)

    def conv_fn(gate, f0, fw, taps, cb):
        conv = _causal_conv(gate, halo[:, f0:f0 + fw], taps, cb)
        tail = gate[tm - SUBLANES:, :]
        halo[:, f0:f0 + fw] = tail
        fconv_ref[0, :, f0:f0 + fw] = tail[SUBLANES - (FFN_CONV_W - 1):, :]
        return conv

    o_ref[...] = _ffn_body(x_ref[...], ya_ref[...], yb_ref[...], yc_ref[...], wo_ref, g_ref, wup_ref,
                           cw_ref, cb_ref, wdn_ref, fg_ref, conv_fn, final_norm)


def _ffn_decode(x2d, ya, yb, yc, w, layer, fg, st, *, final_norm):
    t = x2d.shape[0]
    rowmap = lambda i: (0, 0)
    blk = lambda width: pl.BlockSpec((t, width), rowmap)
    return pl.pallas_call(
        functools.partial(_ffn_decode_kernel, final_norm=final_norm),
        grid=(1,),
        in_specs=[blk(D_MODEL), blk(D_LRU), blk(D_RET), blk(D_SSD)]
        + [_lspec(w[n], layer) for n in FFN_PARAMS]
        + [pl.BlockSpec((1, D_MODEL), rowmap),
           pl.BlockSpec((None, t, 2 * D_FF), lambda i: (layer, 0, 0))],
        out_specs=[blk(D_MODEL), blk(2 * D_FF)],
        out_shape=[jax.ShapeDtypeStruct((t, D_MODEL), F32), jax.ShapeDtypeStruct((t, 2 * D_FF), F32)],
        compiler_params=_cparams("arbitrary"),
        name="ffn_decode",
    )(x2d, ya, yb, yc, *[w[n] for n in FFN_PARAMS], fg, st)


def _ffn_prompt(x2d, ya, yb, yc, w, layer, fg, *, bsz, seqlen, tm, final_norm):
    nt = seqlen // tm
    rows = lambda width: pl.BlockSpec((tm, width), lambda b, i: (b * nt + i, 0))
    return pl.pallas_call(
        functools.partial(_ffn_prompt_kernel, final_norm=final_norm),
        grid=(bsz, nt),
        in_specs=[rows(D_MODEL), rows(D_LRU), rows(D_RET), rows(D_SSD)]
        + [_lspec(w[n], layer) for n in FFN_PARAMS] + [pl.BlockSpec((1, D_MODEL), lambda b, i: (0, 0))],
        out_specs=[rows(D_MODEL), pl.BlockSpec((1, FFN_CONV_W - 1, D_FF), lambda b, i: (b, 0, 0))],
        out_shape=[
            jax.ShapeDtypeStruct((bsz * seqlen, D_MODEL), F32),
            jax.ShapeDtypeStruct((bsz, FFN_CONV_W - 1, D_FF), F32),
        ],
        scratch_shapes=[pltpu.VMEM((SUBLANES, D_FF), F32)],
        compiler_params=_cparams("arbitrary", "arbitrary"),
        name="ffn_prompt",
    )(x2d, ya, yb, yc, *[w[n] for n in FFN_PARAMS], fg)


DEC_TILE = 16


MIX_DECODE_INPUTS = 35


def _mix_decode_kernel(*refs, n_alias):
    _mix_decode_body(*refs[:MIX_DECODE_INPUTS], *refs[MIX_DECODE_INPUTS + n_alias:])


def _mix_decode_body(lx_ref, lg_ref, q_ref, k_ref, v_ref, rg_ref, z_ref, xbc_ref, dt_ref,
                       lconv_ref, lh_ref, rs_ref, sconv_ref, ss_ref,
                       lcw_ref, lcb_ref, wr_ref, br_ref, wi_ref, bi_ref, lam_ref,
                       cos_ref, sin_ref, cdec_ref, rng_ref,
                       scw_ref, scb_ref, dtb_ref, alog_ref, dfull_ref, sng_ref,
                       eye_ref, hexp_ref, hexp_tile_ref, rsel_ref,
                       ya_ref, yb_ref, yc_ref, lconv_o, lh_o, rs_o, sconv_o, ss_o,
                       obuf, ybuf):
    x = lx_ref[...]
    st = lconv_ref[...]
    xc = lcb_ref[...] + lcw_ref[0:1, :] * st[:, 0:D_LRU] + lcw_ref[1:2, :] * st[:, D_LRU:2 * D_LRU]
    xc = xc + lcw_ref[2:3, :] * st[:, 2 * D_LRU:] + lcw_ref[3:4, :] * x
    lconv_o[:, 0:2 * D_LRU] = st[:, D_LRU:]
    lconv_o[:, 2 * D_LRU:] = x
    a, b = _lru_gates(xc, wr_ref, br_ref, wi_ref, bi_ref, lam_ref)
    h = b + a * lh_ref[...]
    lh_o[...] = h
    ya_ref[...] = (h * _gelu(lg_ref[...])).astype(ya_ref.dtype)

    cos = cos_ref[...]
    sin = sin_ref[...]
    q = _rotary_split(q_ref[...], cos, sin)
    k = _rotary_split(k_ref[...], cos, sin) * (RET_DK ** -0.5)
    v = v_ref[...]

    xin = xbc_ref[...]
    sst = sconv_ref[...]
    n = SSD_CONV_DIM
    xc = scb_ref[...] + scw_ref[0:1, :] * sst[:, 0:n] + scw_ref[1:2, :] * sst[:, n:2 * n]
    xc = xc + scw_ref[2:3, :] * sst[:, 2 * n:] + scw_ref[3:4, :] * xin
    sconv_o[:, 0:2 * n] = sst[:, n:]
    sconv_o[:, 2 * n:] = xin
    xc = _silu(xc)
    xs = xc[:, :D_SSD]
    bm = xc[:, D_SSD:D_SSD + SSD_BC]
    cm_b = xc[:, D_SSD + SSD_BC:].astype(BF16)
    dt = _softplus(dt_ref[...] + dtb_ref[...])
    dt_full = _dot_exact(dt, hexp_ref[...])
    eda_tiles = _dot_exact(jnp.exp(dt * (-jnp.exp(alog_ref[...]))), hexp_tile_ref[...])

    cols_src = jnp.concatenate([xs * dt_full, k], axis=1)
    cols_src = jnp.concatenate(
        [cols_src, jnp.zeros((LANES - DEC_TILE, cols_src.shape[1]), F32)], axis=0).astype(BF16)
    eye = eye_ref[...]
    cols_t = jnp.concatenate(
        [_dot_nt(eye, cols_src[:, j * LANES:(j + 1) * LANES]) for j in range(cols_src.shape[1] // LANES)],
        axis=0).astype(BF16)
    k_row0 = D_SSD

    head_q = _head_of_qk_lane((DEC_TILE, D_QK), 1)
    q_heads = jnp.concatenate([jnp.where(head_q == hh, q, 0.0) for hh in range(RET_HEADS)],
                              axis=0).astype(BF16)
    half = RET_DK // 2
    for r in range(DEC_TILE):
        colb = _dot(cols_t, rsel_ref[r])
        pieces = []
        for hh in range(SSD_HEADS):
            g = hh // HEADS_PER_GROUP
            e = jnp.broadcast_to(eda_tiles[r:r + 1, hh * LANES:(hh + 1) * LANES], (SSD_HD, SSD_DSTATE))
            brow = bm[r:r + 1, g * SSD_DSTATE:(g + 1) * SSD_DSTATE]
            s_new = ss_ref[0, r, hh] * e + colb[hh * SSD_HD:(hh + 1) * SSD_HD, :] * brow
            ss_o[0, r, hh] = s_new
            pieces.append(s_new.astype(BF16))
        gw = D_SSD // SSD_GROUPS
        for g in range(SSD_GROUPS):
            sg = jnp.concatenate(pieces[g * HEADS_PER_GROUP:(g + 1) * HEADS_PER_GROUP], axis=0)
            yg = _dot_nt(cm_b[:, g * SSD_DSTATE:(g + 1) * SSD_DSTATE], sg)
            ybuf[r:r + 1, g * gw:(g + 1) * gw] = yg[r:r + 1, :]
        pieces = []
        for part in range(2):
            for hh in range(RET_HEADS):
                r0 = k_row0 + part * LANES + hh * half
                vrow = v[r:r + 1, hh * RET_DV:(hh + 1) * RET_DV]
                s_old = rs_ref[0, r, hh, part * half:(part + 1) * half, :]
                s_new = s_old * cdec_ref[hh] + colb[r0:r0 + half, :] * vrow
                rs_o[0, r, hh, part * half:(part + 1) * half, :] = s_new
                pieces.append(s_new.astype(BF16))
        s_perm = jnp.concatenate(pieces, axis=0)
        res = _dot(q_heads, s_perm)
        for hh in range(RET_HEADS):
            obuf[r:r + 1, hh * RET_DV:(hh + 1) * RET_DV] = res[hh * DEC_TILE + r:hh * DEC_TILE + r + 1, :]

    o = obuf[...]
    rg = rg_ref[...]
    for hh in range(RET_HEADS):
        oh = _rms_rows(o[:, hh * RET_DV:(hh + 1) * RET_DV], rng_ref[hh:hh + 1, :])
        yb_ref[:, hh * RET_DV:(hh + 1) * RET_DV] = (
            _silu(rg[:, hh * RET_DV:(hh + 1) * RET_DV]) * oh).astype(yb_ref.dtype)
    y = ybuf[...] + dfull_ref[...] * xs
    yc_ref[...] = _ssd_gate_norm(y, z_ref[...], sng_ref[...]).astype(yc_ref.dtype)


SSD_DECODE_PARAMS = SSD_PROMPT_PARAMS


def _mix_decode(proj, states, layer, w, tabs, prev_full):
    nb = proj.shape[0]
    depth = states[2].shape[0]
    grid = (nb // DEC_TILE,)
    col = lambda width, c0: pl.BlockSpec((DEC_TILE, width), lambda i: (i, c0 // width))
    const2 = lambda shape: pl.BlockSpec(shape, lambda i: (0, 0))
    const3 = lambda shape: pl.BlockSpec(shape, lambda i: (0, 0, 0))
    lconv, lh, rs, sconv, ss = states
    ret_blk = pl.BlockSpec((1, DEC_TILE, RET_HEADS, RET_DK, RET_DV), lambda i: (layer, i, 0, 0, 0))
    ssd_blk = pl.BlockSpec((1, DEC_TILE, SSD_HEADS, SSD_HD, SSD_DSTATE), lambda i: (layer, i, 0, 0, 0))
    in_specs = [
        col(D_LRU, COL_LRU_X), col(D_LRU, COL_LRU_G), col(D_QK, COL_Q), col(D_QK, COL_K),
        col(D_RET, COL_V), col(D_RET, COL_RG), col(D_SSD, COL_Z), col(SSD_CONV_DIM, COL_XBC),
        col(LANES, COL_DT),
        pl.BlockSpec((None, DEC_TILE, (CONV_W - 1) * D_LRU), lambda i: (layer, i, 0)),
        pl.BlockSpec((None, DEC_TILE, D_LRU), lambda i: (layer, i, 0)),
        ret_blk,
        pl.BlockSpec((None, DEC_TILE, (CONV_W - 1) * SSD_CONV_DIM), lambda i: (layer, i, 0)),
        ssd_blk,
    ] + [_lspec(w[n], layer) for n in LRU_PARAMS] + [
        const2((1, LANES)), const2((1, LANES)), const3((RET_HEADS, 1, 1)), _lspec(w["ret_norm_g"], layer),
    ] + [_lspec(w[n], layer) for n in SSD_DECODE_PARAMS] + [
        const2((LANES, LANES)), const2((LANES, D_SSD)), const2((LANES, SSD_HEADS * LANES)),
        const3((DEC_TILE, LANES, LANES)),
    ]
    args = [proj] * 9 + [lconv, lh, rs, sconv, ss] + [w[n] for n in LRU_PARAMS] + [
        tabs["cos_s"], tabs["sin_s"], tabs["cdec1"], w["ret_norm_g"]] + [
        w[n] for n in SSD_DECODE_PARAMS] + [tabs["eye"], tabs["hexp"], tabs["hexp_tile"], tabs["rsel"]]
    assert len(args) == MIX_DECODE_INPUTS
    aliases = {}
    if prev_full is not None:
        in_specs += [pl.BlockSpec(memory_space=pl.ANY)] * 2
        args += list(prev_full)
        aliases = {MIX_DECODE_INPUTS: 5, MIX_DECODE_INPUTS + 1: 7}
    out_specs = [
        pl.BlockSpec((DEC_TILE, D_LRU), lambda i: (i, 0)),
        pl.BlockSpec((DEC_TILE, D_RET), lambda i: (i, 0)),
        pl.BlockSpec((DEC_TILE, D_SSD), lambda i: (i, 0)),
        pl.BlockSpec((DEC_TILE, (CONV_W - 1) * D_LRU), lambda i: (i, 0)),
        pl.BlockSpec((DEC_TILE, D_LRU), lambda i: (i, 0)),
        ret_blk,
        pl.BlockSpec((DEC_TILE, (CONV_W - 1) * SSD_CONV_DIM), lambda i: (i, 0)),
        ssd_blk,
    ]
    out_shape = [
        jax.ShapeDtypeStruct((nb, D_LRU), BF16),
        jax.ShapeDtypeStruct((nb, D_RET), BF16),
        jax.ShapeDtypeStruct((nb, D_SSD), BF16),
        jax.ShapeDtypeStruct((nb, (CONV_W - 1) * D_LRU), F32),
        jax.ShapeDtypeStruct((nb, D_LRU), F32),
        jax.ShapeDtypeStruct((depth, nb, RET_HEADS, RET_DK, RET_DV), F32),
        jax.ShapeDtypeStruct((nb, (CONV_W - 1) * SSD_CONV_DIM), F32),
        jax.ShapeDtypeStruct((depth, nb, SSD_HEADS, SSD_HD, SSD_DSTATE), F32),
    ]
    return pl.pallas_call(
        functools.partial(_mix_decode_kernel, n_alias=len(aliases)),
        grid=grid, in_specs=in_specs, out_specs=out_specs, out_shape=out_shape,
        input_output_aliases=aliases,
        scratch_shapes=[pltpu.VMEM((DEC_TILE, D_RET), F32), pltpu.VMEM((DEC_TILE, D_SSD), F32)],
        compiler_params=_cparams("arbitrary"),
        name="mix_decode",
    )(*args)


def _qk_perm():
    half = RET_DK // 2
    idx = np.empty((D_QK,), np.int32)
    for part in range(2):
        for h in range(RET_HEADS):
            for j in range(half):
                idx[part * LANES + h * half + j] = h * RET_DK + part * half + j
    return idx


def _block_diag(wh):
    nh, hd, _ = wh.shape
    bands = [jnp.pad(wh[h], ((0, 0), (h * hd, (nh - 1 - h) * hd))) for h in range(nh)]
    return jnp.concatenate(bands, axis=0)


def _prep_weights(p):
    perm = _qk_perm()
    w_in = p["w_in"]
    depth = w_in.shape[0]
    w_qk = jnp.concatenate([w_in[:, :, COL_Q:COL_Q + D_QK][:, :, perm],
                            w_in[:, :, COL_K:COL_K + D_QK][:, :, perm]], axis=2).astype(BF16)
    n_dt = w_in.shape[2] - COL_DT
    w_dt = jnp.concatenate([w_in[:, :, COL_DT:], jnp.zeros((depth, D_MODEL, LANES - n_dt), w_in.dtype)],
                           axis=2).astype(BF16)
    row = lambda v: v[:, None, :]
    pad_lanes = lambda v: row(jnp.concatenate([v, jnp.zeros((depth, LANES - v.shape[1]), v.dtype)], axis=1))
    return {
        "norm_mix_g": row(p["norm_mix_g"]),
        "w_in": w_in.astype(BF16),
        "w_qk": w_qk,
        "w_dt": w_dt,
        "lru_conv_w": p["lru_conv_w"],
        "lru_conv_b": row(p["lru_conv_b"]),
        "lru_wr": jax.vmap(_block_diag)(p["lru_wr"]).astype(BF16),
        "lru_br": row(p["lru_br"]),
        "lru_wi": jax.vmap(_block_diag)(p["lru_wi"]).astype(BF16),
        "lru_bi": row(p["lru_bi"]),
        "lru_lambda": row(p["lru_lambda"]),
        "ret_norm_g": p["ret_norm_g"],
        "ssd_conv_w": p["ssd_conv_w"],
        "ssd_conv_b": row(p["ssd_conv_b"]),
        "ssd_dt_bias": pad_lanes(p["ssd_dt_bias"]),
        "ssd_a_log": pad_lanes(p["ssd_a_log"]),
        "ssd_d_full": row(jnp.repeat(p["ssd_d"], SSD_HD, axis=1)),
        "ssd_norm_g": row(p["ssd_norm_g"]),
        "w_out": p["w_out"].astype(BF16),
        "norm_ffn_g": row(p["norm_ffn_g"]),
        "ffn_w_up": p["ffn_w_up"].astype(BF16),
        "ffn_conv_w": p["ffn_conv_w"],
        "ffn_conv_b": row(p["ffn_conv_b"]),
        "ffn_w_down": p["ffn_w_down"].astype(BF16),
    }


def _tables(seqlen, past_len):
    half = RET_DK // 2
    freqs = ROPE_BASE ** (-np.arange(half, dtype=np.float64) / half)

    def cs(pos):
        ang = pos.astype(np.float64)[:, None] * freqs[None, :]
        return np.tile(np.cos(ang), (1, RET_HEADS)), np.tile(np.sin(ang), (1, RET_HEADS))

    cos_p, sin_p = cs(np.arange(seqlen))
    cos_s, sin_s = cs(past_len + np.arange(1))
    log_g = np.log1p(-np.exp2(-5.0 - np.arange(RET_HEADS, dtype=np.float64)))
    idx = np.arange(CHUNK, dtype=np.float64)
    diff = idx[:, None] - idx[None, :]
    decay = np.where(diff[None] >= 0, np.exp(diff[None] * log_g[:, None, None]), 0.0)
    q_dec = np.exp((idx[None, :] + 1.0) * log_g[:, None])
    k_dec = np.exp((CHUNK - 1.0 - idx[None, :]) * log_g[:, None])
    lanes_of = lambda t: np.tile(np.repeat(t.T, half, axis=1), (1, 2))
    f32 = lambda t: jnp.asarray(np.asarray(t, np.float32))
    hexp = np.arange(LANES)[:, None] == (np.arange(D_SSD)[None, :] // SSD_HD)
    hexp_tile = np.arange(LANES)[:, None] == (np.arange(SSD_HEADS * LANES)[None, :] // LANES)
    rsel = np.broadcast_to(np.arange(LANES)[None, :, None] == np.arange(DEC_TILE)[:, None, None],
                           (DEC_TILE, LANES, LANES))
    return {
        "cos_p": f32(cos_p), "sin_p": f32(sin_p), "cos_s": f32(cos_s), "sin_s": f32(sin_s),
        "decay": f32(decay), "qdec": f32(lanes_of(q_dec)), "kdec": f32(lanes_of(k_dec)),
        "cdec": f32(np.repeat(np.exp(CHUNK * log_g), RET_DV)[None, :]),
        "cdec1": f32(np.exp(log_g)[:, None, None]),
        "tril": f32(np.tril(np.ones((CHUNK, CHUNK)))),
        "eye": f32(np.eye(LANES)).astype(BF16),
        "hexp": f32(hexp),
        "hexp_tile": f32(hexp_tile),
        "rsel": f32(rsel).astype(BF16),
    }


TM_PROMPT = 512
TR_MIX = 512
TR_RET = 1024


def _prompt_stack(x, w, tabs, fg, tm=TM_PROMPT, tr=TR_MIX, tr_ret=TR_RET):
    bsz, seqlen, _ = x.shape
    depth = w["w_in"].shape[0]
    x2d = x.reshape(bsz * seqlen, D_MODEL)
    outs = [[] for _ in range(6)]
    for l in range(depth):
        proj, ya, lconv, lh = _inproj_lru_prompt(x2d, w, l, bsz=bsz, seqlen=seqlen, tm=tm)
        yb, rs = _ret_prompt(proj, bsz, seqlen, w, l, tabs, tr_ret)
        yc, sconv, ss = _ssd_prompt(proj, bsz, seqlen, w, l, tabs, tr)
        x2d, fconv = _ffn_prompt(x2d, ya, yb, yc, w, l, fg, bsz=bsz, seqlen=seqlen, tm=tm,
                                 final_norm=(l == depth - 1))
        for lst, val in zip(outs, (lconv, lh[:, 0, :], rs, sconv, ss, fconv)):
            lst.append(val)
    return x2d.reshape(bsz, seqlen, D_MODEL), [jnp.stack(o, axis=0) for o in outs]


def _sample_stack(x, states, w, tabs, fg):
    nb = x.shape[0]
    depth = w["w_in"].shape[0]
    x2d = x.reshape(nb, D_MODEL)
    st_lconv, st_lh, st_rs, st_sconv, st_ss, st_fconv = states
    lconv_in = st_lconv.reshape(depth, nb, (CONV_W - 1) * D_LRU)
    sconv_in = st_sconv.reshape(depth, nb, (CONV_W - 1) * SSD_CONV_DIM)
    fconv_in = st_fconv.reshape(depth, nb, (FFN_CONV_W - 1) * D_FF)
    outs = [[] for _ in range(4)]
    full = None
    for l in range(depth):
        proj = _inproj(x2d, w, l, tm=nb)
        ya, yb, yc, lconv, lh, rs_full, sconv, ss_full = _mix_decode(
            proj, (lconv_in, st_lh, st_rs, sconv_in, st_ss), l, w, tabs, full)
        full = (rs_full, ss_full)
        x2d, fconv = _ffn_decode(x2d, ya, yb, yc, w, l, fg, fconv_in, final_norm=(l == depth - 1))
        for lst, val in zip(outs, (lconv, lh, sconv, fconv)):
            lst.append(val)
    lconv, lh, sconv, fconv = [jnp.stack(o, axis=0) for o in outs]
    new_states = [lconv.reshape(depth, nb, CONV_W - 1, D_LRU), lh, full[0],
                  sconv.reshape(depth, nb, CONV_W - 1, SSD_CONV_DIM), full[1],
                  fconv.reshape(depth, nb, FFN_CONV_W - 1, D_FF)]
    return x2d.reshape(nb, 1, D_MODEL), new_states


def kernel(x_prompt, x_sample, state_lru_conv, state_lru_h, state_ret, state_ssd_conv, state_ssd, state_ffn_conv,
           norm_mix_g, w_in, lru_conv_w, lru_conv_b, lru_wr, lru_br, lru_wi, lru_bi, lru_lambda,
           ret_norm_g, ssd_conv_w, ssd_conv_b, ssd_dt_bias, ssd_a_log, ssd_d, ssd_norm_g,
           w_out, norm_ffn_g, ffn_w_up, ffn_conv_w, ffn_conv_b, ffn_w_down, norm_final_g):
    p = {"norm_mix_g": norm_mix_g, "w_in": w_in, "lru_conv_w": lru_conv_w, "lru_conv_b": lru_conv_b,
         "lru_wr": lru_wr, "lru_br": lru_br, "lru_wi": lru_wi, "lru_bi": lru_bi, "lru_lambda": lru_lambda,
         "ret_norm_g": ret_norm_g, "ssd_conv_w": ssd_conv_w, "ssd_conv_b": ssd_conv_b,
         "ssd_dt_bias": ssd_dt_bias, "ssd_a_log": ssd_a_log, "ssd_d": ssd_d, "ssd_norm_g": ssd_norm_g,
         "w_out": w_out, "norm_ffn_g": norm_ffn_g, "ffn_w_up": ffn_w_up, "ffn_conv_w": ffn_conv_w,
         "ffn_conv_b": ffn_conv_b, "ffn_w_down": ffn_w_down}
    weights = _prep_weights(p)
    past_len = 16384
    tabs = _tables(x_prompt.shape[1], past_len)
    fg = norm_final_g[None, :]

    y_p, ns_p = _prompt_stack(x_prompt, weights, tabs, fg)
    y_s, ns_s = _sample_stack(
        x_sample, (state_lru_conv, state_lru_h, state_ret, state_ssd_conv, state_ssd, state_ffn_conv),
        weights, tabs, fg)
    lru_conv_p, lru_h_p, ret_p, ssd_conv_p, ssd_p, ffn_conv_p = ns_p
    lru_conv_s, lru_h_s, ret_s, ssd_conv_s, ssd_s, ffn_conv_s = ns_s
    return (y_p, y_s, lru_conv_p, lru_conv_s, lru_h_p, lru_h_s, ret_p, ret_s,
            ssd_conv_p, ssd_conv_s, ssd_p, ssd_s, ffn_conv_p, ffn_conv_s)
```

```python
import functools

import jax
import jax.numpy as jnp
import numpy as np
from jax import lax
from jax.experimental import pallas as pl
from jax.experimental.pallas import tpu as pltpu

F32 = jnp.float32
BF16 = jnp.bfloat16

D_MODEL = 1024
DEPTH = 4
D_LRU = 512
LRU_HEADS = 8
LRU_HD = D_LRU // LRU_HEADS
LRU_C = 8.0
CONV_W = 4
RET_HEADS = 4
RET_DK = 64
RET_DV = 128
D_RET = RET_HEADS * RET_DV
D_QK = RET_HEADS * RET_DK
ROPE_BASE = 10000.0
SSD_HD = 64
D_SSD = 512
SSD_HEADS = 8
SSD_GROUPS = 2
SSD_DSTATE = 128
SSD_BC = SSD_GROUPS * SSD_DSTATE
SSD_CONV_DIM = D_SSD + 2 * SSD_BC
D_MIX = D_LRU + D_RET + D_SSD
D_FF = 2816
FFN_CONV_W = 3
EPS = 1e-6
CHUNK = 128
PAST_LEN = 16384

COL_LRU_X = 0
COL_LRU_G = 512
COL_Q = 1024
COL_K = 1280
COL_V = 1536
COL_RG = 2048
COL_Z = 2560
COL_XBC = 3072
COL_DT = 4096
D_IN_PAD = 4224
PROMPT_COL0 = COL_Q
D_IN_PROMPT = D_IN_PAD - PROMPT_COL0

LANES = 128
SUBLANES = 8
VMEM_LIMIT_BYTES = 56 * 1024 * 1024


def _cparams(*sem):
    return pltpu.CompilerParams(dimension_semantics=sem, vmem_limit_bytes=VMEM_LIMIT_BYTES)


def _dot(a, b):
    return jnp.dot(a, b, preferred_element_type=F32)


def _dot_nt(a, b):
    return lax.dot_general(a, b, (((1,), (1,)), ((), ())), preferred_element_type=F32)


def _dot_tn(a, b):
    return lax.dot_general(a, b, (((0,), (0,)), ((), ())), preferred_element_type=F32)


def _dot_exact(a, b):
    return jnp.dot(a, b, preferred_element_type=F32, precision=lax.Precision.HIGHEST)


def _rms_rows(x, g):
    return x * lax.rsqrt(jnp.mean(x * x, axis=-1, keepdims=True) + EPS) * g


def _sigmoid(x):
    return 1.0 / (1.0 + jnp.exp(-x))


def _silu(x):
    return x * _sigmoid(x)


def _gelu(x):
    return jax.nn.gelu(x, approximate=True)


def _softplus(x):
    return jnp.maximum(x, 0.0) + jnp.log1p(jnp.exp(-jnp.abs(x)))


IN_COL_CHUNK = 512


def _lspec(arr, layer):
    _, r, c = arr.shape
    return pl.BlockSpec((None, r, c), lambda *_: (layer, 0, 0), pipeline_mode=pl.Buffered(1))


def _inproj_kernel(x_ref, g_ref, wm_ref, wqk_ref, wdt_ref, o_ref):
    hn = _rms_rows(x_ref[...], g_ref[...]).astype(BF16)
    for c0 in range(0, COL_DT, IN_COL_CHUNK):
        if c0 == COL_Q:
            o_ref[:, c0:c0 + IN_COL_CHUNK] = _dot(hn, wqk_ref[...])
        else:
            o_ref[:, c0:c0 + IN_COL_CHUNK] = _dot(hn, wm_ref[:, c0:c0 + IN_COL_CHUNK])
    o_ref[:, COL_DT:] = _dot(hn, wdt_ref[...])


def _inproj(x2d, w, layer, tm):
    t = x2d.shape[0]
    return pl.pallas_call(
        _inproj_kernel,
        grid=(t // tm,),
        in_specs=[
            pl.BlockSpec((tm, D_MODEL), lambda i: (i, 0)),
            _lspec(w["norm_mix_g"], layer),
            _lspec(w["w_in"], layer),
            _lspec(w["w_qk"], layer),
            _lspec(w["w_dt"], layer),
        ],
        out_specs=pl.BlockSpec((tm, D_IN_PAD), lambda i: (i, 0)),
        out_shape=jax.ShapeDtypeStruct((t, D_IN_PAD), F32),
        compiler_params=_cparams("arbitrary"),
        name="inproj",
    )(x2d, w["norm_mix_g"], w["w_in"], w["w_qk"], w["w_dt"])


def _lru_gate_pre(xc, wr_ref, br_ref, wi_ref, bi_ref, lam_ref):
    xcb = xc.astype(BF16)
    return (_dot(xcb, wr_ref[...]) + br_ref[...], _dot(xcb, wi_ref[...]) + bi_ref[...],
            _softplus(-lam_ref[...]))


def _lru_gate_post(xc, r_pre, i_pre, sp):
    log_a = (-LRU_C * _sigmoid(r_pre)) * sp
    a = jnp.exp(log_a)
    b = jnp.sqrt(-jnp.tanh(log_a) * (a * a + 1.0)) * (_sigmoid(i_pre) * xc)
    return a, b


def _lru_gates(xc, wr_ref, br_ref, wi_ref, bi_ref, lam_ref):
    return _lru_gate_post(xc, *_lru_gate_pre(xc, wr_ref, br_ref, wi_ref, bi_ref, lam_ref))


def _scan_rows(a, b, h_in):
    n, cols = a.shape
    groups = n // SUBLANES
    a = a.reshape(groups, SUBLANES, cols)
    b = b.reshape(groups, SUBLANES, cols)
    row = lax.broadcasted_iota(jnp.int32, a.shape, 1)
    d = 1
    while d < SUBLANES:
        keep = row >= d
        a_sh = jnp.where(keep, pltpu.roll(a, d, axis=1), 1.0)
        b_sh = jnp.where(keep, pltpu.roll(b, d, axis=1), 0.0)
        b = a * b_sh + b
        a = a * a_sh
        d *= 2
    hs = []
    h_prev = h_in
    for g in range(groups):
        hg = b[g] + a[g] * h_prev
        hs.append(hg)
        h_prev = hg[SUBLANES - 1:SUBLANES, :]
    return jnp.concatenate(hs, axis=0)


def _causal_conv(x, halo, taps, bias):
    n, cols = x.shape
    groups = n // SUBLANES
    xe = jnp.concatenate([halo, x], axis=0).reshape(groups + 1, SUBLANES, cols)
    row = lax.broadcasted_iota(jnp.int32, (groups, SUBLANES, cols), 1)
    if len(taps) == 4:
        w0, w1, w2, w3 = taps
        r1 = pltpu.roll(xe, 1, axis=1)
        sx = jnp.concatenate([r1[0:1], jnp.where(row >= 1, r1[1:], r1[:-1])], axis=0)
        r2 = pltpu.roll(w1 * xe + w0 * sx, 2, axis=1)
        s2u = jnp.where(row >= 2, r2[1:], r2[:-1])
        return bias + w3 * x + (w2 * sx[1:] + s2u).reshape(n, cols)
    y = bias + taps[-1] * x
    for s in range(1, len(taps)):
        r = pltpu.roll(xe, s, axis=1)
        shifted = jnp.where(row >= s, r[1:], r[:-1]).reshape(n, cols)
        y = y + taps[-1 - s] * shifted
    return y


def _rotary_split(x, cos, sin):
    xa, xb = x[:, :LANES], x[:, LANES:]
    return jnp.concatenate([xa * cos - xb * sin, xa * sin + xb * cos], axis=1)


def _head_of_qk_lane(shape, axis):
    lane = lax.broadcasted_iota(jnp.int32, shape, axis)
    return (lane % LANES) // (RET_DK // 2)


def _lru_back(x, gate, front, conv_ref, h_ref, xbuf, hcar, reset, valid):
    xc, pre, halo_old, h_old = front
    tl = x.shape[0]
    a, b = _lru_gate_post(xc, *pre)
    h = _scan_rows(a, b, jnp.where(reset, 0.0, h_old))
    y = h * _gelu(gate)
    tail = jnp.where(valid, x[tl - SUBLANES:, :], halo_old)
    xbuf[...] = tail
    conv_ref[0] = tail[SUBLANES - (CONV_W - 1):, :]
    hlast = jnp.where(valid, h[tl - 1:tl, :], h_old)
    hcar[...] = hlast
    h_ref[0] = hlast
    return y


LRU_PARAMS = ("lru_conv_w", "lru_conv_b", "lru_wr", "lru_br", "lru_wi", "lru_bi", "lru_lambda")


def _inproj_lru_kernel(x_ref, g_ref, wm_ref, wqk_ref, wdt_ref,
                       cw_ref, cb_ref, wr_ref, br_ref, wi_ref, bi_ref, lam_ref,
                       o_ref, y_ref, conv_ref, h_ref, ring, xbuf, hcar, *, tiles_per_seq):
    g = pl.program_id(0)

    @pl.when(g == 0)
    def _():
        xbuf[...] = jnp.zeros_like(xbuf)
        hcar[...] = jnp.zeros_like(hcar)
        ring[...] = jnp.zeros_like(ring)

    valid = g > 0
    reset = jnp.logical_and((g + tiles_per_seq - 1) % tiles_per_seq == 0, valid)
    lru_x = ring[:, 0:D_LRU]
    lru_gate = ring[:, D_LRU:]
    halo_old = xbuf[...]
    h_old = hcar[...]
    xc = _causal_conv(lru_x, jnp.where(reset, 0.0, halo_old), [cw_ref[k:k + 1, :] for k in range(CONV_W)],
                      cb_ref[...])
    front = (xc, _lru_gate_pre(xc, wr_ref, br_ref, wi_ref, bi_ref, lam_ref), halo_old, h_old)

    hn = _rms_rows(x_ref[...], g_ref[...]).astype(BF16)
    for c0 in range(0, COL_DT, IN_COL_CHUNK):
        o0 = c0 - PROMPT_COL0
        if c0 == COL_Q:
            o_ref[:, o0:o0 + IN_COL_CHUNK] = _dot(hn, wqk_ref[...])
        elif c0 < PROMPT_COL0:
            ring[:, c0:c0 + IN_COL_CHUNK] = _dot(hn, wm_ref[:, c0:c0 + IN_COL_CHUNK])
        else:
            o_ref[:, o0:o0 + IN_COL_CHUNK] = _dot(hn, wm_ref[:, c0:c0 + IN_COL_CHUNK])
    o_ref[:, COL_DT - PROMPT_COL0:] = _dot(hn, wdt_ref[...])

    y_ref[...] = _lru_back(lru_x, lru_gate, front, conv_ref, h_ref, xbuf, hcar, reset, valid).astype(
        y_ref.dtype)


def _inproj_lru_prompt(x2d, w, layer, *, bsz, seqlen, tm):
    tps = seqlen // tm
    n_tiles = bsz * tps
    proj_tile = lambda g: jnp.minimum(g, n_tiles - 1)
    lru_tile = lambda g: jnp.maximum(g - 1, 0)
    in_names = ("norm_mix_g", "w_in", "w_qk", "w_dt")
    return pl.pallas_call(
        functools.partial(_inproj_lru_kernel, tiles_per_seq=tps),
        grid=(n_tiles + 1,),
        in_specs=[pl.BlockSpec((tm, D_MODEL), lambda g: (proj_tile(g), 0))]
        + [_lspec(w[n], layer) for n in in_names] + [_lspec(w[n], layer) for n in LRU_PARAMS],
        out_specs=[
            pl.BlockSpec((tm, D_IN_PROMPT), lambda g: (proj_tile(g), 0)),
            pl.BlockSpec((tm, D_LRU), lambda g: (lru_tile(g), 0)),
            pl.BlockSpec((1, CONV_W - 1, D_LRU), lambda g: (lru_tile(g) // tps, 0, 0)),
            pl.BlockSpec((1, 1, D_LRU), lambda g: (lru_tile(g) // tps, 0, 0)),
        ],
        out_shape=[
            jax.ShapeDtypeStruct((bsz * seqlen, D_IN_PROMPT), F32),
            jax.ShapeDtypeStruct((bsz * seqlen, D_LRU), BF16),
            jax.ShapeDtypeStruct((bsz, CONV_W - 1, D_LRU), F32),
            jax.ShapeDtypeStruct((bsz, 1, D_LRU), F32),
        ],
        scratch_shapes=[
            pltpu.VMEM((tm, 2 * D_LRU), F32),
            pltpu.VMEM((SUBLANES, D_LRU), F32),
            pltpu.VMEM((1, D_LRU), F32),
        ],
        compiler_params=_cparams("arbitrary"),
        name="inproj_lru_prompt",
    )(x2d, *[w[n] for n in in_names], *[w[n] for n in LRU_PARAMS])


def _ret_prompt_kernel(q_ref, k_ref, v_ref, g_ref, cos_ref, sin_ref, decay_ref, qdec_ref, kdec_ref,
                       cdec_ref, ng_ref, y_ref, s_ref, state):
    c = pl.program_id(1)

    @pl.when(c == 0)
    def _():
        state[...] = jnp.zeros_like(state)

    head = _head_of_qk_lane((CHUNK, D_QK), 1)
    own_block = (_head_of_qk_lane((D_QK, D_RET), 0)
                 == lax.broadcasted_iota(jnp.int32, (D_QK, D_RET), 1) // RET_DV)
    for ci in range(q_ref.shape[0] // CHUNK):
        rows = slice(ci * CHUNK, (ci + 1) * CHUNK)
        cos = cos_ref[rows, :]
        sin = sin_ref[rows, :]
        q = _rotary_split(q_ref[rows, :], cos, sin)
        k = _rotary_split(k_ref[rows, :], cos, sin) * (RET_DK ** -0.5)
        kb = k.astype(BF16)
        vb = v_ref[rows, :].astype(BF16)
        s_old = state[...]
        o_inter = _dot((q * qdec_ref[...]).astype(BF16), s_old.astype(BF16))
        upd = _dot_tn((k * kdec_ref[...]).astype(BF16), vb)
        state[...] = s_old * cdec_ref[...] + jnp.where(own_block, upd, 0.0)
        for h in range(RET_HEADS):
            cols = slice(h * RET_DV, (h + 1) * RET_DV)
            scores = _dot_nt(jnp.where(head == h, q, 0.0).astype(BF16), kb) * decay_ref[h]
            o = _dot(scores.astype(BF16), vb[:, cols]) + o_inter[:, cols]
            o = _rms_rows(o, ng_ref[h:h + 1, :])
            y_ref[rows, cols] = (_silu(g_ref[rows, cols]) * o).astype(y_ref.dtype)
    half = RET_DK // 2
    for h in range(RET_HEADS):
        cols = slice(h * RET_DV, (h + 1) * RET_DV)
        s_ref[0, h, 0:half, :] = state[h * half:(h + 1) * half, cols]
        s_ref[0, h, half:RET_DK, :] = state[LANES + h * half:LANES + (h + 1) * half, cols]


def _ret_prompt(proj, bsz, seqlen, w, layer, tabs, tr):
    nc = seqlen // tr
    row = lambda b, c: b * nc + c
    const2 = lambda shape: pl.BlockSpec(shape, lambda b, c: (0, 0))
    const3 = lambda shape: pl.BlockSpec(shape, lambda b, c: (0, 0, 0))
    return pl.pallas_call(
        _ret_prompt_kernel,
        grid=(bsz, nc),
        in_specs=[
            pl.BlockSpec((tr, D_QK), lambda b, c: (row(b, c), (COL_Q - PROMPT_COL0) // D_QK)),
            pl.BlockSpec((tr, D_QK), lambda b, c: (row(b, c), (COL_K - PROMPT_COL0) // D_QK)),
            pl.BlockSpec((tr, D_RET), lambda b, c: (row(b, c), (COL_V - PROMPT_COL0) // D_RET)),
            pl.BlockSpec((tr, D_RET), lambda b, c: (row(b, c), (COL_RG - PROMPT_COL0) // D_RET)),
            pl.BlockSpec((tr, LANES), lambda b, c: (c, 0)),
            pl.BlockSpec((tr, LANES), lambda b, c: (c, 0)),
            const3((RET_HEADS, CHUNK, CHUNK)),
            const2((CHUNK, D_QK)),
            const2((CHUNK, D_QK)),
            const2((1, D_RET)),
            _lspec(w["ret_norm_g"], layer),
        ],
        out_specs=[
            pl.BlockSpec((tr, D_RET), lambda b, c: (row(b, c), 0)),
            pl.BlockSpec((1, RET_HEADS, RET_DK, RET_DV), lambda b, c: (b, 0, 0, 0)),
        ],
        out_shape=[
            jax.ShapeDtypeStruct((bsz * seqlen, D_RET), BF16),
            jax.ShapeDtypeStruct((bsz, RET_HEADS, RET_DK, RET_DV), F32),
        ],
        scratch_shapes=[pltpu.VMEM((D_QK, D_RET), F32)],
        compiler_params=_cparams("arbitrary", "arbitrary"),
        name="ret_prompt",
    )(proj, proj, proj, proj, tabs["cos_p"], tabs["sin_p"], tabs["decay"], tabs["qdec"],
      tabs["kdec"], tabs["cdec"], w["ret_norm_g"])


HEADS_PER_GROUP = SSD_HEADS // SSD_GROUPS
PAIRS = SSD_HEADS // 2


def _ssd_tile(z_ref, xbc_ref, dt_ref, cw_ref, cb_ref, dtb_ref, alog_ref, dfull_ref, ng_ref,
              tril_ref, conv_ref, s_ref, xbuf, state, ybuf, reset):
    tr = xbc_ref.shape[0]
    x = xbc_ref[...]
    halo_in = jnp.where(reset, 0.0, xbuf[...])
    xc = _causal_conv(x, halo_in, [cw_ref[k:k + 1, :] for k in range(CONV_W)], cb_ref[...])
    tail = x[tr - SUBLANES:, :]
    xbuf[...] = tail
    conv_ref[0] = tail[SUBLANES - (CONV_W - 1):, :]
    state[...] = jnp.where(reset, 0.0, state[...])
    xc = _silu(xc)
    xs_all = xc[:, :D_SSD]
    bm_all = xc[:, D_SSD:D_SSD + SSD_BC].astype(BF16)
    cm_all = xc[:, D_SSD + SSD_BC:].astype(BF16)

    dt_all = _softplus(dt_ref[...] + dtb_ref[...])
    da_all = dt_all * (-jnp.exp(alog_ref[...]))

    causal = (lax.broadcasted_iota(jnp.int32, (CHUNK, CHUNK), 0)
              >= lax.broadcasted_iota(jnp.int32, (CHUNK, CHUNK), 1))
    lane = lax.broadcasted_iota(jnp.int32, (CHUNK, LANES), 1)
    lo = lane < SSD_HD
    rowi = lax.broadcasted_iota(jnp.int32, (2 * SSD_HD, SSD_DSTATE), 0)
    row_lo = rowi < SSD_HD

    for ci in range(tr // CHUNK):
        rows = slice(ci * CHUNK, (ci + 1) * CHUNK)
        xs, bm, cm, dt = xs_all[rows], bm_all[rows], cm_all[rows], dt_all[rows]
        cum = _dot_exact(tril_ref[...], da_all[rows])
        cum_t = cum.T
        dt_t = dt.T
        ecum = jnp.exp(cum)
        wlast = jnp.exp(cum[CHUNK - 1:CHUNK, :] - cum) * dt
        elast = jnp.exp(cum_t[:, CHUNK - 1:CHUNK])
        cbw = [_dot_nt(cm[:, g * SSD_DSTATE:(g + 1) * SSD_DSTATE], bm[:, g * SSD_DSTATE:(g + 1) * SSD_DSTATE])
               for g in range(SSD_GROUPS)]
        for p in range(PAIRS):
            g = (2 * p) // HEADS_PER_GROUP
            xp = xs[:, p * LANES:(p + 1) * LANES]
            bg = bm[:, g * SSD_DSTATE:(g + 1) * SSD_DSTATE]
            cg = cm[:, g * SSD_DSTATE:(g + 1) * SSD_DSTATE]
            y = None
            for j, sel in ((0, lo), (1, jnp.logical_not(lo))):
                h = 2 * p + j
                seg = cum[:, h:h + 1] - cum_t[h:h + 1, :]
                lmask = jnp.exp(jnp.where(causal, seg, -jnp.inf))
                wgt = cbw[g] * lmask * dt_t[h:h + 1, :]
                yj = _dot(wgt.astype(BF16), jnp.where(sel, xp, 0.0).astype(BF16))
                y = yj if y is None else y + yj
            h0, h1 = 2 * p, 2 * p + 1
            sp = state[p]
            y = y + _dot_nt(cg, sp.astype(BF16)) * jnp.where(lo, ecum[:, h0:h0 + 1], ecum[:, h1:h1 + 1])
            wl = jnp.where(lo, wlast[:, h0:h0 + 1], wlast[:, h1:h1 + 1])
            state[p] = sp * jnp.where(row_lo, elast[h0:h0 + 1, :], elast[h1:h1 + 1, :]) + _dot_tn(
                (xp * wl).astype(BF16), bg)
            ybuf[rows, p * LANES:(p + 1) * LANES] = y + dfull_ref[:, p * LANES:(p + 1) * LANES] * xp
    for p in range(PAIRS):
        s_ref[0, 2 * p] = state[p, :SSD_HD, :]
        s_ref[0, 2 * p + 1] = state[p, SSD_HD:, :]
    return _ssd_gate_norm(ybuf[...], z_ref[...], ng_ref[...])


def _ssd_gate_norm(y, z, ng):
    gated = y * _silu(z)
    gw = D_SSD // SSD_GROUPS
    outs = []
    for g in range(SSD_GROUPS):
        outs.append(_rms_rows(gated[:, g * gw:(g + 1) * gw], ng[:, g * gw:(g + 1) * gw]))
    return jnp.concatenate(outs, axis=1)


SSD_PROMPT_PARAMS = ("ssd_conv_w", "ssd_conv_b", "ssd_dt_bias", "ssd_a_log", "ssd_d_full", "ssd_norm_g")


def _ssd_prompt_kernel(z_ref, xbc_ref, dt_ref, cw_ref, cb_ref, dtb_ref, alog_ref, dfull_ref, ng_ref,
                       tril_ref, y_ref, conv_ref, s_ref, xbuf, state, ybuf):
    c = pl.program_id(1)

    @pl.when(jnp.logical_and(pl.program_id(0) == 0, c == 0))
    def _():
        xbuf[...] = jnp.zeros_like(xbuf)
        state[...] = jnp.zeros_like(state)

    y_ref[...] = _ssd_tile(z_ref, xbc_ref, dt_ref, cw_ref, cb_ref, dtb_ref, alog_ref, dfull_ref, ng_ref,
                           tril_ref, conv_ref, s_ref, xbuf, state, ybuf, c == 0).astype(y_ref.dtype)


def _ssd_prompt(proj, bsz, seqlen, w, layer, tabs, tr):
    nc = seqlen // tr
    row = lambda b, c: b * nc + c
    const2 = lambda shape: pl.BlockSpec(shape, lambda b, c: (0, 0))
    return pl.pallas_call(
        _ssd_prompt_kernel,
        grid=(bsz, nc),
        in_specs=[
            pl.BlockSpec((tr, D_SSD), lambda b, c: (row(b, c), (COL_Z - PROMPT_COL0) // D_SSD)),
            pl.BlockSpec((tr, SSD_CONV_DIM), lambda b, c: (row(b, c), (COL_XBC - PROMPT_COL0) // SSD_CONV_DIM)),
            pl.BlockSpec((tr, LANES), lambda b, c: (row(b, c), (COL_DT - PROMPT_COL0) // LANES)),
        ] + [_lspec(w[n], layer) for n in SSD_PROMPT_PARAMS] + [const2((CHUNK, CHUNK))],
        out_specs=[
            pl.BlockSpec((tr, D_SSD), lambda b, c: (row(b, c), 0)),
            pl.BlockSpec((1, CONV_W - 1, SSD_CONV_DIM), lambda b, c: (b, 0, 0)),
            pl.BlockSpec((1, SSD_HEADS, SSD_HD, SSD_DSTATE), lambda b, c: (b, 0, 0, 0)),
        ],
        out_shape=[
            jax.ShapeDtypeStruct((bsz * seqlen, D_SSD), BF16),
            jax.ShapeDtypeStruct((bsz, CONV_W - 1, SSD_CONV_DIM), F32),
            jax.ShapeDtypeStruct((bsz, SSD_HEADS, SSD_HD, SSD_DSTATE), F32),
        ],
        scratch_shapes=[
            pltpu.VMEM((SUBLANES, SSD_CONV_DIM), F32),
            pltpu.VMEM((PAIRS, 2 * SSD_HD, SSD_DSTATE), F32),
            pltpu.VMEM((tr, D_SSD), F32),
        ],
        compiler_params=_cparams("arbitrary", "arbitrary"),
        name="ssd_prompt",
    )(proj, proj, proj, *[w[n] for n in SSD_PROMPT_PARAMS], tabs["tril"])


FF_TILES = ((0, 2816),)
FFN_PARAMS = ("w_out", "norm_ffn_g", "ffn_w_up", "ffn_conv_w", "ffn_conv_b", "ffn_w_down")


def _ffn_body(x, ya, yb, yc, wo_ref, g_ref, wup_ref, cw_ref, cb_ref, wdn_ref, fg_ref, conv_fn, final_norm):
    x1 = x + _dot(ya, wo_ref[0:D_LRU, :])
    x1 = x1 + _dot(yb, wo_ref[D_LRU:D_LRU + D_RET, :])
    x1 = x1 + _dot(yc, wo_ref[D_LRU + D_RET:, :])
    hn = _rms_rows(x1, g_ref[...]).astype(BF16)
    acc = None
    for f0, fw in FF_TILES:
        gate = _dot(hn, wup_ref[:, f0:f0 + fw])
        up = _dot(hn, wup_ref[:, D_FF + f0:D_FF + f0 + fw])
        taps = [cw_ref[k:k + 1, f0:f0 + fw] for k in range(FFN_CONV_W)]
        conv = conv_fn(gate, f0, fw, taps, cb_ref[:, f0:f0 + fw])
        act = (_gelu(conv) * up).astype(BF16)
        d = _dot(act, wdn_ref[f0:f0 + fw, :])
        acc = d if acc is None else acc + d
    acc = x1 + acc
    if final_norm:
        acc = _rms_rows(acc, fg_ref[...])
    return acc


def _ffn_decode_kernel(x_ref, ya_ref, yb_ref, yc_ref, wo_ref, g_ref, wup_ref, cw_ref, cb_ref, wdn_ref, fg_ref,
                       st_ref, o_ref, conv_ref, *, final_norm):
    def conv_fn(gate, f0, fw, taps, cb):
        s0 = st_ref[:, f0:f0 + fw]
        s1 = st_ref[:, D_FF + f0:D_FF + f0 + fw]
        conv_ref[:, f0:f0 + fw] = s1
        conv_ref[:, D_FF + f0:D_FF + f0 + fw] = gate
        return cb + taps[0] * s0 + taps[1] * s1 + taps[2] * gate

    o_ref[...] = _ffn_body(x_ref[...], ya_ref[...], yb_ref[...], yc_ref[...], wo_ref, g_ref, wup_ref,
                           cw_ref, cb_ref, wdn_ref, fg_ref, conv_fn, final_norm)


def _ffn_prompt_kernel(x_ref, ya_ref, yb_ref, yc_ref, wo_ref, g_ref, wup_ref, cw_ref, cb_ref, wdn_ref, fg_ref,
                       o_ref, fconv_ref, halo, *, final_norm):
    tm = x_ref.shape[0]

    @pl.when(pl.program_id(1) == 0)
    def _():
        halo[...] = jnp.zeros_like(halo)

    def conv_fn(gate, f0, fw, taps, cb):
        conv = _causal_conv(gate, halo[:, f0:f0 + fw], taps, cb)
        tail = gate[tm - SUBLANES:, :]
        halo[:, f0:f0 + fw] = tail
        fconv_ref[0, :, f0:f0 + fw] = tail[SUBLANES - (FFN_CONV_W - 1):, :]
        return conv

    o_ref[...] = _ffn_body(x_ref[...], ya_ref[...], yb_ref[...], yc_ref[...], wo_ref, g_ref, wup_ref,
                           cw_ref, cb_ref, wdn_ref, fg_ref, conv_fn, final_norm)


def _ffn_decode(x2d, ya, yb, yc, w, layer, fg, st, *, final_norm):
    t = x2d.shape[0]
    rowmap = lambda i: (0, 0)
    blk = lambda width: pl.BlockSpec((t, width), rowmap)
    return pl.pallas_call(
        functools.partial(_ffn_decode_kernel, final_norm=final_norm),
        grid=(1,),
        in_specs=[blk(D_MODEL), blk(D_LRU), blk(D_RET), blk(D_SSD)]
        + [_lspec(w[n], layer) for n in FFN_PARAMS]
        + [pl.BlockSpec((1, D_MODEL), rowmap),
           pl.BlockSpec((None, t, 2 * D_FF), lambda i: (layer, 0, 0))],
        out_specs=[blk(D_MODEL), blk(2 * D_FF)],
        out_shape=[jax.ShapeDtypeStruct((t, D_MODEL), F32), jax.ShapeDtypeStruct((t, 2 * D_FF), F32)],
        compiler_params=_cparams("arbitrary"),
        name="ffn_decode",
    )(x2d, ya, yb, yc, *[w[n] for n in FFN_PARAMS], fg, st)


def _ffn_prompt(x2d, ya, yb, yc, w, layer, fg, *, bsz, seqlen, tm, final_norm):
    nt = seqlen // tm
    rows = lambda width: pl.BlockSpec((tm, width), lambda b, i: (b * nt + i, 0))
    return pl.pallas_call(
        functools.partial(_ffn_prompt_kernel, final_norm=final_norm),
        grid=(bsz, nt),
        in_specs=[rows(D_MODEL), rows(D_LRU), rows(D_RET), rows(D_SSD)]
        + [_lspec(w[n], layer) for n in FFN_PARAMS] + [pl.BlockSpec((1, D_MODEL), lambda b, i: (0, 0))],
        out_specs=[rows(D_MODEL), pl.BlockSpec((1, FFN_CONV_W - 1, D_FF), lambda b, i: (b, 0, 0))],
        out_shape=[
            jax.ShapeDtypeStruct((bsz * seqlen, D_MODEL), F32),
            jax.ShapeDtypeStruct((bsz, FFN_CONV_W - 1, D_FF), F32),
        ],
        scratch_shapes=[pltpu.VMEM((SUBLANES, D_FF), F32)],
        compiler_params=_cparams("arbitrary", "arbitrary"),
        name="ffn_prompt",
    )(x2d, ya, yb, yc, *[w[n] for n in FFN_PARAMS], fg)


DEC_TILE = 16


MIX_DECODE_INPUTS = 35


def _mix_decode_kernel(*refs, n_alias):
    _mix_decode_body(*refs[:MIX_DECODE_INPUTS], *refs[MIX_DECODE_INPUTS + n_alias:])


def _mix_decode_body(lx_ref, lg_ref, q_ref, k_ref, v_ref, rg_ref, z_ref, xbc_ref, dt_ref,
                       lconv_ref, lh_ref, rs_ref, sconv_ref, ss_ref,
                       lcw_ref, lcb_ref, wr_ref, br_ref, wi_ref, bi_ref, lam_ref,
                       cos_ref, sin_ref, cdec_ref, rng_ref,
                       scw_ref, scb_ref, dtb_ref, alog_ref, dfull_ref, sng_ref,
                       eye_ref, hexp_ref, hexp_tile_ref, rsel_ref,
                       ya_ref, yb_ref, yc_ref, lconv_o, lh_o, rs_o, sconv_o, ss_o,
                       obuf, ybuf):
    x = lx_ref[...]
    st = lconv_ref[...]
    xc = lcb_ref[...] + lcw_ref[0:1, :] * st[:, 0:D_LRU] + lcw_ref[1:2, :] * st[:, D_LRU:2 * D_LRU]
    xc = xc + lcw_ref[2:3, :] * st[:, 2 * D_LRU:] + lcw_ref[3:4, :] * x
    lconv_o[:, 0:2 * D_LRU] = st[:, D_LRU:]
    lconv_o[:, 2 * D_LRU:] = x
    a, b = _lru_gates(xc, wr_ref, br_ref, wi_ref, bi_ref, lam_ref)
    h = b + a * lh_ref[...]
    lh_o[...] = h
    ya_ref[...] = (h * _gelu(lg_ref[...])).astype(ya_ref.dtype)

    cos = cos_ref[...]
    sin = sin_ref[...]
    q = _rotary_split(q_ref[...], cos, sin)
    k = _rotary_split(k_ref[...], cos, sin) * (RET_DK ** -0.5)
    v = v_ref[...]

    xin = xbc_ref[...]
    sst = sconv_ref[...]
    n = SSD_CONV_DIM
    xc = scb_ref[...] + scw_ref[0:1, :] * sst[:, 0:n] + scw_ref[1:2, :] * sst[:, n:2 * n]
    xc = xc + scw_ref[2:3, :] * sst[:, 2 * n:] + scw_ref[3:4, :] * xin
    sconv_o[:, 0:2 * n] = sst[:, n:]
    sconv_o[:, 2 * n:] = xin
    xc = _silu(xc)
    xs = xc[:, :D_SSD]
    bm = xc[:, D_SSD:D_SSD + SSD_BC]
    cm_b = xc[:, D_SSD + SSD_BC:].astype(BF16)
    dt = _softplus(dt_ref[...] + dtb_ref[...])
    dt_full = _dot_exact(dt, hexp_ref[...])
    eda_tiles = _dot_exact(jnp.exp(dt * (-jnp.exp(alog_ref[...]))), hexp_tile_ref[...])

    cols_src = jnp.concatenate([xs * dt_full, k], axis=1)
    cols_src = jnp.concatenate(
        [cols_src, jnp.zeros((LANES - DEC_TILE, cols_src.shape[1]), F32)], axis=0).astype(BF16)
    eye = eye_ref[...]
    cols_t = jnp.concatenate(
        [_dot_nt(eye, cols_src[:, j * LANES:(j + 1) * LANES]) for j in range(cols_src.shape[1] // LANES)],
        axis=0).astype(BF16)
    k_row0 = D_SSD

    head_q = _head_of_qk_lane((DEC_TILE, D_QK), 1)
    q_heads = jnp.concatenate([jnp.where(head_q == hh, q, 0.0) for hh in range(RET_HEADS)],
                              axis=0).astype(BF16)
    half = RET_DK // 2
    for r in range(DEC_TILE):
        colb = _dot(cols_t, rsel_ref[r])
        pieces = []
        for hh in range(SSD_HEADS):
            g = hh // HEADS_PER_GROUP
            e = jnp.broadcast_to(eda_tiles[r:r + 1, hh * LANES:(hh + 1) * LANES], (SSD_HD, SSD_DSTATE))
            brow = bm[r:r + 1, g * SSD_DSTATE:(g + 1) * SSD_DSTATE]
            s_new = ss_ref[0, r, hh] * e + colb[hh * SSD_HD:(hh + 1) * SSD_HD, :] * brow
            ss_o[0, r, hh] = s_new
            pieces.append(s_new.astype(BF16))
        gw = D_SSD // SSD_GROUPS
        for g in range(SSD_GROUPS):
            sg = jnp.concatenate(pieces[g * HEADS_PER_GROUP:(g + 1) * HEADS_PER_GROUP], axis=0)
            yg = _dot_nt(cm_b[:, g * SSD_DSTATE:(g + 1) * SSD_DSTATE], sg)
            ybuf[r:r + 1, g * gw:(g + 1) * gw] = yg[r:r + 1, :]
        pieces = []
        for part in range(2):
            for hh in range(RET_HEADS):
                r0 = k_row0 + part * LANES + hh * half
                vrow = v[r:r + 1, hh * RET_DV:(hh + 1) * RET_DV]
                s_old = rs_ref[0, r, hh, part * half:(part + 1) * half, :]
                s_new = s_old * cdec_ref[hh] + colb[r0:r0 + half, :] * vrow
                rs_o[0, r, hh, part * half:(part + 1) * half, :] = s_new
                pieces.append(s_new.astype(BF16))
        s_perm = jnp.concatenate(pieces, axis=0)
        res = _dot(q_heads, s_perm)
        for hh in range(RET_HEADS):
            obuf[r:r + 1, hh * RET_DV:(hh + 1) * RET_DV] = res[hh * DEC_TILE + r:hh * DEC_TILE + r + 1, :]

    o = obuf[...]
    rg = rg_ref[...]
    for hh in range(RET_HEADS):
        oh = _rms_rows(o[:, hh * RET_DV:(hh + 1) * RET_DV], rng_ref[hh:hh + 1, :])
        yb_ref[:, hh * RET_DV:(hh + 1) * RET_DV] = (
            _silu(rg[:, hh * RET_DV:(hh + 1) * RET_DV]) * oh).astype(yb_ref.dtype)
    y = ybuf[...] + dfull_ref[...] * xs
    yc_ref[...] = _ssd_gate_norm(y, z_ref[...], sng_ref[...]).astype(yc_ref.dtype)


SSD_DECODE_PARAMS = SSD_PROMPT_PARAMS


def _mix_decode(proj, states, layer, w, tabs, prev_full):
    nb = proj.shape[0]
    depth = states[2].shape[0]
    grid = (nb // DEC_TILE,)
    col = lambda width, c0: pl.BlockSpec((DEC_TILE, width), lambda i: (i, c0 // width))
    const2 = lambda shape: pl.BlockSpec(shape, lambda i: (0, 0))
    const3 = lambda shape: pl.BlockSpec(shape, lambda i: (0, 0, 0))
    lconv, lh, rs, sconv, ss = states
    ret_blk = pl.BlockSpec((1, DEC_TILE, RET_HEADS, RET_DK, RET_DV), lambda i: (layer, i, 0, 0, 0))
    ssd_blk = pl.BlockSpec((1, DEC_TILE, SSD_HEADS, SSD_HD, SSD_DSTATE), lambda i: (layer, i, 0, 0, 0))
    in_specs = [
        col(D_LRU, COL_LRU_X), col(D_LRU, COL_LRU_G), col(D_QK, COL_Q), col(D_QK, COL_K),
        col(D_RET, COL_V), col(D_RET, COL_RG), col(D_SSD, COL_Z), col(SSD_CONV_DIM, COL_XBC),
        col(LANES, COL_DT),
        pl.BlockSpec((None, DEC_TILE, (CONV_W - 1) * D_LRU), lambda i: (layer, i, 0)),
        pl.BlockSpec((None, DEC_TILE, D_LRU), lambda i: (layer, i, 0)),
        ret_blk,
        pl.BlockSpec((None, DEC_TILE, (CONV_W - 1) * SSD_CONV_DIM), lambda i: (layer, i, 0)),
        ssd_blk,
    ] + [_lspec(w[n], layer) for n in LRU_PARAMS] + [
        const2((1, LANES)), const2((1, LANES)), const3((RET_HEADS, 1, 1)), _lspec(w["ret_norm_g"], layer),
    ] + [_lspec(w[n], layer) for n in SSD_DECODE_PARAMS] + [
        const2((LANES, LANES)), const2((LANES, D_SSD)), const2((LANES, SSD_HEADS * LANES)),
        const3((DEC_TILE, LANES, LANES)),
    ]
    args = [proj] * 9 + [lconv, lh, rs, sconv, ss] + [w[n] for n in LRU_PARAMS] + [
        tabs["cos_s"], tabs["sin_s"], tabs["cdec1"], w["ret_norm_g"]] + [
        w[n] for n in SSD_DECODE_PARAMS] + [tabs["eye"], tabs["hexp"], tabs["hexp_tile"], tabs["rsel"]]
    assert len(args) == MIX_DECODE_INPUTS
    aliases = {}
    if prev_full is not None:
        in_specs += [pl.BlockSpec(memory_space=pl.ANY)] * 2
        args += list(prev_full)
        aliases = {MIX_DECODE_INPUTS: 5, MIX_DECODE_INPUTS + 1: 7}
    out_specs = [
        pl.BlockSpec((DEC_TILE, D_LRU), lambda i: (i, 0)),
        pl.BlockSpec((DEC_TILE, D_RET), lambda i: (i, 0)),
        pl.BlockSpec((DEC_TILE, D_SSD), lambda i: (i, 0)),
        pl.BlockSpec((DEC_TILE, (CONV_W - 1) * D_LRU), lambda i: (i, 0)),
        pl.BlockSpec((DEC_TILE, D_LRU), lambda i: (i, 0)),
        ret_blk,
        pl.BlockSpec((DEC_TILE, (CONV_W - 1) * SSD_CONV_DIM), lambda i: (i, 0)),
        ssd_blk,
    ]
    out_shape = [
        jax.ShapeDtypeStruct((nb, D_LRU), BF16),
        jax.ShapeDtypeStruct((nb, D_RET), BF16),
        jax.ShapeDtypeStruct((nb, D_SSD), BF16),
        jax.ShapeDtypeStruct((nb, (CONV_W - 1) * D_LRU), F32),
        jax.ShapeDtypeStruct((nb, D_LRU), F32),
        jax.ShapeDtypeStruct((depth, nb, RET_HEADS, RET_DK, RET_DV), F32),
        jax.ShapeDtypeStruct((nb, (CONV_W - 1) * SSD_CONV_DIM), F32),
        jax.ShapeDtypeStruct((depth, nb, SSD_HEADS, SSD_HD, SSD_DSTATE), F32),
    ]
    return pl.pallas_call(
        functools.partial(_mix_decode_kernel, n_alias=len(aliases)),
        grid=grid, in_specs=in_specs, out_specs=out_specs, out_shape=out_shape,
        input_output_aliases=aliases,
        scratch_shapes=[pltpu.VMEM((DEC_TILE, D_RET), F32), pltpu.VMEM((DEC_TILE, D_SSD), F32)],
        compiler_params=_cparams("arbitrary"),
        name="mix_decode",
    )(*args)


def _qk_perm():
    half = RET_DK // 2
    idx = np.empty((D_QK,), np.int32)
    for part in range(2):
        for h in range(RET_HEADS):
            for j in range(half):
                idx[part * LANES + h * half + j] = h * RET_DK + part * half + j
    return idx


def _block_diag(wh):
    nh, hd, _ = wh.shape
    bands = [jnp.pad(wh[h], ((0, 0), (h * hd, (nh - 1 - h) * hd))) for h in range(nh)]
    return jnp.concatenate(bands, axis=0)


def _prep_weights(p):
    perm = _qk_perm()
    w_in = p["w_in"]
    depth = w_in.shape[0]
    w_qk = jnp.concatenate([w_in[:, :, COL_Q:COL_Q + D_QK][:, :, perm],
                            w_in[:, :, COL_K:COL_K + D_QK][:, :, perm]], axis=2).astype(BF16)
    n_dt = w_in.shape[2] - COL_DT
    w_dt = jnp.concatenate([w_in[:, :, COL_DT:], jnp.zeros((depth, D_MODEL, LANES - n_dt), w_in.dtype)],
                           axis=2).astype(BF16)
    row = lambda v: v[:, None, :]
    pad_lanes = lambda v: row(jnp.concatenate([v, jnp.zeros((depth, LANES - v.shape[1]), v.dtype)], axis=1))
    return {
        "norm_mix_g": row(p["norm_mix_g"]),
        "w_in": w_in.astype(BF16),
        "w_qk": w_qk,
        "w_dt": w_dt,
        "lru_conv_w": p["lru_conv_w"],
        "lru_conv_b": row(p["lru_conv_b"]),
        "lru_wr": jax.vmap(_block_diag)(p["lru_wr"]).astype(BF16),
        "lru_br": row(p["lru_br"]),
        "lru_wi": jax.vmap(_block_diag)(p["lru_wi"]).astype(BF16),
        "lru_bi": row(p["lru_bi"]),
        "lru_lambda": row(p["lru_lambda"]),
        "ret_norm_g": p["ret_norm_g"],
        "ssd_conv_w": p["ssd_conv_w"],
        "ssd_conv_b": row(p["ssd_conv_b"]),
        "ssd_dt_bias": pad_lanes(p["ssd_dt_bias"]),
        "ssd_a_log": pad_lanes(p["ssd_a_log"]),
        "ssd_d_full": row(jnp.repeat(p["ssd_d"], SSD_HD, axis=1)),
        "ssd_norm_g": row(p["ssd_norm_g"]),
        "w_out": p["w_out"].astype(BF16),
        "norm_ffn_g": row(p["norm_ffn_g"]),
        "ffn_w_up": p["ffn_w_up"].astype(BF16),
        "ffn_conv_w": p["ffn_conv_w"],
        "ffn_conv_b": row(p["ffn_conv_b"]),
        "ffn_w_down": p["ffn_w_down"].astype(BF16),
    }


def _tables(seqlen, past_len):
    half = RET_DK // 2
    freqs = ROPE_BASE ** (-np.arange(half, dtype=np.float64) / half)

    def cs(pos):
        ang = pos.astype(np.float64)[:, None] * freqs[None, :]
        return np.tile(np.cos(ang), (1, RET_HEADS)), np.tile(np.sin(ang), (1, RET_HEADS))

    cos_p, sin_p = cs(np.arange(seqlen))
    cos_s, sin_s = cs(past_len + np.arange(1))
    log_g = np.log1p(-np.exp2(-5.0 - np.arange(RET_HEADS, dtype=np.float64)))
    idx = np.arange(CHUNK, dtype=np.float64)
    diff = idx[:, None] - idx[None, :]
    decay = np.where(diff[None] >= 0, np.exp(diff[None] * log_g[:, None, None]), 0.0)
    q_dec = np.exp((idx[None, :] + 1.0) * log_g[:, None])
    k_dec = np.exp((CHUNK - 1.0 - idx[None, :]) * log_g[:, None])
    lanes_of = lambda t: np.tile(np.repeat(t.T, half, axis=1), (1, 2))
    f32 = lambda t: jnp.asarray(np.asarray(t, np.float32))
    hexp = np.arange(LANES)[:, None] == (np.arange(D_SSD)[None, :] // SSD_HD)
    hexp_tile = np.arange(LANES)[:, None] == (np.arange(SSD_HEADS * LANES)[None, :] // LANES)
    rsel = np.broadcast_to(np.arange(LANES)[None, :, None] == np.arange(DEC_TILE)[:, None, None],
                           (DEC_TILE, LANES, LANES))
    return {
        "cos_p": f32(cos_p), "sin_p": f32(sin_p), "cos_s": f32(cos_s), "sin_s": f32(sin_s),
        "decay": f32(decay), "qdec": f32(lanes_of(q_dec)), "kdec": f32(lanes_of(k_dec)),
        "cdec": f32(np.repeat(np.exp(CHUNK * log_g), RET_DV)[None, :]),
        "cdec1": f32(np.exp(log_g)[:, None, None]),
        "tril": f32(np.tril(np.ones((CHUNK, CHUNK)))),
        "eye": f32(np.eye(LANES)).astype(BF16),
        "hexp": f32(hexp),
        "hexp_tile": f32(hexp_tile),
        "rsel": f32(rsel).astype(BF16),
    }


TM_PROMPT = 512
TR_MIX = 512
TR_RET = 2048


def _prompt_stack(x, w, tabs, fg, tm=TM_PROMPT, tr=TR_MIX, tr_ret=TR_RET):
    bsz, seqlen, _ = x.shape
    depth = w["w_in"].shape[0]
    x2d = x.reshape(bsz * seqlen, D_MODEL)
    outs = [[] for _ in range(6)]
    for l in range(depth):
        proj, ya, lconv, lh = _inproj_lru_prompt(x2d, w, l, bsz=bsz, seqlen=seqlen, tm=tm)
        yb, rs = _ret_prompt(proj, bsz, seqlen, w, l, tabs, tr_ret)
        yc, sconv, ss = _ssd_prompt(proj, bsz, seqlen, w, l, tabs, tr)
        x2d, fconv = _ffn_prompt(x2d, ya, yb, yc, w, l, fg, bsz=bsz, seqlen=seqlen, tm=tm,
                                 final_norm=(l == depth - 1))
        for lst, val in zip(outs, (lconv, lh[:, 0, :], rs, sconv, ss, fconv)):
            lst.append(val)
    return x2d.reshape(bsz, seqlen, D_MODEL), [jnp.stack(o, axis=0) for o in outs]


def _sample_stack(x, states, w, tabs, fg):
    nb = x.shape[0]
    depth = w["w_in"].shape[0]
    x2d = x.reshape(nb, D_MODEL)
    st_lconv, st_lh, st_rs, st_sconv, st_ss, st_fconv = states
    lconv_in = st_lconv.reshape(depth, nb, (CONV_W - 1) * D_LRU)
    sconv_in = st_sconv.reshape(depth, nb, (CONV_W - 1) * SSD_CONV_DIM)
    fconv_in = st_fconv.reshape(depth, nb, (FFN_CONV_W - 1) * D_FF)
    outs = [[] for _ in range(4)]
    full = None
    for l in range(depth):
        proj = _inproj(x2d, w, l, tm=nb)
        ya, yb, yc, lconv, lh, rs_full, sconv, ss_full = _mix_decode(
            proj, (lconv_in, st_lh, st_rs, sconv_in, st_ss), l, w, tabs, full)
        full = (rs_full, ss_full)
        x2d, fconv = _ffn_decode(x2d, ya, yb, yc, w, l, fg, fconv_in, final_norm=(l == depth - 1))
        for lst, val in zip(outs, (lconv, lh, sconv, fconv)):
            lst.append(val)
    lconv, lh, sconv, fconv = [jnp.stack(o, axis=0) for o in outs]
    new_states = [lconv.reshape(depth, nb, CONV_W - 1, D_LRU), lh, full[0],
                  sconv.reshape(depth, nb, CONV_W - 1, SSD_CONV_DIM), full[1],
                  fconv.reshape(depth, nb, FFN_CONV_W - 1, D_FF)]
    return x2d.reshape(nb, 1, D_MODEL), new_states


def kernel(x_prompt, x_sample, state_lru_conv, state_lru_h, state_ret, state_ssd_conv, state_ssd, state_ffn_conv,
           norm_mix_g, w_in, lru_conv_w, lru_conv_b, lru_wr, lru_br, lru_wi, lru_bi, lru_lambda,
           ret_norm_g, ssd_conv_w, ssd_conv_b, ssd_dt_bias, ssd_a_log, ssd_d, ssd_norm_g,
           w_out, norm_ffn_g, ffn_w_up, ffn_conv_w, ffn_conv_b, ffn_w_down, norm_final_g):
    p = {"norm_mix_g": norm_mix_g, "w_in": w_in, "lru_conv_w": lru_conv_w, "lru_conv_b": lru_conv_b,
         "lru_wr": lru_wr, "lru_br": lru_br, "lru_wi": lru_wi, "lru_bi": lru_bi, "lru_lambda": lru_lambda,
         "ret_norm_g": ret_norm_g, "ssd_conv_w": ssd_conv_w, "ssd_conv_b": ssd_conv_b,
         "ssd_dt_bias": ssd_dt_bias, "ssd_a_log": ssd_a_log, "ssd_d": ssd_d, "ssd_norm_g": ssd_norm_g,
         "w_out": w_out, "norm_ffn_g": norm_ffn_g, "ffn_w_up": ffn_w_up, "ffn_conv_w": ffn_conv_w,
         "ffn_conv_b": ffn_conv_b, "ffn_w_down": ffn_w_down}
    weights = _prep_weights(p)
    tabs = _tables(x_prompt.shape[1], PAST_LEN)
    fg = norm_final_g[None, :]

    y_p, ns_p = _prompt_stack(x_prompt, weights, tabs, fg)
    y_s, ns_s = _sample_stack(
        x_sample, (state_lru_conv, state_lru_h, state_ret, state_ssd_conv, state_ssd, state_ffn_conv),
        weights, tabs, fg)
    lru_conv_p, lru_h_p, ret_p, ssd_conv_p, ssd_p, ffn_conv_p = ns_p
    lru_conv_s, lru_h_s, ret_s, ssd_conv_s, ssd_s, ffn_conv_s = ns_s
    return (y_p, y_s, lru_conv_p, lru_conv_s, lru_h_p, lru_h_s, ret_p, ret_s,
            ssd_conv_p, ssd_conv_s, ssd_p, ssd_s, ffn_conv_p, ffn_conv_s)
```
